```python
import math
import jax, jax.numpy as jnp
from jax import lax
import numpy as np

D_MODEL = 1024
BATCH = 8
SEQ = 4096
DEPTH = 2

D_MIX = D_MODEL
GROUP_DIM = 64
W_A = D_MIX // 4
W_SSD = D_MIX // 2
W_CONF = D_MIX - W_A - W_SSD
A_CONV = 3
SSD_HEAD_DIM = 64
SSD_HEADS = W_SSD // SSD_HEAD_DIM
SSD_GROUPS = 2
SSD_STATE = 128
SSD_CONV = 4
SSD_CHUNK = 128
SSD_XBC = W_SSD + 2 * SSD_GROUPS * SSD_STATE
CONF_KERNEL = 31
IN_COLS = 3 * W_A + W_SSD + SSD_XBC + SSD_HEADS + 2 * W_CONF
D_FF = ((8 * D_MODEL // 3 + 127) // 128) * 128
N_EXPERTS = 8
TOP_K = 2
D_FF_EXPERT = 7 * D_MODEL // 2
MOE_BLOCK = 128
N_DENSE = (DEPTH + 1) // 2
N_MOE = DEPTH // 2
EPS = 1e-5

_SPLITS = list(np.cumsum([W_A, W_A, W_A, W_SSD, SSD_XBC, SSD_HEADS])[:])

kernel_name = "hybrid_shortconv_ssd_conformer_moe"


def rmsnorm(x, w):
    xf = x.astype(jnp.float32)
    y = xf * lax.rsqrt(jnp.mean(xf * xf, axis=-1, keepdims=True) + EPS)
    return (y * w.astype(jnp.float32)).astype(x.dtype)


def layernorm(x, g, b):
    xf = x.astype(jnp.float32)
    mu = jnp.mean(xf, axis=-1, keepdims=True)
    var = jnp.mean(jnp.square(xf - mu), axis=-1, keepdims=True)
    y = (xf - mu) * lax.rsqrt(var + EPS)
    return (y * g.astype(jnp.float32) + b.astype(jnp.float32)).astype(x.dtype)


def causal_dwconv(u, w, b=None):
    k, c = w.shape
    y = lax.conv_general_dilated(
        u, w[:, None, :].astype(u.dtype), window_strides=(1,), padding=[(k - 1, 0)],
        dimension_numbers=('NWC', 'WIO', 'NWC'), feature_group_count=c)
    return y if b is None else y + b


def ssd_chunked(xh, dt, a, bm, cm):
    bsz, seq = xh.shape[:2]
    nc = seq // SSD_CHUNK
    r = SSD_HEADS // SSD_GROUPS
    f32 = jnp.float32
    xc = xh.reshape(bsz, nc, SSD_CHUNK, SSD_GROUPS, r, SSD_HEAD_DIM).astype(f32)
    dtc = dt.reshape(bsz, nc, SSD_CHUNK, SSD_GROUPS, r)
    bc = bm.reshape(bsz, nc, SSD_CHUNK, SSD_GROUPS, SSD_STATE).astype(f32)
    cc = cm.reshape(bsz, nc, SSD_CHUNK, SSD_GROUPS, SSD_STATE).astype(f32)
    xdt = xc * dtc[..., None]
    a_cum = jnp.cumsum(dtc * a.reshape(SSD_GROUPS, r), axis=2)
    causal = jnp.tril(jnp.ones((SSD_CHUNK, SSD_CHUNK), dtype=bool))
    seg = a_cum[:, :, :, None] - a_cum[:, :, None, :]
    decay_ls = jnp.exp(jnp.where(causal[:, :, None, None], seg, -jnp.inf))
    cb = jnp.einsum('bclgn,bcsgn->bclsg', cc, bc)
    y_diag = jnp.einsum('bclsg,bclsgr,bcsgrp->bclgrp', cb, decay_ls, xdt)
    decay_to_end = jnp.exp(a_cum[:, :, -1:] - a_cum)
    states = jnp.einsum('bclgn,bclgr,bclgrp->bcgrpn', bc, decay_to_end, xdt)
    chunk_decay = jnp.exp(a_cum[:, :, -1])

    def step(h, inp):
        dec, st = inp
        return dec[..., None, None] * h + st, h

    h0 = jnp.zeros_like(states[:, 0])
    _, h_prev = lax.scan(step, h0, (jnp.moveaxis(chunk_decay, 1, 0), jnp.moveaxis(states, 1, 0)))
    h_prev = jnp.moveaxis(h_prev, 0, 1)
    y_off = jnp.einsum('bclgn,bcgrpn,bclgr->bclgrp', cc, h_prev, jnp.exp(a_cum))
    return (y_diag + y_off).reshape(bsz, seq, SSD_HEADS, SSD_HEAD_DIM)


def gated_rmsnorm(y, z, w):
    yz = (y * jax.nn.silu(z)).astype(jnp.float32)
    shp = yz.shape
    yg = yz.reshape(shp[:-1] + (SSD_GROUPS, shp[-1] // SSD_GROUPS))
    yg = yg * lax.rsqrt(jnp.mean(yg * yg, axis=-1, keepdims=True) + EPS)
    return (yg.reshape(shp) * w.astype(jnp.float32)).astype(y.dtype)


def hybrid_mixer(h, w_in, conv_a_w, conv_ssd_w, conv_ssd_b, dt_bias, a_log, d_skip,
                 ssd_norm_w, conv_conf_w, conv_conf_b, conf_ln_g, conf_ln_b, w_out):
    bsz, seq, _ = h.shape
    proj = h @ w_in
    a_b, a_c, a_x, s_z, s_xbc, s_dt, c_glu = jnp.split(proj, _SPLITS, axis=-1)
    y_a = a_b * causal_dwconv(a_c * a_x, conv_a_w)
    xbc = jax.nn.silu(causal_dwconv(s_xbc, conv_ssd_w, conv_ssd_b))
    s_x, s_b, s_c = jnp.split(xbc, [W_SSD, W_SSD + SSD_GROUPS * SSD_STATE], axis=-1)
    dt = jax.nn.softplus(s_dt.astype(jnp.float32) + dt_bias.astype(jnp.float32))
    a = -jnp.exp(a_log.astype(jnp.float32))
    xh = s_x.reshape(bsz, seq, SSD_HEADS, SSD_HEAD_DIM)
    y_s = ssd_chunked(xh, dt, a,
                      s_b.reshape(bsz, seq, SSD_GROUPS, SSD_STATE),
                      s_c.reshape(bsz, seq, SSD_GROUPS, SSD_STATE)).astype(h.dtype)
    y_s = (y_s + d_skip[:, None] * xh).reshape(bsz, seq, W_SSD)
    y_s = gated_rmsnorm(y_s, s_z, ssd_norm_w)
    g = c_glu[..., :W_CONF] * jax.nn.sigmoid(c_glu[..., W_CONF:])
    g = causal_dwconv(g, conv_conf_w, conv_conf_b)
    y_c = jax.nn.silu(layernorm(g, conf_ln_g, conf_ln_b))
    return jnp.concatenate([y_a, y_s, y_c], axis=-1) @ w_out


def swiglu(h, w_gate, w_up, w_down):
    return (jax.nn.silu(h @ w_gate) * (h @ w_up)) @ w_down


def moe_swiglu(h, w_router, w_gate, w_up, w_down):
    bsz, seq, d = h.shape
    t = bsz * seq
    m = t * TOP_K
    hf = h.reshape(t, d)
    logits = (hf @ w_router).astype(jnp.float32)
    top_vals, top_idx = lax.top_k(logits, TOP_K)
    gates = jax.nn.softmax(top_vals, axis=-1)
    flat_e = top_idx.reshape(m)
    order = jnp.argsort(flat_e)
    sorted_e = flat_e[order]
    tok = order // TOP_K
    sizes = jnp.bincount(flat_e, length=N_EXPERTS).astype(jnp.int32)
    padded = ((sizes + MOE_BLOCK - 1) // MOE_BLOCK) * MOE_BLOCK
    ends = jnp.cumsum(padded)
    starts_pad = ends - padded
    starts_sorted = jnp.cumsum(sizes) - sizes
    dest = starts_pad[sorted_e] + (jnp.arange(m, dtype=jnp.int32) - starts_sorted[sorted_e])
    cap = m + N_EXPERTS * MOE_BLOCK
    n_blocks = cap // MOE_BLOCK
    buf = jnp.zeros((cap, d), hf.dtype).at[dest].set(hf[tok])
    blk_e = jnp.minimum(jnp.searchsorted(ends, jnp.arange(n_blocks, dtype=jnp.int32) * MOE_BLOCK,
                                         side='right'), N_EXPERTS - 1)

    def expert_block(args):
        xb, e = args
        return (jax.nn.silu(xb @ w_gate[e]) * (xb @ w_up[e])) @ w_down[e]

    ybuf = lax.map(expert_block, (buf.reshape(n_blocks, MOE_BLOCK, d), blk_e)).reshape(cap, d)
    y = ybuf[dest] * gates.reshape(m)[order][:, None].astype(ybuf.dtype)
    out = jnp.zeros_like(hf).at[tok].add(y)
    return out.reshape(bsz, seq, d)


def setup_inputs(seed: int = 0) -> dict:
    key = jax.random.key(seed)
    ks = jax.random.split(key, 32)
    f32 = jnp.float32

    def nrm(k, shape, scale):
        return jax.random.normal(k, shape, f32) * scale

    def gain(k, shape):
        return 1.0 + 0.02 * jax.random.normal(k, shape, f32)

    dt0 = jnp.exp(jax.random.uniform(ks[5], (DEPTH, SSD_HEADS), f32,
                                     math.log(1e-3), math.log(1e-1)))
    dt_bias = dt0 + jnp.log(-jnp.expm1(-dt0))
    a_log = jnp.log(jax.random.uniform(ks[6], (DEPTH, SSD_HEADS), f32, 1.0, 16.0))
    return {
        'x': jax.random.normal(ks[0], (BATCH, SEQ, D_MODEL), f32),
        'norm_mix': gain(ks[1], (DEPTH, D_MODEL)),
        'w_in': nrm(ks[2], (DEPTH, D_MODEL, IN_COLS), D_MODEL ** -0.5),
        'conv_a_w': nrm(ks[3], (DEPTH, A_CONV, W_A), A_CONV ** -0.5),
        'conv_ssd_w': nrm(ks[4], (DEPTH, SSD_CONV, SSD_XBC), SSD_CONV ** -0.5),
        'conv_ssd_b': nrm(ks[7], (DEPTH, SSD_XBC), 0.02),
        'dt_bias': dt_bias,
        'a_log': a_log,
        'd_skip': 1.0 + 0.1 * jax.random.normal(ks[8], (DEPTH, SSD_HEADS), f32),
        'ssd_norm_w': gain(ks[9], (DEPTH, W_SSD)),
        'conv_conf_w': nrm(ks[10], (DEPTH, CONF_KERNEL, W_CONF), CONF_KERNEL ** -0.5),
        'conv_conf_b': nrm(ks[11], (DEPTH, W_CONF), 0.02),
        'conf_ln_g': gain(ks[12], (DEPTH, W_CONF)),
        'conf_ln_b': nrm(ks[13], (DEPTH, W_CONF), 0.02),
        'w_out': nrm(ks[14], (DEPTH, D_MIX, D_MODEL), D_MIX ** -0.5),
        'norm_ffn': gain(ks[15], (DEPTH, D_MODEL)),
        'ffn_w_gate': nrm(ks[16], (N_DENSE, D_MODEL, D_FF), D_MODEL ** -0.5),
        'ffn_w_up': nrm(ks[17], (N_DENSE, D_MODEL, D_FF), D_MODEL ** -0.5),
        'ffn_w_down': nrm(ks[18], (N_DENSE, D_FF, D_MODEL), D_FF ** -0.5),
        'moe_router': nrm(ks[19], (N_MOE, D_MODEL, N_EXPERTS), D_MODEL ** -0.5),
        'moe_w_gate': nrm(ks[20], (N_MOE, N_EXPERTS, D_MODEL, D_FF_EXPERT), D_MODEL ** -0.5),
        'moe_w_up': nrm(ks[21], (N_MOE, N_EXPERTS, D_MODEL, D_FF_EXPERT), D_MODEL ** -0.5),
        'moe_w_down': nrm(ks[22], (N_MOE, N_EXPERTS, D_FF_EXPERT, D_MODEL), D_FF_EXPERT ** -0.5),
        'norm_final': gain(ks[23], (D_MODEL,)),
    }


def reference(x, norm_mix, w_in, conv_a_w, conv_ssd_w, conv_ssd_b, dt_bias, a_log, d_skip,
              ssd_norm_w, conv_conf_w, conv_conf_b, conf_ln_g, conf_ln_b, w_out, norm_ffn,
              ffn_w_gate, ffn_w_up, ffn_w_down, moe_router, moe_w_gate, moe_w_up, moe_w_down,
              norm_final):
    for i in range(DEPTH):
        h = rmsnorm(x, norm_mix[i])
        x = x + hybrid_mixer(h, w_in[i], conv_a_w[i], conv_ssd_w[i], conv_ssd_b[i], dt_bias[i],
                             a_log[i], d_skip[i], ssd_norm_w[i], conv_conf_w[i], conv_conf_b[i],
                             conf_ln_g[i], conf_ln_b[i], w_out[i])
        h = rmsnorm(x, norm_ffn[i])
        j = i // 2
        if i % 2 == 0:
            x = x + swiglu(h, ffn_w_gate[j], ffn_w_up[j], ffn_w_down[j])
        else:
            x = x + moe_swiglu(h, moe_router[j], moe_w_gate[j], moe_w_up[j], moe_w_down[j])
    return rmsnorm(x, norm_final)
```

```python
import functools

import jax
import jax.numpy as jnp
from jax import lax
from jax.experimental import pallas as pl
from jax.experimental.pallas import tpu as pltpu

F32 = jnp.float32
BF16 = jnp.bfloat16
EPS = 1e-5

LANES = 128
SUBLANES = 8
VMEM_LIMIT_BYTES = 56 * 1024 * 1024

GROUP_DIM = 64
A_CONV = 3
SSD_HEAD_DIM = 64
SSD_GROUPS = 2
SSD_STATE = 128
SSD_CONV = 4
CONF_KERNEL = 31
N_EXPERTS = 8
TOP_K = 2

SSD_CHUNK = 128
CONF_HALO = 32
SMALL_HALO = SUBLANES
CONV_ROWS = 64


def _split3(v):
    hi = v.astype(BF16)
    r1 = v - hi.astype(F32)
    mid = r1.astype(BF16)
    lo = (r1 - mid.astype(F32)).astype(BF16)
    return hi, mid, lo


def _dot(a, b):
    return jnp.dot(a, b, preferred_element_type=F32)


def _dot_nt(a, b):
    return lax.dot_general(a, b, (((1,), (1,)), ((), ())), preferred_element_type=F32)


def _dot_sel_right(v, sel):
    hi, mid, lo = _split3(v)
    return _dot(hi, sel) + _dot(mid, sel) + _dot(lo, sel)


def _dot_sel_left(sel, v):
    hi, mid, lo = _split3(v)
    return _dot(sel, hi) + _dot(sel, mid) + _dot(sel, lo)


def _softplus(v):
    return jnp.maximum(v, 0.0) + jnp.log1p(jnp.exp(-jnp.abs(v)))


def _silu(v):
    return v * jax.nn.sigmoid(v)


def _rmsnorm(x, w):
    return x * lax.rsqrt(jnp.mean(x * x, axis=-1, keepdims=True) + EPS) * w


def _mixer_kernel(dims, x_ref, nw_ref, win_ref, wdtT_ref, caw_ref, csw_ref, csb_ref, dtbc_ref, dtbr_ref,
                  alc_ref, alr_ref, dsk_ref, snw_ref, ccw_ref, ccb_ref, lng_ref, lnb_ref, wout_ref,
                  eexp_ref, e8_ref, out_ref,
                  buf_a, buf_s, buf_c, xact, dtc, dtr, zbuf, ysd, ycat, hstate):
    tl, w_a, w_ssd, w_conf, n_heads = dims
    q = SSD_CHUNK
    n_bc = SSD_GROUPS * SSD_STATE
    hpg = n_heads // SSD_GROUPS
    gw = hpg * SSD_HEAD_DIM
    o_z = 3 * w_a
    o_xbc = o_z + w_ssd
    o_c = o_xbc + w_ssd + 2 * n_bc
    o_dt = o_c + 2 * w_conf
    lt = pl.program_id(1)

    @pl.when(lt == 0)
    def _():
        buf_a[0:SMALL_HALO, :] = jnp.zeros((SMALL_HALO, w_a), F32)
        buf_s[0:SMALL_HALO, :] = jnp.zeros((SMALL_HALO, buf_s.shape[1]), F32)
        buf_c[0:CONF_HALO, :] = jnp.zeros((CONF_HALO, w_conf), F32)
        hstate[...] = jnp.zeros(hstate.shape, F32)

    @pl.when(lt > 0)
    def _():
        buf_a[0:SMALL_HALO, :] = buf_a[tl:tl + SMALL_HALO, :]
        buf_s[0:SMALL_HALO, :] = buf_s[tl:tl + SMALL_HALO, :]
        buf_c[0:CONF_HALO, :] = buf_c[tl:tl + CONF_HALO, :]

    x = x_ref[...]
    hn = _rmsnorm(x, nw_ref[...]).astype(BF16)

    pa = _dot(hn, win_ref[:, 0:3 * w_a])
    buf_a[SMALL_HALO:SMALL_HALO + tl, :] = pa[:, w_a:2 * w_a] * pa[:, 2 * w_a:3 * w_a]
    ya = jnp.zeros((tl, w_a), F32)
    for j in range(A_CONV):
        s = SMALL_HALO - (A_CONV - 1) + j
        ya = ya + caw_ref[j:j + 1, :] * buf_a[s:s + tl, :]
    ycat[:, 0:w_a] = (pa[:, 0:w_a] * ya).astype(BF16)

    pc = _dot(hn, win_ref[:, o_c:o_c + 2 * w_conf])
    buf_c[CONF_HALO:CONF_HALO + tl, :] = pc[:, 0:w_conf] * jax.nn.sigmoid(pc[:, w_conf:2 * w_conf])
    for rb in range(tl // CONV_ROWS):
        acc = jnp.zeros((CONV_ROWS, w_conf), F32) + ccb_ref[...]
        for j in range(CONF_KERNEL):
            s = CONF_HALO - (CONF_KERNEL - 1) + j + rb * CONV_ROWS
            acc = acc + ccw_ref[j:j + 1, :] * buf_c[s:s + CONV_ROWS, :]
        mu = jnp.mean(acc, axis=-1, keepdims=True)
        cen = acc - mu
        var = jnp.mean(cen * cen, axis=-1, keepdims=True)
        yc = cen * lax.rsqrt(var + EPS) * lng_ref[...] + lnb_ref[...]
        ycat[rb * CONV_ROWS:(rb + 1) * CONV_ROWS, w_a + w_ssd:w_a + w_ssd + w_conf] = _silu(yc).astype(BF16)

    zbuf[...] = _dot(hn, win_ref[:, o_z:o_z + w_ssd])
    buf_s[SMALL_HALO:SMALL_HALO + tl, :] = _dot(hn, win_ref[:, o_xbc:o_xbc + w_ssd + 2 * n_bc])
    xc = jnp.zeros((tl, w_ssd + 2 * n_bc), F32) + csb_ref[...]
    for j in range(SSD_CONV):
        s = SMALL_HALO - (SSD_CONV - 1) + j
        xc = xc + csw_ref[j:j + 1, :] * buf_s[s:s + tl, :]
    xact[...] = _silu(xc)
    dtc[...] = _dot(hn, win_ref[:, o_dt:o_dt + LANES])
    dtr[...] = _dot_nt(wdtT_ref[...], hn)

    a_col = -jnp.exp(alc_ref[...])
    a_row = -jnp.exp(alr_ref[...])
    ri = lax.broadcasted_iota(jnp.int32, (q, q), 0)
    ci = lax.broadcasted_iota(jnp.int32, (q, q), 1)
    causal = ri >= ci
    tril = jnp.where(causal, 1.0, 0.0).astype(BF16)
    triu = jnp.where(ri <= ci, 1.0, 0.0).astype(BF16)

    def chunk(c, carry):
        r0 = pl.multiple_of(c * q, q)
        rows = pl.ds(r0, q)
        xs = xact[rows, 0:w_ssd]
        dt_c = _softplus(dtc[rows, :] + dtbc_ref[...])
        dt_r = _softplus(dtr[:, rows] + dtbr_ref[...])
        acum_c = _dot_sel_left(tril, dt_c * a_col)
        acum_r = _dot_sel_right(dt_r * a_row, triu)
        a_end = acum_c[q - 1:q, :]
        stack = jnp.concatenate([dt_c, dt_c * jnp.exp(a_end - acum_c), jnp.exp(acum_c)], axis=0)
        ex = _dot_sel_right(stack, eexp_ref[...])
        xdt = (xs * ex[0:q, :]).astype(BF16)
        xst = (xs * ex[q:2 * q, :]).astype(BF16)
        ea_x = ex[2 * q:3 * q, :]
        for g in range(SSD_GROUPS):
            bg = xact[rows, w_ssd + g * SSD_STATE:w_ssd + (g + 1) * SSD_STATE]
            cg = xact[rows, w_ssd + n_bc + g * SSD_STATE:w_ssd + n_bc + (g + 1) * SSD_STATE].astype(BF16)
            cb = _dot_nt(cg, bg.astype(BF16))
            hg = hstate[g]
            yoff = _dot(cg, hg.astype(BF16)) * ea_x[:, g * gw:(g + 1) * gw]
            for r in range(hpg):
                h = g * hpg + r
                colb = _dot_sel_right(acum_c, e8_ref[:, h * q:(h + 1) * q])
                seg = colb - acum_r[h:h + 1, :]
                dec = jnp.exp(jnp.where(causal, seg, -jnp.inf))
                yd = _dot((cb * dec).astype(BF16), xdt[:, h * SSD_HEAD_DIM:(h + 1) * SSD_HEAD_DIM])
                ysd[rows, h * SSD_HEAD_DIM:(h + 1) * SSD_HEAD_DIM] = (
                    yd + yoff[:, r * SSD_HEAD_DIM:(r + 1) * SSD_HEAD_DIM])
            st = _dot(bg.T.astype(BF16), xst[:, g * gw:(g + 1) * gw])
            hstate[g] = ea_x[q - 1:q, g * gw:(g + 1) * gw] * hg + st
        yz = (ysd[rows, :] + dsk_ref[...] * xs) * _silu(zbuf[rows, :])
        for g in range(SSD_GROUPS):
            yg = yz[:, g * gw:(g + 1) * gw]
            yn = yg * lax.rsqrt(jnp.mean(yg * yg, axis=-1, keepdims=True) + EPS) * snw_ref[:, g * gw:(g + 1) * gw]
            ycat[rows, w_a + g * gw:w_a + (g + 1) * gw] = yn.astype(BF16)
        return carry

    lax.fori_loop(0, tl // q, chunk, 0)

    out_ref[...] = x + _dot(ycat[...], wout_ref[...])


def _mixer_layer(x, norm_w, w_in, conv_a_w, conv_ssd_w, conv_ssd_b, dt_bias, a_log, d_skip, ssd_norm_w,
                 conv_conf_w, conv_conf_b, conf_ln_g, conf_ln_b, w_out, tl):
    bsz, seq, d = x.shape
    w_a = conv_a_w.shape[1]
    w_ssd = ssd_norm_w.shape[0]
    w_conf = conv_conf_w.shape[1]
    n_heads = dt_bias.shape[0]
    n_bc = SSD_GROUPS * SSD_STATE
    xbc = w_ssd + 2 * n_bc
    assert n_heads == SUBLANES and n_heads * SSD_HEAD_DIM == w_ssd and seq % tl == 0 and tl % SSD_CHUNK == 0
    assert tl % CONV_ROWS == 0 and conv_a_w.shape[0] == A_CONV and conv_ssd_w.shape[0] == SSD_CONV
    assert conv_conf_w.shape[0] == CONF_KERNEL
    o_dt = 3 * w_a + w_ssd + xbc
    w_dt = w_in[:, o_dt:o_dt + n_heads]
    w_main = jnp.concatenate(
        [w_in[:, :o_dt], w_in[:, o_dt + n_heads:], jnp.pad(w_dt, ((0, 0), (0, LANES - n_heads)))], axis=1).astype(BF16)
    n_cols = w_main.shape[1]
    w_dt_t = w_dt.T.astype(BF16)
    row = lambda v: v.reshape(1, -1).astype(F32)
    pad_lanes = lambda v: jnp.pad(v.reshape(1, -1).astype(F32), ((0, 0), (0, LANES - v.shape[0])))
    col = lambda v: v.reshape(-1, 1).astype(F32)
    head_of_channel = jnp.arange(w_ssd) // SSD_HEAD_DIM
    e_exp = (jnp.arange(LANES)[:, None] == head_of_channel[None, :]).astype(BF16)
    e8 = (jnp.arange(LANES)[:, None] == (jnp.arange(n_heads * SSD_CHUNK) // SSD_CHUNK)[None, :]).astype(BF16)
    d_skip_x = jnp.repeat(d_skip.astype(F32), SSD_HEAD_DIM).reshape(1, w_ssd)

    const = lambda shape: pl.BlockSpec(shape, lambda b, l: (0,) * len(shape))
    dims = (tl, w_a, w_ssd, w_conf, n_heads)
    return pl.pallas_call(
        functools.partial(_mixer_kernel, dims),
        out_shape=jax.ShapeDtypeStruct((bsz, seq, d), F32),
        grid=(bsz, seq // tl),
        in_specs=[
            pl.BlockSpec((None, tl, d), lambda b, l: (b, l, 0)),
            const((1, d)), const((d, n_cols)), const((n_heads, d)),
            const((A_CONV, w_a)), const((SSD_CONV, xbc)), const((1, xbc)),
            const((1, LANES)), const((n_heads, 1)), const((1, LANES)), const((n_heads, 1)),
            const((1, w_ssd)), const((1, w_ssd)),
            const((CONF_KERNEL, w_conf)), const((1, w_conf)), const((1, w_conf)), const((1, w_conf)),
            const((d, d)), const((LANES, w_ssd)), const((LANES, n_heads * SSD_CHUNK)),
        ],
        out_specs=pl.BlockSpec((None, tl, d), lambda b, l: (b, l, 0)),
        scratch_shapes=[
            pltpu.VMEM((SMALL_HALO + tl, w_a), F32),
            pltpu.VMEM((SMALL_HALO + tl, xbc), F32),
            pltpu.VMEM((CONF_HALO + tl, w_conf), F32),
            pltpu.VMEM((tl, xbc), F32),
            pltpu.VMEM((tl, LANES), F32),
            pltpu.VMEM((n_heads, tl), F32),
            pltpu.VMEM((tl, w_ssd), F32),
            pltpu.VMEM((tl, w_ssd), F32),
            pltpu.VMEM((tl, d), BF16),
            pltpu.VMEM((SSD_GROUPS, SSD_STATE, w_ssd // SSD_GROUPS), F32),
        ],
        compiler_params=pltpu.CompilerParams(
            dimension_semantics=("arbitrary", "arbitrary"), vmem_limit_bytes=VMEM_LIMIT_BYTES),
        name="mixer",
    )(x, row(norm_w), w_main, w_dt_t, conv_a_w.astype(F32), conv_ssd_w.astype(F32), row(conv_ssd_b),
      pad_lanes(dt_bias), col(dt_bias), pad_lanes(a_log), col(a_log), d_skip_x, row(ssd_norm_w),
      conv_conf_w.astype(F32), row(conv_conf_b), row(conf_ln_g), row(conf_ln_b), w_out.astype(BF16), e_exp, e8)


def _dense_ffn_kernel(final_norm, x_ref, nw_ref, wg_ref, wu_ref, wd_ref, fw_ref, out_ref, hn, acc):
    j = pl.program_id(1)

    @pl.when(j == 0)
    def _():
        hn[...] = _rmsnorm(x_ref[...], nw_ref[...]).astype(BF16)
        acc[...] = jnp.zeros(acc.shape, F32)

    h = hn[...]
    a = (_silu(_dot(h, wg_ref[...])) * _dot(h, wu_ref[...])).astype(BF16)
    acc[...] += _dot(a, wd_ref[...])

    @pl.when(j == pl.num_programs(1) - 1)
    def _():
        y = x_ref[...] + acc[...]
        out_ref[...] = _rmsnorm(y, fw_ref[...]) if final_norm else y


def _dense_ffn(x2, norm_w, w_gate, w_up, w_down, final_w, tm, tf):
    t, d = x2.shape
    ff = w_gate.shape[1]
    assert t % tm == 0 and ff % tf == 0
    final_norm = final_w is not None
    fw = (final_w if final_norm else jnp.ones((d,), F32)).reshape(1, d).astype(F32)
    return pl.pallas_call(
        functools.partial(_dense_ffn_kernel, final_norm),
        out_shape=jax.ShapeDtypeStruct((t, d), F32),
        grid=(t // tm, ff // tf),
        in_specs=[
            pl.BlockSpec((tm, d), lambda i, j: (i, 0)),
            pl.BlockSpec((1, d), lambda i, j: (0, 0)),
            pl.BlockSpec((d, tf), lambda i, j: (0, j)),
            pl.BlockSpec((d, tf), lambda i, j: (0, j)),
            pl.BlockSpec((tf, d), lambda i, j: (j, 0)),
            pl.BlockSpec((1, d), lambda i, j: (0, 0)),
        ],
        out_specs=pl.BlockSpec((tm, d), lambda i, j: (i, 0)),
        scratch_shapes=[pltpu.VMEM((tm, d), BF16), pltpu.VMEM((tm, d), F32)],
        compiler_params=pltpu.CompilerParams(
            dimension_semantics=("arbitrary", "arbitrary"), vmem_limit_bytes=VMEM_LIMIT_BYTES),
        name="dense_ffn",
    )(x2, norm_w.reshape(1, d).astype(F32), w_gate.astype(BF16), w_up.astype(BF16), w_down.astype(BF16), fw)


def _expert_ffn_kernel(blk_e_ref, nact_ref, xb_ref, wg_ref, wu_ref, wd_ref, out_ref, hb, acc):
    i = pl.program_id(0)
    j = pl.program_id(1)

    @pl.when(i < nact_ref[0])
    def _():
        @pl.when(j == 0)
        def _():
            hb[...] = xb_ref[...].astype(BF16)
            acc[...] = jnp.zeros(acc.shape, F32)

        h = hb[...]
        a = (_silu(_dot(h, wg_ref[...])) * _dot(h, wu_ref[...])).astype(BF16)
        acc[...] += _dot(a, wd_ref[...])

        @pl.when(j == pl.num_programs(1) - 1)
        def _():
            out_ref[...] = acc[...]

    @pl.when((i >= nact_ref[0]) & (j == 0))
    def _():
        out_ref[...] = jnp.zeros(out_ref.shape, F32)


def _expert_ffn(buf, blk_e, nact, w_gate, w_up, w_down, bm, tf):
    cap, d = buf.shape
    ff = w_gate.shape[2]
    n_blocks = cap // bm
    n_j = ff // tf
    assert cap % bm == 0 and ff % tf == 0
    row_blk = lambda i, j, be, na: (jnp.minimum(i, na[0] - 1), 0)
    col_j = lambda i, j, na: jnp.where(i < na[0], j, n_j - 1)
    return pl.pallas_call(
        _expert_ffn_kernel,
        out_shape=jax.ShapeDtypeStruct((cap, d), F32),
        grid_spec=pltpu.PrefetchScalarGridSpec(
            num_scalar_prefetch=2,
            grid=(n_blocks, n_j),
            in_specs=[
                pl.BlockSpec((bm, d), row_blk),
                pl.BlockSpec((None, d, tf), lambda i, j, be, na: (be[i], 0, col_j(i, j, na))),
                pl.BlockSpec((None, d, tf), lambda i, j, be, na: (be[i], 0, col_j(i, j, na))),
                pl.BlockSpec((None, tf, d), lambda i, j, be, na: (be[i], col_j(i, j, na), 0)),
            ],
            out_specs=pl.BlockSpec((bm, d), lambda i, j, be, na: (i, 0)),
            scratch_shapes=[pltpu.VMEM((bm, d), BF16), pltpu.VMEM((bm, d), F32)],
        ),
        compiler_params=pltpu.CompilerParams(
            dimension_semantics=("arbitrary", "arbitrary"), vmem_limit_bytes=VMEM_LIMIT_BYTES),
        name="expert_ffn",
    )(blk_e, nact, buf, w_gate.astype(BF16), w_up.astype(BF16), w_down.astype(BF16))


def _router_kernel(x_ref, nw_ref, wr_ref, hn_ref, route_ref):
    h = _rmsnorm(x_ref[...], nw_ref[...])
    hn_ref[...] = h
    h3 = _split3(h)
    w3 = _split3(wr_ref[...])
    logits = jnp.zeros((h.shape[0], LANES), F32)
    for a in range(3):
        for b in range(3 - a):
            logits = logits + _dot(h3[a], w3[b])
    lane = lax.broadcasted_iota(jnp.int32, logits.shape, 1)
    neg = jnp.float32(-jnp.inf)
    logits = jnp.where(lane < N_EXPERTS, logits, neg)
    m1 = jnp.max(logits, axis=-1, keepdims=True)
    i1 = jnp.min(jnp.where(logits == m1, lane, LANES), axis=-1, keepdims=True)
    rest = jnp.where(lane == i1, neg, logits)
    m2 = jnp.max(rest, axis=-1, keepdims=True)
    i2 = jnp.min(jnp.where(rest == m2, lane, LANES), axis=-1, keepdims=True)
    e2 = jnp.exp(m2 - m1)
    g1 = 1.0 / (1.0 + e2)
    g2 = e2 * g1
    route = jnp.where(lane == 0, i1.astype(F32), jnp.where(lane == 1, i2.astype(F32),
                      jnp.where(lane == 2, g1, jnp.where(lane == 3, g2, 0.0))))
    route_ref[...] = route


def _router(x2, norm_w, w_router, tm):
    t, d = x2.shape
    n_e = w_router.shape[1]
    assert n_e == N_EXPERTS and t % tm == 0
    wr = jnp.pad(w_router.astype(F32), ((0, 0), (0, LANES - n_e)))
    return pl.pallas_call(
        _router_kernel,
        out_shape=(jax.ShapeDtypeStruct((t, d), F32), jax.ShapeDtypeStruct((t, LANES), F32)),
        grid=(t // tm,),
        in_specs=[
            pl.BlockSpec((tm, d), lambda i: (i, 0)),
            pl.BlockSpec((1, d), lambda i: (0, 0)),
            pl.BlockSpec((d, LANES), lambda i: (0, 0)),
        ],
        out_specs=(pl.BlockSpec((tm, d), lambda i: (i, 0)), pl.BlockSpec((tm, LANES), lambda i: (i, 0))),
        compiler_params=pltpu.CompilerParams(
            dimension_semantics=("arbitrary",), vmem_limit_bytes=VMEM_LIMIT_BYTES),
        name="router",
    )(x2, norm_w.reshape(1, d).astype(F32), wr)


def _row_copy(src_ref, src_row, dst_ref, dst_row, sem):
    return pltpu.make_async_copy(src_ref.at[pl.ds(src_row, 1)], dst_ref.at[pl.ds(dst_row, 1)], sem)


def _tile_indices(dest, tile):
    k, t = dest.shape
    return dest.reshape(k, t // tile, tile).transpose(1, 0, 2).reshape(t // tile, 1, k * tile)


def _scatter_kernel(ts, dest_ref, hn_ref, buf_in_ref, buf_ref, sem):
    del buf_in_ref

    def issue(r, carry):
        for k in range(TOP_K):
            _row_copy(hn_ref, r, buf_ref, dest_ref[0, 0, k * ts + r], sem).start()
        return carry

    lax.fori_loop(0, ts, issue, 0)

    def drain(r, carry):
        for k in range(TOP_K):
            _row_copy(hn_ref, 0, buf_ref, 0, sem).wait()
        return carry

    lax.fori_loop(0, ts, drain, 0)


def _scatter_rows(hn, dest, cap, ts):
    t, d = hn.shape
    assert t % ts == 0
    return pl.pallas_call(
        functools.partial(_scatter_kernel, ts),
        out_shape=jax.ShapeDtypeStruct((cap, d), F32),
        grid=(t // ts,),
        in_specs=[
            pl.BlockSpec((1, 1, TOP_K * ts), lambda i: (i, 0, 0), memory_space=pltpu.SMEM),
            pl.BlockSpec((ts, d), lambda i: (i, 0)),
            pl.BlockSpec(memory_space=pl.ANY),
        ],
        out_specs=pl.BlockSpec(memory_space=pl.ANY),
        scratch_shapes=[pltpu.SemaphoreType.DMA(())],
        input_output_aliases={2: 0},
        compiler_params=pltpu.CompilerParams(
            dimension_semantics=("arbitrary",), vmem_limit_bytes=VMEM_LIMIT_BYTES),
        name="scatter_rows",
    )(_tile_indices(dest, ts), hn, jnp.zeros((cap, d), F32))


def _combine_kernel(tc, final_norm, dest_ref, x_ref, route_ref, fw_ref, ybuf_ref, out_ref, rows, sem):
    def issue(r, carry):
        for k in range(TOP_K):
            _row_copy(ybuf_ref, dest_ref[0, 0, k * tc + r], rows.at[k], r, sem).start()
        return carry

    lax.fori_loop(0, tc, issue, 0)

    def drain(r, carry):
        for k in range(TOP_K):
            _row_copy(ybuf_ref, 0, rows.at[k], 0, sem).wait()
        return carry

    lax.fori_loop(0, tc, drain, 0)

    route = route_ref[...]
    y = x_ref[...]
    for k in range(TOP_K):
        y = y + route[:, TOP_K + k:TOP_K + k + 1] * rows[k]
    out_ref[...] = _rmsnorm(y, fw_ref[...]) if final_norm else y


def _combine(x2, route, dest, ybuf, final_w, tc):
    t, d = x2.shape
    assert t % tc == 0
    final_norm = final_w is not None
    fw = (final_w if final_norm else jnp.ones((d,), F32)).reshape(1, d).astype(F32)
    return pl.pallas_call(
        functools.partial(_combine_kernel, tc, final_norm),
        out_shape=jax.ShapeDtypeStruct((t, d), F32),
        grid=(t // tc,),
        in_specs=[
            pl.BlockSpec((1, 1, TOP_K * tc), lambda i: (i, 0, 0), memory_space=pltpu.SMEM),
            pl.BlockSpec((tc, d), lambda i: (i, 0)),
            pl.BlockSpec((tc, LANES), lambda i: (i, 0)),
            pl.BlockSpec((1, d), lambda i: (0, 0)),
            pl.BlockSpec(memory_space=pl.ANY),
        ],
        out_specs=pl.BlockSpec((tc, d), lambda i: (i, 0)),
        scratch_shapes=[pltpu.VMEM((TOP_K, tc, d), F32), pltpu.SemaphoreType.DMA(())],
        compiler_params=pltpu.CompilerParams(
            dimension_semantics=("arbitrary",), vmem_limit_bytes=VMEM_LIMIT_BYTES),
        name="combine",
    )(_tile_indices(dest, tc), x2, route, fw, ybuf)


def _moe_layer(x2, norm_w, w_router, w_gate, w_up, w_down, final_w, tiles):
    t, d = x2.shape
    bm = tiles["moe_bm"]
    hn, route = _router(x2, norm_w, w_router, tiles["router_tm"])
    e_idx = route[:, :TOP_K].astype(jnp.int32).T
    onehot = (e_idx.reshape(-1)[None, :] == jnp.arange(N_EXPERTS, dtype=jnp.int32)[:, None]).astype(jnp.int32)
    csum = jnp.cumsum(onehot, axis=1)
    rank = jnp.sum(csum * onehot, axis=0) - 1
    sizes = csum[:, -1]
    padded = ((sizes + bm - 1) // bm) * bm
    ends = jnp.cumsum(padded)
    starts = ends - padded
    dest = (jnp.sum(starts[:, None] * onehot, axis=0) + rank).astype(jnp.int32).reshape(TOP_K, t)
    cap = TOP_K * t + N_EXPERTS * bm
    n_blocks = cap // bm
    blk_e = jnp.minimum(
        jnp.searchsorted(ends, jnp.arange(n_blocks, dtype=jnp.int32) * bm, side="right"), N_EXPERTS - 1
    ).astype(jnp.int32)
    nact = (ends[-1:] // bm).astype(jnp.int32)

    buf = _scatter_rows(hn, dest, cap, tiles["scatter_ts"])
    ybuf = _expert_ffn(buf, blk_e, nact, w_gate, w_up, w_down, bm, tiles["moe_tf"])
    return _combine(x2, route, dest, ybuf, final_w, tiles["combine_tc"])


TILES = {
    "mixer_tl": 256,
    "ffn_tm": 512, "ffn_tf": 256,
    "router_tm": 512,
    "moe_bm": 512, "moe_tf": 512,
    "scatter_ts": 512,
    "combine_tc": 256,
}


def kernel(x, norm_mix, w_in, conv_a_w, conv_ssd_w, conv_ssd_b, dt_bias, a_log, d_skip, ssd_norm_w, conv_conf_w, conv_conf_b, conf_ln_g, conf_ln_b, w_out, norm_ffn, ffn_w_gate, ffn_w_up, ffn_w_down, moe_router, moe_w_gate, moe_w_up, moe_w_down, norm_final):
    bsz, seq, d = x.shape
    depth = norm_mix.shape[0]
    tiles = dict(TILES)
    tiles["mixer_tl"] = min(tiles["mixer_tl"], seq)
    for i in range(depth):
        x = _mixer_layer(x, norm_mix[i], w_in[i], conv_a_w[i], conv_ssd_w[i], conv_ssd_b[i], dt_bias[i], a_log[i],
                         d_skip[i], ssd_norm_w[i], conv_conf_w[i], conv_conf_b[i], conf_ln_g[i], conf_ln_b[i],
                         w_out[i], tiles["mixer_tl"])
        x2 = x.reshape(bsz * seq, d)
        final_w = norm_final if i == depth - 1 else None
        j = i // 2
        if i % 2 == 0:
            x2 = _dense_ffn(x2, norm_ffn[i], ffn_w_gate[j], ffn_w_up[j], ffn_w_down[j], final_w,
                            tiles["ffn_tm"], tiles["ffn_tf"])
        else:
            x2 = _moe_layer(x2, norm_ffn[i], moe_router[j], moe_w_gate[j], moe_w_up[j], moe_w_down[j], final_w, tiles)
        x = x2.reshape(bsz, seq, d)
    return x
```

```python
import functools

import jax
import jax.numpy as jnp
from jax import lax
from jax.experimental import pallas as pl
from jax.experimental.pallas import tpu as pltpu

F32 = jnp.float32
BF16 = jnp.bfloat16
EPS = 1e-5

LANES = 128
SUBLANES = 8
VMEM_LIMIT_BYTES = 56 * 1024 * 1024

GROUP_DIM = 64
A_CONV = 3
SSD_HEAD_DIM = 64
SSD_GROUPS = 2
SSD_STATE = 128
SSD_CONV = 4
CONF_KERNEL = 31
N_EXPERTS = 8
TOP_K = 2

SSD_CHUNK = 128
CONF_HALO = 32
SMALL_HALO = SUBLANES
CONV_ROWS = 32
ROW_BLOCK = 64
LANE_BLOCK = 256


def _split3(v):
    hi = v.astype(BF16)
    r1 = v - hi.astype(F32)
    mid = r1.astype(BF16)
    lo = (r1 - mid.astype(F32)).astype(BF16)
    return hi, mid, lo


def _dot(a, b):
    return jnp.dot(a, b, preferred_element_type=F32)


def _dot_nt(a, b):
    return lax.dot_general(a, b, (((1,), (1,)), ((), ())), preferred_element_type=F32)


def _dot_sel_right(v, sel):
    hi, mid, lo = _split3(v)
    return _dot(hi, sel) + _dot(mid, sel) + _dot(lo, sel)


def _dot_sel_left(sel, v):
    hi, mid, lo = _split3(v)
    return _dot(sel, hi) + _dot(sel, mid) + _dot(sel, lo)


def _softplus(v):
    return jnp.maximum(v, 0.0) + jnp.log1p(jnp.exp(-jnp.abs(v)))


def _silu(v):
    return v * jax.nn.sigmoid(v)


def _rmsnorm(x, w):
    return x * lax.rsqrt(jnp.mean(x * x, axis=-1, keepdims=True) + EPS) * w


def _mixer_kernel(dims, x_ref, nw_ref, win_ref, wdtT_ref, caw_ref, csw_ref, csb_ref, dtbc_ref, dtbr_ref,
                  alc_ref, alr_ref, dsk_ref, snw_ref, ccw_ref, ccb_ref, lng_ref, lnb_ref, wout_ref,
                  out_ref,
                  hn_s, proj, buf_a, buf_c, shc, xact, dtr, ysd, ycat, hstate):
    tl, w_a, w_ssd, w_conf, n_heads = dims
    q = SSD_CHUNK
    n_bc = SSD_GROUPS * SSD_STATE
    xbc = w_ssd + 2 * n_bc
    hpg = n_heads // SSD_GROUPS
    gw = hpg * SSD_HEAD_DIM
    hb = SMALL_HALO
    o_z = 3 * w_a
    o_xbc = o_z + w_ssd
    o_c = o_xbc + xbc
    o_dt = o_c + 2 * w_conf
    lt = pl.program_id(1)

    @pl.when(lt == 0)
    def _():
        buf_a[0:hb, :] = jnp.zeros((hb, w_a), F32)
        proj[0:hb, o_xbc:o_xbc + xbc] = jnp.zeros((hb, xbc), F32)
        buf_c[0:CONF_HALO, :] = jnp.zeros((CONF_HALO, w_conf), F32)
        hstate[...] = jnp.zeros(hstate.shape, F32)

    @pl.when(lt > 0)
    def _():
        buf_a[0:hb, :] = buf_a[tl:tl + hb, :]
        proj[0:hb, o_xbc:o_xbc + xbc] = proj[tl:tl + hb, o_xbc:o_xbc + xbc]
        buf_c[0:CONF_HALO, :] = buf_c[tl:tl + CONF_HALO, :]

    hn_s[...] = _rmsnorm(x_ref[...], nw_ref[...]).astype(BF16)
    proj[hb:hb + tl, :] = _dot(hn_s[...], win_ref[...])
    dtr[...] = _dot_nt(wdtT_ref[...], hn_s[...])

    for rb in range(tl // ROW_BLOCK):
        r0 = hb + rb * ROW_BLOCK
        buf_a[r0:r0 + ROW_BLOCK, :] = proj[r0:r0 + ROW_BLOCK, w_a:2 * w_a] * proj[r0:r0 + ROW_BLOCK, 2 * w_a:3 * w_a]
        ya = jnp.zeros((ROW_BLOCK, w_a), F32)
        for j in range(A_CONV):
            s = r0 - (A_CONV - 1) + j
            ya = ya + caw_ref[j:j + 1, :] * buf_a[s:s + ROW_BLOCK, :]
        ycat[rb * ROW_BLOCK:(rb + 1) * ROW_BLOCK, 0:w_a] = (proj[r0:r0 + ROW_BLOCK, 0:w_a] * ya).astype(BF16)

    for rb in range(tl // ROW_BLOCK):
        r0 = hb + rb * ROW_BLOCK
        buf_c[CONF_HALO + rb * ROW_BLOCK:CONF_HALO + (rb + 1) * ROW_BLOCK, :] = (
            proj[r0:r0 + ROW_BLOCK, o_c:o_c + w_conf]
            * jax.nn.sigmoid(proj[r0:r0 + ROW_BLOCK, o_c + w_conf:o_c + 2 * w_conf]))
    n_sh = shc.shape[1]
    for r in range(1, SUBLANES):
        shc[r - 1] = buf_c[r:r + n_sh, :]
    for rb in range(tl // CONV_ROWS):
        acc = jnp.zeros((CONV_ROWS, w_conf), F32) + ccb_ref[...]
        for j in range(CONF_KERNEL):
            a, r = divmod(CONF_HALO - (CONF_KERNEL - 1) + j, SUBLANES)
            base = a * SUBLANES + rb * CONV_ROWS
            src = buf_c[base:base + CONV_ROWS, :] if r == 0 else shc[r - 1, base:base + CONV_ROWS, :]
            acc = acc + ccw_ref[j:j + 1, :] * src
        mu = jnp.mean(acc, axis=-1, keepdims=True)
        cen = acc - mu
        var = jnp.mean(cen * cen, axis=-1, keepdims=True)
        yc = cen * lax.rsqrt(var + EPS) * lng_ref[...] + lnb_ref[...]
        ycat[rb * CONV_ROWS:(rb + 1) * CONV_ROWS, w_a + w_ssd:w_a + w_ssd + w_conf] = _silu(yc).astype(BF16)

    for rb in range(tl // ROW_BLOCK):
        for lb in range(xbc // LANE_BLOCK):
            cols = slice(lb * LANE_BLOCK, (lb + 1) * LANE_BLOCK)
            pcols = slice(o_xbc + lb * LANE_BLOCK, o_xbc + (lb + 1) * LANE_BLOCK)
            xc = jnp.zeros((ROW_BLOCK, LANE_BLOCK), F32) + csb_ref[:, cols]
            for j in range(SSD_CONV):
                s = hb + rb * ROW_BLOCK - (SSD_CONV - 1) + j
                xc = xc + csw_ref[j:j + 1, cols] * proj[s:s + ROW_BLOCK, pcols]
            xact[rb * ROW_BLOCK:(rb + 1) * ROW_BLOCK, cols] = _silu(xc)

    a_col = -jnp.exp(alc_ref[...])
    a_row = -jnp.exp(alr_ref[...])
    ri = lax.broadcasted_iota(jnp.int32, (q, q), 0)
    ci = lax.broadcasted_iota(jnp.int32, (q, q), 1)
    causal = ri >= ci
    tril = jnp.where(causal, 1.0, 0.0).astype(BF16)
    triu = jnp.where(ri <= ci, 1.0, 0.0).astype(BF16)
    hd = SSD_HEAD_DIM

    for c in range(tl // q):
        rows = slice(c * q, (c + 1) * q)
        prow = slice(hb + c * q, hb + (c + 1) * q)
        dt_c = _softplus(proj[prow, o_dt:o_dt + LANES] + dtbc_ref[...])
        dt_r = _softplus(dtr[:, rows] + dtbr_ref[...])
        acum_c = _dot_sel_left(tril, dt_c * a_col)
        acum_r = _dot_sel_right(dt_r * a_row, triu)
        to_end_r = dt_r * jnp.exp(acum_r[:, q - 1:q] - acum_r)
        for g in range(SSD_GROUPS):
            bg = xact[rows, w_ssd + g * SSD_STATE:w_ssd + (g + 1) * SSD_STATE]
            cg = xact[rows, w_ssd + n_bc + g * SSD_STATE:w_ssd + n_bc + (g + 1) * SSD_STATE].astype(BF16)
            cb = _dot_nt(cg, bg.astype(BF16))
            bgt = bg.T
            hg = hstate[g]
            yoff = _dot(cg, hg.astype(BF16))
            for r in range(hpg):
                h = g * hpg + r
                xh = xact[rows, h * hd:(h + 1) * hd].astype(BF16)
                acol = jnp.broadcast_to(acum_c[:, h:h + 1], (q, q))
                dec = jnp.where(causal, jnp.exp(acol - acum_r[h:h + 1, :]), 0.0)
                yd = _dot((cb * dec * dt_r[h:h + 1, :]).astype(BF16), xh)
                ysd[rows, h * hd:(h + 1) * hd] = yd + yoff[:, r * hd:(r + 1) * hd] * jnp.exp(acol[:, 0:hd])
                st = _dot((bgt * to_end_r[h:h + 1, :]).astype(BF16), xh)
                hstate[g, :, r * hd:(r + 1) * hd] = jnp.exp(acol[q - 1:q, 0:hd]) * hg[:, r * hd:(r + 1) * hd] + st
        xs = xact[rows, 0:w_ssd]
        yz = (ysd[rows, :] + dsk_ref[...] * xs) * _silu(proj[prow, o_z:o_z + w_ssd])
        for g in range(SSD_GROUPS):
            yg = yz[:, g * gw:(g + 1) * gw]
            yn = yg * lax.rsqrt(jnp.mean(yg * yg, axis=-1, keepdims=True) + EPS) * snw_ref[:, g * gw:(g + 1) * gw]
            ycat[rows, w_a + g * gw:w_a + (g + 1) * gw] = yn.astype(BF16)

    out_ref[...] = x_ref[...] + _dot(ycat[...], wout_ref[...])


def _mixer_layer(x, norm_w, w_in, conv_a_w, conv_ssd_w, conv_ssd_b, dt_bias, a_log, d_skip, ssd_norm_w,
                 conv_conf_w, conv_conf_b, conf_ln_g, conf_ln_b, w_out, tl):
    bsz, seq, d = x.shape
    w_a = conv_a_w.shape[1]
    w_ssd = ssd_norm_w.shape[0]
    w_conf = conv_conf_w.shape[1]
    n_heads = dt_bias.shape[0]
    n_bc = SSD_GROUPS * SSD_STATE
    xbc = w_ssd + 2 * n_bc
    assert n_heads == SUBLANES and n_heads * SSD_HEAD_DIM == w_ssd and seq % tl == 0 and tl % SSD_CHUNK == 0
    assert tl % ROW_BLOCK == 0 and xbc % LANE_BLOCK == 0 and conv_a_w.shape[0] == A_CONV and conv_ssd_w.shape[0] == SSD_CONV
    assert conv_conf_w.shape[0] == CONF_KERNEL
    o_dt = 3 * w_a + w_ssd + xbc
    w_dt = w_in[:, o_dt:o_dt + n_heads]
    w_main = jnp.concatenate(
        [w_in[:, :o_dt], w_in[:, o_dt + n_heads:], jnp.pad(w_dt, ((0, 0), (0, LANES - n_heads)))], axis=1).astype(BF16)
    n_cols = w_main.shape[1]
    w_dt_t = w_dt.T.astype(BF16)
    row = lambda v: v.reshape(1, -1).astype(F32)
    pad_lanes = lambda v: jnp.pad(v.reshape(1, -1).astype(F32), ((0, 0), (0, LANES - v.shape[0])))
    col = lambda v: v.reshape(-1, 1).astype(F32)
    d_skip_x = jnp.repeat(d_skip.astype(F32), SSD_HEAD_DIM).reshape(1, w_ssd)

    const = lambda shape: pl.BlockSpec(shape, lambda b, l: (0,) * len(shape))
    dims = (tl, w_a, w_ssd, w_conf, n_heads)
    return pl.pallas_call(
        functools.partial(_mixer_kernel, dims),
        out_shape=jax.ShapeDtypeStruct((bsz, seq, d), F32),
        grid=(bsz, seq // tl),
        in_specs=[
            pl.BlockSpec((None, tl, d), lambda b, l: (b, l, 0)),
            const((1, d)), const((d, n_cols)), const((n_heads, d)),
            const((A_CONV, w_a)), const((SSD_CONV, xbc)), const((1, xbc)),
            const((1, LANES)), const((n_heads, 1)), const((1, LANES)), const((n_heads, 1)),
            const((1, w_ssd)), const((1, w_ssd)),
            const((CONF_KERNEL, w_conf)), const((1, w_conf)), const((1, w_conf)), const((1, w_conf)),
            const((d, d)),
        ],
        out_specs=pl.BlockSpec((None, tl, d), lambda b, l: (b, l, 0)),
        scratch_shapes=[
            pltpu.VMEM((tl, d), BF16),
            pltpu.VMEM((SMALL_HALO + tl, n_cols), F32),
            pltpu.VMEM((SMALL_HALO + tl, w_a), F32),
            pltpu.VMEM((CONF_HALO + tl, w_conf), F32),
            pltpu.VMEM((SUBLANES - 1, tl + CONF_HALO - SUBLANES, w_conf), F32),
            pltpu.VMEM((tl, xbc), F32),
            pltpu.VMEM((n_heads, tl), F32),
            pltpu.VMEM((tl, w_ssd), F32),
            pltpu.VMEM((tl, d), BF16),
            pltpu.VMEM((SSD_GROUPS, SSD_STATE, w_ssd // SSD_GROUPS), F32),
        ],
        compiler_params=pltpu.CompilerParams(
            dimension_semantics=("arbitrary", "arbitrary"), vmem_limit_bytes=VMEM_LIMIT_BYTES),
        name="mixer",
    )(x, row(norm_w), w_main, w_dt_t, conv_a_w.astype(F32), conv_ssd_w.astype(F32), row(conv_ssd_b),
      pad_lanes(dt_bias), col(dt_bias), pad_lanes(a_log), col(a_log), d_skip_x, row(ssd_norm_w),
      conv_conf_w.astype(F32), row(conv_conf_b), row(conf_ln_g), row(conf_ln_b), w_out.astype(BF16))


def _dense_ffn_kernel(final_norm, x_ref, nw_ref, wg_ref, wu_ref, wd_ref, fw_ref, out_ref, hn, acc):
    j = pl.program_id(1)

    @pl.when(j == 0)
    def _():
        hn[...] = _rmsnorm(x_ref[...], nw_ref[...]).astype(BF16)
        acc[...] = jnp.zeros(acc.shape, F32)

    h = hn[...]
    a = (_silu(_dot(h, wg_ref[...])) * _dot(h, wu_ref[...])).astype(BF16)
    acc[...] += _dot(a, wd_ref[...])

    @pl.when(j == pl.num_programs(1) - 1)
    def _():
        y = x_ref[...] + acc[...]
        out_ref[...] = _rmsnorm(y, fw_ref[...]) if final_norm else y


def _dense_ffn(x2, norm_w, w_gate, w_up, w_down, final_w, tm, tf):
    t, d = x2.shape
    ff = w_gate.shape[1]
    assert t % tm == 0 and ff % tf == 0
    final_norm = final_w is not None
    fw = (final_w if final_norm else jnp.ones((d,), F32)).reshape(1, d).astype(F32)
    return pl.pallas_call(
        functools.partial(_dense_ffn_kernel, final_norm),
        out_shape=jax.ShapeDtypeStruct((t, d), F32),
        grid=(t // tm, ff // tf),
        in_specs=[
            pl.BlockSpec((tm, d), lambda i, j: (i, 0)),
            pl.BlockSpec((1, d), lambda i, j: (0, 0)),
            pl.BlockSpec((d, tf), lambda i, j: (0, j)),
            pl.BlockSpec((d, tf), lambda i, j: (0, j)),
            pl.BlockSpec((tf, d), lambda i, j: (j, 0)),
            pl.BlockSpec((1, d), lambda i, j: (0, 0)),
        ],
        out_specs=pl.BlockSpec((tm, d), lambda i, j: (i, 0)),
        scratch_shapes=[pltpu.VMEM((tm, d), BF16), pltpu.VMEM((tm, d), F32)],
        compiler_params=pltpu.CompilerParams(
            dimension_semantics=("arbitrary", "arbitrary"), vmem_limit_bytes=VMEM_LIMIT_BYTES),
        name="dense_ffn",
    )(x2, norm_w.reshape(1, d).astype(F32), w_gate.astype(BF16), w_up.astype(BF16), w_down.astype(BF16), fw)


def _expert_ffn_kernel(blk_e_ref, nact_ref, xb_ref, wg_ref, wu_ref, wd_ref, out_ref, hb, acc):
    i = pl.program_id(0)
    j = pl.program_id(1)

    @pl.when(i < nact_ref[0])
    def _():
        @pl.when(j == 0)
        def _():
            hb[...] = xb_ref[...].astype(BF16)
            acc[...] = jnp.zeros(acc.shape, F32)

        h = hb[...]
        a = (_silu(_dot(h, wg_ref[...])) * _dot(h, wu_ref[...])).astype(BF16)
        acc[...] += _dot(a, wd_ref[...])

        @pl.when(j == pl.num_programs(1) - 1)
        def _():
            out_ref[...] = acc[...]

    @pl.when((i >= nact_ref[0]) & (j == 0))
    def _():
        out_ref[...] = jnp.zeros(out_ref.shape, F32)


def _expert_ffn(buf, blk_e, nact, w_gate, w_up, w_down, bm, tf):
    cap, d = buf.shape
    ff = w_gate.shape[2]
    n_blocks = cap // bm
    n_j = ff // tf
    assert cap % bm == 0 and ff % tf == 0
    row_blk = lambda i, j, be, na: (jnp.minimum(i, na[0] - 1), 0)
    col_j = lambda i, j, na: jnp.where(i < na[0], j, n_j - 1)
    return pl.pallas_call(
        _expert_ffn_kernel,
        out_shape=jax.ShapeDtypeStruct((cap, d), F32),
        grid_spec=pltpu.PrefetchScalarGridSpec(
            num_scalar_prefetch=2,
            grid=(n_blocks, n_j),
            in_specs=[
                pl.BlockSpec((bm, d), row_blk),
                pl.BlockSpec((None, d, tf), lambda i, j, be, na: (be[i], 0, col_j(i, j, na))),
                pl.BlockSpec((None, d, tf), lambda i, j, be, na: (be[i], 0, col_j(i, j, na))),
                pl.BlockSpec((None, tf, d), lambda i, j, be, na: (be[i], col_j(i, j, na), 0)),
            ],
            out_specs=pl.BlockSpec((bm, d), lambda i, j, be, na: (i, 0)),
            scratch_shapes=[pltpu.VMEM((bm, d), BF16), pltpu.VMEM((bm, d), F32)],
        ),
        compiler_params=pltpu.CompilerParams(
            dimension_semantics=("arbitrary", "arbitrary"), vmem_limit_bytes=VMEM_LIMIT_BYTES),
        name="expert_ffn",
    )(blk_e, nact, buf, w_gate.astype(BF16), w_up.astype(BF16), w_down.astype(BF16))


def _router_kernel(x_ref, nw_ref, wr_ref, hn_ref, route_ref):
    h = _rmsnorm(x_ref[...], nw_ref[...])
    hn_ref[...] = h
    h3 = _split3(h)
    w3 = _split3(wr_ref[...])
    logits = jnp.zeros((h.shape[0], LANES), F32)
    for a in range(3):
        for b in range(3 - a):
            logits = logits + _dot(h3[a], w3[b])
    lane = lax.broadcasted_iota(jnp.int32, logits.shape, 1)
    neg = jnp.float32(-jnp.inf)
    logits = jnp.where(lane < N_EXPERTS, logits, neg)
    m1 = jnp.max(logits, axis=-1, keepdims=True)
    i1 = jnp.min(jnp.where(logits == m1, lane, LANES), axis=-1, keepdims=True)
    rest = jnp.where(lane == i1, neg, logits)
    m2 = jnp.max(rest, axis=-1, keepdims=True)
    i2 = jnp.min(jnp.where(rest == m2, lane, LANES), axis=-1, keepdims=True)
    e2 = jnp.exp(m2 - m1)
    g1 = 1.0 / (1.0 + e2)
    g2 = e2 * g1
    route = jnp.where(lane == 0, i1.astype(F32), jnp.where(lane == 1, i2.astype(F32),
                      jnp.where(lane == 2, g1, jnp.where(lane == 3, g2, 0.0))))
    route_ref[...] = route


def _router(x2, norm_w, w_router, tm):
    t, d = x2.shape
    n_e = w_router.shape[1]
    assert n_e == N_EXPERTS and t % tm == 0
    wr = jnp.pad(w_router.astype(F32), ((0, 0), (0, LANES - n_e)))
    return pl.pallas_call(
        _router_kernel,
        out_shape=(jax.ShapeDtypeStruct((t, d), F32), jax.ShapeDtypeStruct((t, LANES), F32)),
        grid=(t // tm,),
        in_specs=[
            pl.BlockSpec((tm, d), lambda i: (i, 0)),
            pl.BlockSpec((1, d), lambda i: (0, 0)),
            pl.BlockSpec((d, LANES), lambda i: (0, 0)),
        ],
        out_specs=(pl.BlockSpec((tm, d), lambda i: (i, 0)), pl.BlockSpec((tm, LANES), lambda i: (i, 0))),
        compiler_params=pltpu.CompilerParams(
            dimension_semantics=("arbitrary",), vmem_limit_bytes=VMEM_LIMIT_BYTES),
        name="router",
    )(x2, norm_w.reshape(1, d).astype(F32), wr)


def _row_copy(src_ref, src_row, dst_ref, dst_row, sem):
    return pltpu.make_async_copy(src_ref.at[pl.ds(src_row, 1)], dst_ref.at[pl.ds(dst_row, 1)], sem)


def _tile_indices(dest, tile):
    k, t = dest.shape
    return dest.reshape(k, t // tile, tile).transpose(1, 0, 2).reshape(t // tile, 1, k * tile)


def _scatter_kernel(ts, dest_ref, hn_ref, buf_in_ref, buf_ref, sem):
    del buf_in_ref

    def issue(r, carry):
        for k in range(TOP_K):
            _row_copy(hn_ref, r, buf_ref, dest_ref[0, 0, k * ts + r], sem).start()
        return carry

    lax.fori_loop(0, ts, issue, 0)

    def drain(r, carry):
        for k in range(TOP_K):
            _row_copy(hn_ref, 0, buf_ref, 0, sem).wait()
        return carry

    lax.fori_loop(0, ts, drain, 0)


def _scatter_rows(hn, dest, cap, ts):
    t, d = hn.shape
    assert t % ts == 0
    return pl.pallas_call(
        functools.partial(_scatter_kernel, ts),
        out_shape=jax.ShapeDtypeStruct((cap, d), F32),
        grid=(t // ts,),
        in_specs=[
            pl.BlockSpec((1, 1, TOP_K * ts), lambda i: (i, 0, 0), memory_space=pltpu.SMEM),
            pl.BlockSpec((ts, d), lambda i: (i, 0)),
            pl.BlockSpec(memory_space=pl.ANY),
        ],
        out_specs=pl.BlockSpec(memory_space=pl.ANY),
        scratch_shapes=[pltpu.SemaphoreType.DMA(())],
        input_output_aliases={2: 0},
        compiler_params=pltpu.CompilerParams(
            dimension_semantics=("arbitrary",), vmem_limit_bytes=VMEM_LIMIT_BYTES),
        name="scatter_rows",
    )(_tile_indices(dest, ts), hn, jnp.zeros((cap, d), F32))


def _combine_kernel(tc, final_norm, dest_ref, x_ref, route_ref, fw_ref, ybuf_ref, out_ref, rows, sem):
    def issue(r, carry):
        for k in range(TOP_K):
            _row_copy(ybuf_ref, dest_ref[0, 0, k * tc + r], rows.at[k], r, sem).start()
        return carry

    lax.fori_loop(0, tc, issue, 0)

    def drain(r, carry):
        for k in range(TOP_K):
            _row_copy(ybuf_ref, 0, rows.at[k], 0, sem).wait()
        return carry

    lax.fori_loop(0, tc, drain, 0)

    route = route_ref[...]
    y = x_ref[...]
    for k in range(TOP_K):
        y = y + route[:, TOP_K + k:TOP_K + k + 1] * rows[k]
    out_ref[...] = _rmsnorm(y, fw_ref[...]) if final_norm else y


def _combine(x2, route, dest, ybuf, final_w, tc):
    t, d = x2.shape
    assert t % tc == 0
    final_norm = final_w is not None
    fw = (final_w if final_norm else jnp.ones((d,), F32)).reshape(1, d).astype(F32)
    return pl.pallas_call(
        functools.partial(_combine_kernel, tc, final_norm),
        out_shape=jax.ShapeDtypeStruct((t, d), F32),
        grid=(t // tc,),
        in_specs=[
            pl.BlockSpec((1, 1, TOP_K * tc), lambda i: (i, 0, 0), memory_space=pltpu.SMEM),
            pl.BlockSpec((tc, d), lambda i: (i, 0)),
            pl.BlockSpec((tc, LANES), lambda i: (i, 0)),
            pl.BlockSpec((1, d), lambda i: (0, 0)),
            pl.BlockSpec(memory_space=pl.ANY),
        ],
        out_specs=pl.BlockSpec((tc, d), lambda i: (i, 0)),
        scratch_shapes=[pltpu.VMEM((TOP_K, tc, d), F32), pltpu.SemaphoreType.DMA(())],
        compiler_params=pltpu.CompilerParams(
            dimension_semantics=("arbitrary",), vmem_limit_bytes=VMEM_LIMIT_BYTES),
        name="combine",
    )(_tile_indices(dest, tc), x2, route, fw, ybuf)


def _moe_layer(x2, norm_w, w_router, w_gate, w_up, w_down, final_w, tiles):
    t, d = x2.shape
    bm = tiles["moe_bm"]
    hn, route = _router(x2, norm_w, w_router, tiles["router_tm"])
    e_idx = route[:, :TOP_K].astype(jnp.int32).T
    onehot = (e_idx.reshape(-1)[None, :] == jnp.arange(N_EXPERTS, dtype=jnp.int32)[:, None]).astype(jnp.int32)
    csum = jnp.cumsum(onehot, axis=1)
    rank = jnp.sum(csum * onehot, axis=0) - 1
    sizes = csum[:, -1]
    padded = ((sizes + bm - 1) // bm) * bm
    ends = jnp.cumsum(padded)
    starts = ends - padded
    dest = (jnp.sum(starts[:, None] * onehot, axis=0) + rank).astype(jnp.int32).reshape(TOP_K, t)
    cap = TOP_K * t + N_EXPERTS * bm
    n_blocks = cap // bm
    blk_e = jnp.minimum(
        jnp.searchsorted(ends, jnp.arange(n_blocks, dtype=jnp.int32) * bm, side="right"), N_EXPERTS - 1
    ).astype(jnp.int32)
    nact = (ends[-1:] // bm).astype(jnp.int32)

    buf = _scatter_rows(hn, dest, cap, tiles["scatter_ts"])
    ybuf = _expert_ffn(buf, blk_e, nact, w_gate, w_up, w_down, bm, tiles["moe_tf"])
    return _combine(x2, route, dest, ybuf, final_w, tiles["combine_tc"])


TILES = {
    "mixer_tl": 256,
    "ffn_tm": 512, "ffn_tf": 256,
    "router_tm": 512,
    "moe_bm": 512, "moe_tf": 512,
    "scatter_ts": 512,
    "combine_tc": 256,
}


def kernel(x, norm_mix, w_in, conv_a_w, conv_ssd_w, conv_ssd_b, dt_bias, a_log, d_skip, ssd_norm_w, conv_conf_w, conv_conf_b, conf_ln_g, conf_ln_b, w_out, norm_ffn, ffn_w_gate, ffn_w_up, ffn_w_down, moe_router, moe_w_gate, moe_w_up, moe_w_down, norm_final):
    bsz, seq, d = x.shape
    depth = norm_mix.shape[0]
    tiles = dict(TILES)
    tiles["mixer_tl"] = min(tiles["mixer_tl"], seq)
    for i in range(depth):
        x = _mixer_layer(x, norm_mix[i], w_in[i], conv_a_w[i], conv_ssd_w[i], conv_ssd_b[i], dt_bias[i], a_log[i],
                         d_skip[i], ssd_norm_w[i], conv_conf_w[i], conv_conf_b[i], conf_ln_g[i], conf_ln_b[i],
                         w_out[i], tiles["mixer_tl"])
        x2 = x.reshape(bsz * seq, d)
        final_w = norm_final if i == depth - 1 else None
        j = i // 2
        if i % 2 == 0:
            x2 = _dense_ffn(x2, norm_ffn[i], ffn_w_gate[j], ffn_w_up[j], ffn_w_down[j], final_w,
                            tiles["ffn_tm"], tiles["ffn_tf"])
        else:
            x2 = _moe_layer(x2, norm_ffn[i], moe_router[j], moe_w_gate[j], moe_w_up[j], moe_w_down[j], final_w, tiles)
        x = x2.reshape(bsz, seq, d)
    return x
```

```python
import functools

import jax
import jax.numpy as jnp
from jax import lax
from jax.experimental import pallas as pl
from jax.experimental.pallas import tpu as pltpu

F32 = jnp.float32
BF16 = jnp.bfloat16
EPS = 1e-5

LANES = 128
SUBLANES = 8
VMEM_LIMIT_BYTES = 56 * 1024 * 1024

GROUP_DIM = 64
A_CONV = 3
SSD_HEAD_DIM = 64
SSD_GROUPS = 2
SSD_STATE = 128
SSD_CONV = 4
CONF_KERNEL = 31
N_EXPERTS = 8
TOP_K = 2

SSD_CHUNK = 128
CONF_HALO = 32
SMALL_HALO = SUBLANES
CONV_ROWS = 32
ROW_BLOCK = 64
LANE_BLOCK = 256


def _split3(v):
    hi = v.astype(BF16)
    r1 = v - hi.astype(F32)
    mid = r1.astype(BF16)
    lo = (r1 - mid.astype(F32)).astype(BF16)
    return hi, mid, lo


def _dot(a, b):
    return jnp.dot(a, b, preferred_element_type=F32)


def _dot_nt(a, b):
    return lax.dot_general(a, b, (((1,), (1,)), ((), ())), preferred_element_type=F32)


def _dot_sel_right(v, sel):
    hi, mid, lo = _split3(v)
    return _dot(hi, sel) + _dot(mid, sel) + _dot(lo, sel)


def _dot_sel_left(sel, v):
    hi, mid, lo = _split3(v)
    return _dot(sel, hi) + _dot(sel, mid) + _dot(sel, lo)


def _softplus(v):
    return jnp.maximum(v, 0.0) + jnp.log1p(jnp.exp(-jnp.abs(v)))


def _silu(v):
    return v * jax.nn.sigmoid(v)


def _rmsnorm(x, w):
    return x * lax.rsqrt(jnp.mean(x * x, axis=-1, keepdims=True) + EPS) * w


def _mixer_kernel(dims, x_ref, nw_ref, win_ref, wdtT_ref, caw_ref, csw_ref, csb_ref, dtbc_ref, dtbr_ref,
                  alc_ref, alr_ref, dsk_ref, snw_ref, ccw_ref, ccb_ref, lng_ref, lnb_ref, wout_ref,
                  out_ref,
                  hn_s, proj, buf_a, buf_c, shc, xact, dtr, ysd, ycat, hstate):
    tl, w_a, w_ssd, w_conf, n_heads = dims
    q = SSD_CHUNK
    n_bc = SSD_GROUPS * SSD_STATE
    xbc = w_ssd + 2 * n_bc
    hpg = n_heads // SSD_GROUPS
    gw = hpg * SSD_HEAD_DIM
    hb = SMALL_HALO
    o_z = 3 * w_a
    o_xbc = o_z + w_ssd
    o_c = o_xbc + xbc
    o_dt = o_c + 2 * w_conf
    lt = pl.program_id(1)

    @pl.when(lt == 0)
    def _():
        buf_a[0:hb, :] = jnp.zeros((hb, w_a), F32)
        proj[0:hb, o_xbc:o_xbc + xbc] = jnp.zeros((hb, xbc), F32)
        buf_c[0:CONF_HALO, :] = jnp.zeros((CONF_HALO, w_conf), F32)
        hstate[...] = jnp.zeros(hstate.shape, F32)

    @pl.when(lt > 0)
    def _():
        buf_a[0:hb, :] = buf_a[tl:tl + hb, :]
        proj[0:hb, o_xbc:o_xbc + xbc] = proj[tl:tl + hb, o_xbc:o_xbc + xbc]
        buf_c[0:CONF_HALO, :] = buf_c[tl:tl + CONF_HALO, :]

    hn_s[...] = _rmsnorm(x_ref[...], nw_ref[...]).astype(BF16)
    proj[hb:hb + tl, :] = _dot(hn_s[...], win_ref[...])
    dtr[...] = _dot_nt(wdtT_ref[...], hn_s[...])

    for rb in range(tl // ROW_BLOCK):
        r0 = hb + rb * ROW_BLOCK
        buf_a[r0:r0 + ROW_BLOCK, :] = proj[r0:r0 + ROW_BLOCK, w_a:2 * w_a] * proj[r0:r0 + ROW_BLOCK, 2 * w_a:3 * w_a]
        ya = jnp.zeros((ROW_BLOCK, w_a), F32)
        for j in range(A_CONV):
            s = r0 - (A_CONV - 1) + j
            ya = ya + caw_ref[j:j + 1, :] * buf_a[s:s + ROW_BLOCK, :]
        ycat[rb * ROW_BLOCK:(rb + 1) * ROW_BLOCK, 0:w_a] = (proj[r0:r0 + ROW_BLOCK, 0:w_a] * ya).astype(BF16)

    for rb in range(tl // ROW_BLOCK):
        r0 = hb + rb * ROW_BLOCK
        buf_c[CONF_HALO + rb * ROW_BLOCK:CONF_HALO + (rb + 1) * ROW_BLOCK, :] = (
            proj[r0:r0 + ROW_BLOCK, o_c:o_c + w_conf]
            * jax.nn.sigmoid(proj[r0:r0 + ROW_BLOCK, o_c + w_conf:o_c + 2 * w_conf]))
    n_sh = shc.shape[1]
    for r in range(1, SUBLANES):
        shc[r - 1] = buf_c[r:r + n_sh, :]
    for rb in range(tl // CONV_ROWS):
        acc = jnp.zeros((CONV_ROWS, w_conf), F32) + ccb_ref[...]
        for j in range(CONF_KERNEL):
            a, r = divmod(CONF_HALO - (CONF_KERNEL - 1) + j, SUBLANES)
            base = a * SUBLANES + rb * CONV_ROWS
            src = buf_c[base:base + CONV_ROWS, :] if r == 0 else shc[r - 1, base:base + CONV_ROWS, :]
            acc = acc + ccw_ref[j:j + 1, :] * src
        mu = jnp.mean(acc, axis=-1, keepdims=True)
        cen = acc - mu
        var = jnp.mean(cen * cen, axis=-1, keepdims=True)
        yc = cen * lax.rsqrt(var + EPS) * lng_ref[...] + lnb_ref[...]
        ycat[rb * CONV_ROWS:(rb + 1) * CONV_ROWS, w_a + w_ssd:w_a + w_ssd + w_conf] = _silu(yc).astype(BF16)

    for rb in range(tl // ROW_BLOCK):
        for lb in range(xbc // LANE_BLOCK):
            cols = slice(lb * LANE_BLOCK, (lb + 1) * LANE_BLOCK)
            pcols = slice(o_xbc + lb * LANE_BLOCK, o_xbc + (lb + 1) * LANE_BLOCK)
            xc = jnp.zeros((ROW_BLOCK, LANE_BLOCK), F32) + csb_ref[:, cols]
            for j in range(SSD_CONV):
                s = hb + rb * ROW_BLOCK - (SSD_CONV - 1) + j
                xc = xc + csw_ref[j:j + 1, cols] * proj[s:s + ROW_BLOCK, pcols]
            xact[rb * ROW_BLOCK:(rb + 1) * ROW_BLOCK, cols] = _silu(xc)

    a_col = -jnp.exp(alc_ref[...])
    a_row = -jnp.exp(alr_ref[...])
    ri = lax.broadcasted_iota(jnp.int32, (q, q), 0)
    ci = lax.broadcasted_iota(jnp.int32, (q, q), 1)
    causal = ri >= ci
    tril = jnp.where(causal, 1.0, 0.0).astype(BF16)
    triu = jnp.where(ri <= ci, 1.0, 0.0).astype(BF16)
    hd = SSD_HEAD_DIM

    for c in range(tl // q):
        rows = slice(c * q, (c + 1) * q)
        prow = slice(hb + c * q, hb + (c + 1) * q)
        dt_c = _softplus(proj[prow, o_dt:o_dt + LANES] + dtbc_ref[...])
        dt_r = _softplus(dtr[:, rows] + dtbr_ref[...])
        acum_c = _dot_sel_left(tril, dt_c * a_col)
        acum_r = _dot_sel_right(dt_r * a_row, triu)
        to_end_r = dt_r * jnp.exp(acum_r[:, q - 1:q] - acum_r)
        for g in range(SSD_GROUPS):
            bg = xact[rows, w_ssd + g * SSD_STATE:w_ssd + (g + 1) * SSD_STATE]
            cg = xact[rows, w_ssd + n_bc + g * SSD_STATE:w_ssd + n_bc + (g + 1) * SSD_STATE].astype(BF16)
            cb = _dot_nt(cg, bg.astype(BF16))
            bgt = bg.T
            hg = hstate[g]
            yoff = _dot(cg, hg.astype(BF16))
            for r in range(hpg):
                h = g * hpg + r
                xh = xact[rows, h * hd:(h + 1) * hd].astype(BF16)
                acol = jnp.broadcast_to(acum_c[:, h:h + 1], (q, q))
                dec = jnp.where(causal, jnp.exp(acol - acum_r[h:h + 1, :]), 0.0)
                yd = _dot((cb * dec * dt_r[h:h + 1, :]).astype(BF16), xh)
                ysd[rows, h * hd:(h + 1) * hd] = yd + yoff[:, r * hd:(r + 1) * hd] * jnp.exp(acol[:, 0:hd])
                st = _dot((bgt * to_end_r[h:h + 1, :]).astype(BF16), xh)
                hstate[g, :, r * hd:(r + 1) * hd] = jnp.exp(acol[q - 1:q, 0:hd]) * hg[:, r * hd:(r + 1) * hd] + st
        xs = xact[rows, 0:w_ssd]
        yz = (ysd[rows, :] + dsk_ref[...] * xs) * _silu(proj[prow, o_z:o_z + w_ssd])
        for g in range(SSD_GROUPS):
            yg = yz[:, g * gw:(g + 1) * gw]
            yn = yg * lax.rsqrt(jnp.mean(yg * yg, axis=-1, keepdims=True) + EPS) * snw_ref[:, g * gw:(g + 1) * gw]
            ycat[rows, w_a + g * gw:w_a + (g + 1) * gw] = yn.astype(BF16)

    out_ref[...] = x_ref[...] + _dot(ycat[...], wout_ref[...])


def _mixer_layer(x, norm_w, w_in, conv_a_w, conv_ssd_w, conv_ssd_b, dt_bias, a_log, d_skip, ssd_norm_w,
                 conv_conf_w, conv_conf_b, conf_ln_g, conf_ln_b, w_out, tl):
    bsz, seq, d = x.shape
    w_a = conv_a_w.shape[1]
    w_ssd = ssd_norm_w.shape[0]
    w_conf = conv_conf_w.shape[1]
    n_heads = dt_bias.shape[0]
    n_bc = SSD_GROUPS * SSD_STATE
    xbc = w_ssd + 2 * n_bc
    assert n_heads == SUBLANES and n_heads * SSD_HEAD_DIM == w_ssd and seq % tl == 0 and tl % SSD_CHUNK == 0
    assert tl % ROW_BLOCK == 0 and xbc % LANE_BLOCK == 0 and conv_a_w.shape[0] == A_CONV and conv_ssd_w.shape[0] == SSD_CONV
    assert conv_conf_w.shape[0] == CONF_KERNEL
    o_dt = 3 * w_a + w_ssd + xbc
    w_dt = w_in[:, o_dt:o_dt + n_heads]
    w_main = jnp.concatenate(
        [w_in[:, :o_dt], w_in[:, o_dt + n_heads:], jnp.pad(w_dt, ((0, 0), (0, LANES - n_heads)))], axis=1).astype(BF16)
    n_cols = w_main.shape[1]
    w_dt_t = w_dt.T.astype(BF16)
    row = lambda v: v.reshape(1, -1).astype(F32)
    pad_lanes = lambda v: jnp.pad(v.reshape(1, -1).astype(F32), ((0, 0), (0, LANES - v.shape[0])))
    col = lambda v: v.reshape(-1, 1).astype(F32)
    d_skip_x = jnp.repeat(d_skip.astype(F32), SSD_HEAD_DIM).reshape(1, w_ssd)

    const = lambda shape: pl.BlockSpec(shape, lambda b, l: (0,) * len(shape))
    dims = (tl, w_a, w_ssd, w_conf, n_heads)
    return pl.pallas_call(
        functools.partial(_mixer_kernel, dims),
        out_shape=jax.ShapeDtypeStruct((bsz, seq, d), F32),
        grid=(bsz, seq // tl),
        in_specs=[
            pl.BlockSpec((None, tl, d), lambda b, l: (b, l, 0)),
            const((1, d)), const((d, n_cols)), const((n_heads, d)),
            const((A_CONV, w_a)), const((SSD_CONV, xbc)), const((1, xbc)),
            const((1, LANES)), const((n_heads, 1)), const((1, LANES)), const((n_heads, 1)),
            const((1, w_ssd)), const((1, w_ssd)),
            const((CONF_KERNEL, w_conf)), const((1, w_conf)), const((1, w_conf)), const((1, w_conf)),
            const((d, d)),
        ],
        out_specs=pl.BlockSpec((None, tl, d), lambda b, l: (b, l, 0)),
        scratch_shapes=[
            pltpu.VMEM((tl, d), BF16),
            pltpu.VMEM((SMALL_HALO + tl, n_cols), F32),
            pltpu.VMEM((SMALL_HALO + tl, w_a), F32),
            pltpu.VMEM((CONF_HALO + tl, w_conf), F32),
            pltpu.VMEM((SUBLANES - 1, tl + CONF_HALO - SUBLANES, w_conf), F32),
            pltpu.VMEM((tl, xbc), F32),
            pltpu.VMEM((n_heads, tl), F32),
            pltpu.VMEM((tl, w_ssd), F32),
            pltpu.VMEM((tl, d), BF16),
            pltpu.VMEM((SSD_GROUPS, SSD_STATE, w_ssd // SSD_GROUPS), F32),
        ],
        compiler_params=pltpu.CompilerParams(
            dimension_semantics=("arbitrary", "arbitrary"), vmem_limit_bytes=VMEM_LIMIT_BYTES),
        name="mixer",
    )(x, row(norm_w), w_main, w_dt_t, conv_a_w.astype(F32), conv_ssd_w.astype(F32), row(conv_ssd_b),
      pad_lanes(dt_bias), col(dt_bias), pad_lanes(a_log), col(a_log), d_skip_x, row(ssd_norm_w),
      conv_conf_w.astype(F32), row(conv_conf_b), row(conf_ln_g), row(conf_ln_b), w_out.astype(BF16))


def _dense_ffn_kernel(final_norm, x_ref, nw_ref, wg_ref, wu_ref, wd_ref, fw_ref, out_ref, hn):
    hn[...] = _rmsnorm(x_ref[...], nw_ref[...]).astype(BF16)
    a = (_silu(_dot(hn[...], wg_ref[...])) * _dot(hn[...], wu_ref[...])).astype(BF16)
    y = x_ref[...] + _dot(a, wd_ref[...])
    out_ref[...] = _rmsnorm(y, fw_ref[...]) if final_norm else y


def _dense_ffn(x2, norm_w, w_gate, w_up, w_down, final_w, tm):
    t, d = x2.shape
    ff = w_gate.shape[1]
    assert t % tm == 0
    final_norm = final_w is not None
    fw = (final_w if final_norm else jnp.ones((d,), F32)).reshape(1, d).astype(F32)
    resident = lambda shape: pl.BlockSpec(shape, lambda i: (0, 0), pipeline_mode=pl.Buffered(1))
    return pl.pallas_call(
        functools.partial(_dense_ffn_kernel, final_norm),
        out_shape=jax.ShapeDtypeStruct((t, d), F32),
        grid=(t // tm,),
        in_specs=[
            pl.BlockSpec((tm, d), lambda i: (i, 0)),
            resident((1, d)), resident((d, ff)), resident((d, ff)), resident((ff, d)), resident((1, d)),
        ],
        out_specs=pl.BlockSpec((tm, d), lambda i: (i, 0)),
        scratch_shapes=[pltpu.VMEM((tm, d), BF16)],
        compiler_params=pltpu.CompilerParams(
            dimension_semantics=("arbitrary",), vmem_limit_bytes=VMEM_LIMIT_BYTES),
        name="dense_ffn",
    )(x2, norm_w.reshape(1, d).astype(F32), w_gate.astype(BF16), w_up.astype(BF16), w_down.astype(BF16), fw)


def _rows_to_tiles(tile_ref, val):
    n, d = val.shape
    n_sub = d // LANES
    for c in range(n_sub):
        tile_ref[pl.ds(c, n, stride=n_sub), :] = val[:, c * LANES:(c + 1) * LANES]


def _tiles_to_rows(tile_ref, n_sub):
    n = tile_ref.shape[0] // n_sub
    return jnp.concatenate([tile_ref[pl.ds(c, n, stride=n_sub), :] for c in range(n_sub)], axis=1)


def _expert_ffn_kernel(blk_e_ref, nact_ref, xb_ref, wg_ref, wu_ref, wd_ref, out_ref, hb, acc):
    i = pl.program_id(0)
    j = pl.program_id(1)

    @pl.when(i < nact_ref[0])
    def _():
        @pl.when(j == 0)
        def _():
            hb[...] = _tiles_to_rows(xb_ref, hb.shape[1] // LANES).astype(BF16)
            acc[...] = jnp.zeros(acc.shape, F32)

        h = hb[...]
        a = (_silu(_dot(h, wg_ref[...])) * _dot(h, wu_ref[...])).astype(BF16)
        acc[...] += _dot(a, wd_ref[...])

        @pl.when(j == pl.num_programs(1) - 1)
        def _():
            _rows_to_tiles(out_ref, acc[...])

    @pl.when((i >= nact_ref[0]) & (j == 0))
    def _():
        out_ref[...] = jnp.zeros(out_ref.shape, F32)


def _expert_ffn(buf, blk_e, nact, w_gate, w_up, w_down, bm, tf):
    d, ff = w_gate.shape[1], w_gate.shape[2]
    n_sub = d // LANES
    cap = buf.shape[0] // n_sub
    n_blocks = cap // bm
    n_j = ff // tf
    assert cap % bm == 0 and ff % tf == 0
    row_blk = lambda i, j, be, na: (jnp.minimum(i, na[0] - 1), 0)
    snake = lambda i, j: jnp.where(i % 2 == 0, j, n_j - 1 - j)
    col_j = lambda i, j, na: jnp.where(i < na[0], snake(i, j), snake(na[0] - 1, n_j - 1))
    return pl.pallas_call(
        _expert_ffn_kernel,
        out_shape=jax.ShapeDtypeStruct((cap * n_sub, LANES), F32),
        grid_spec=pltpu.PrefetchScalarGridSpec(
            num_scalar_prefetch=2,
            grid=(n_blocks, n_j),
            in_specs=[
                pl.BlockSpec((bm * n_sub, LANES), row_blk),
                pl.BlockSpec((None, d, tf), lambda i, j, be, na: (be[i], 0, col_j(i, j, na))),
                pl.BlockSpec((None, d, tf), lambda i, j, be, na: (be[i], 0, col_j(i, j, na))),
                pl.BlockSpec((None, tf, d), lambda i, j, be, na: (be[i], col_j(i, j, na), 0)),
            ],
            out_specs=pl.BlockSpec((bm * n_sub, LANES), lambda i, j, be, na: (i, 0)),
            scratch_shapes=[pltpu.VMEM((bm, d), BF16), pltpu.VMEM((bm, d), F32)],
        ),
        compiler_params=pltpu.CompilerParams(
            dimension_semantics=("arbitrary", "arbitrary"), vmem_limit_bytes=VMEM_LIMIT_BYTES),
        name="expert_ffn",
    )(blk_e, nact, buf, w_gate.astype(BF16), w_up.astype(BF16), w_down.astype(BF16))


def _router_kernel(x_ref, nw_ref, wr_ref, hn_ref, route_ref):
    h = _rmsnorm(x_ref[...], nw_ref[...])
    _rows_to_tiles(hn_ref, h)
    h3 = _split3(h)
    w3 = _split3(wr_ref[...])
    logits = jnp.zeros((h.shape[0], LANES), F32)
    for a in range(3):
        for b in range(3 - a):
            logits = logits + _dot(h3[a], w3[b])
    lane = lax.broadcasted_iota(jnp.int32, logits.shape, 1)
    neg = jnp.float32(-jnp.inf)
    logits = jnp.where(lane < N_EXPERTS, logits, neg)
    m1 = jnp.max(logits, axis=-1, keepdims=True)
    i1 = jnp.min(jnp.where(logits == m1, lane, LANES), axis=-1, keepdims=True)
    rest = jnp.where(lane == i1, neg, logits)
    m2 = jnp.max(rest, axis=-1, keepdims=True)
    i2 = jnp.min(jnp.where(rest == m2, lane, LANES), axis=-1, keepdims=True)
    e2 = jnp.exp(m2 - m1)
    g1 = 1.0 / (1.0 + e2)
    g2 = e2 * g1
    route = jnp.where(lane == 0, i1.astype(F32), jnp.where(lane == 1, i2.astype(F32),
                      jnp.where(lane == 2, g1, jnp.where(lane == 3, g2, 0.0))))
    route_ref[...] = route


def _router(x2, norm_w, w_router, tm):
    t, d = x2.shape
    n_e = w_router.shape[1]
    assert n_e == N_EXPERTS and t % tm == 0 and d % LANES == 0
    n_sub = d // LANES
    wr = jnp.pad(w_router.astype(F32), ((0, 0), (0, LANES - n_e)))
    return pl.pallas_call(
        _router_kernel,
        out_shape=(jax.ShapeDtypeStruct((t * n_sub, LANES), F32), jax.ShapeDtypeStruct((t, LANES), F32)),
        grid=(t // tm,),
        in_specs=[
            pl.BlockSpec((tm, d), lambda i: (i, 0)),
            pl.BlockSpec((1, d), lambda i: (0, 0)),
            pl.BlockSpec((d, LANES), lambda i: (0, 0)),
        ],
        out_specs=(pl.BlockSpec((tm * n_sub, LANES), lambda i: (i, 0)),
                   pl.BlockSpec((tm, LANES), lambda i: (i, 0))),
        compiler_params=pltpu.CompilerParams(
            dimension_semantics=("arbitrary",), vmem_limit_bytes=VMEM_LIMIT_BYTES),
        name="router",
    )(x2, norm_w.reshape(1, d).astype(F32), wr)


def _row_copy(n_sub, src_ref, src_row, dst_ref, dst_row, sem):
    return pltpu.make_async_copy(src_ref.at[pl.ds(pl.multiple_of(src_row, n_sub), n_sub)],
                                 dst_ref.at[pl.ds(pl.multiple_of(dst_row, n_sub), n_sub)], sem)


def _tile_indices(dest, tile):
    k, t = dest.shape
    return dest.reshape(k, t // tile, tile).transpose(1, 0, 2).reshape(t // tile, 1, k * tile)


def _scatter_kernel(ts, n_sub, dest_ref, hn_ref, buf_in_ref, buf_ref, sem):
    del buf_in_ref

    def issue(r, carry):
        for k in range(TOP_K):
            _row_copy(n_sub, hn_ref, r * n_sub, buf_ref, dest_ref[0, 0, k * ts + r], sem).start(priority=k % 2)
        return carry

    lax.fori_loop(0, ts, issue, 0)

    def drain(r, carry):
        for k in range(TOP_K):
            _row_copy(n_sub, hn_ref, 0, buf_ref, 0, sem).wait()
        return carry

    lax.fori_loop(0, ts, drain, 0)


def _scatter_rows(hn, dest, cap, ts, n_sub):
    t = hn.shape[0] // n_sub
    assert t % ts == 0
    return pl.pallas_call(
        functools.partial(_scatter_kernel, ts, n_sub),
        out_shape=jax.ShapeDtypeStruct((cap * n_sub, LANES), F32),
        grid=(t // ts,),
        in_specs=[
            pl.BlockSpec((1, 1, TOP_K * ts), lambda i: (i, 0, 0), memory_space=pltpu.SMEM),
            pl.BlockSpec((ts * n_sub, LANES), lambda i: (i, 0)),
            pl.BlockSpec(memory_space=pl.ANY),
        ],
        out_specs=pl.BlockSpec(memory_space=pl.ANY),
        scratch_shapes=[pltpu.SemaphoreType.DMA(())],
        input_output_aliases={2: 0},
        compiler_params=pltpu.CompilerParams(
            dimension_semantics=("arbitrary",), vmem_limit_bytes=VMEM_LIMIT_BYTES),
        name="scatter_rows",
    )(_tile_indices(dest, ts), hn, jnp.zeros((cap * n_sub, LANES), F32))


def _combine_kernel(tc, final_norm, dest_ref, x_ref, route_ref, fw_ref, ybuf_ref, out_ref, rows, sem):
    n_sub = x_ref.shape[1] // LANES

    def issue(r, carry):
        for k in range(TOP_K):
            _row_copy(n_sub, ybuf_ref, dest_ref[0, 0, k * tc + r], rows.at[k], r * n_sub, sem).start(priority=k % 2)
        return carry

    lax.fori_loop(0, tc, issue, 0)

    def drain(r, carry):
        for k in range(TOP_K):
            _row_copy(n_sub, ybuf_ref, 0, rows.at[k], 0, sem).wait()
        return carry

    lax.fori_loop(0, tc, drain, 0)

    route = route_ref[...]
    y = x_ref[...]
    for k in range(TOP_K):
        y = y + route[:, TOP_K + k:TOP_K + k + 1] * _tiles_to_rows(rows.at[k], n_sub)
    out_ref[...] = _rmsnorm(y, fw_ref[...]) if final_norm else y


def _combine(x2, route, dest, ybuf, final_w, tc):
    t, d = x2.shape
    assert t % tc == 0
    final_norm = final_w is not None
    fw = (final_w if final_norm else jnp.ones((d,), F32)).reshape(1, d).astype(F32)
    return pl.pallas_call(
        functools.partial(_combine_kernel, tc, final_norm),
        out_shape=jax.ShapeDtypeStruct((t, d), F32),
        grid=(t // tc,),
        in_specs=[
            pl.BlockSpec((1, 1, TOP_K * tc), lambda i: (i, 0, 0), memory_space=pltpu.SMEM),
            pl.BlockSpec((tc, d), lambda i: (i, 0)),
            pl.BlockSpec((tc, LANES), lambda i: (i, 0)),
            pl.BlockSpec((1, d), lambda i: (0, 0)),
            pl.BlockSpec(memory_space=pl.ANY),
        ],
        out_specs=pl.BlockSpec((tc, d), lambda i: (i, 0)),
        scratch_shapes=[pltpu.VMEM((TOP_K, tc * (d // LANES), LANES), F32), pltpu.SemaphoreType.DMA(())],
        compiler_params=pltpu.CompilerParams(
            dimension_semantics=("arbitrary",), vmem_limit_bytes=VMEM_LIMIT_BYTES),
        name="combine",
    )(_tile_indices(dest, tc), x2, route, fw, ybuf)


def _moe_layer(x2, norm_w, w_router, w_gate, w_up, w_down, final_w, tiles):
    t, d = x2.shape
    bm = tiles["moe_bm"]
    hn, route = _router(x2, norm_w, w_router, tiles["router_tm"])
    e_idx = route[:, :TOP_K].astype(jnp.int32).T
    onehot = (e_idx.reshape(-1)[None, :] == jnp.arange(N_EXPERTS, dtype=jnp.int32)[:, None]).astype(jnp.int32)
    csum = jnp.cumsum(onehot, axis=1)
    rank = jnp.sum(csum * onehot, axis=0) - 1
    sizes = csum[:, -1]
    padded = ((sizes + bm - 1) // bm) * bm
    ends = jnp.cumsum(padded)
    starts = ends - padded
    dest = (jnp.sum(starts[:, None] * onehot, axis=0) + rank).astype(jnp.int32).reshape(TOP_K, t)
    cap = TOP_K * t + N_EXPERTS * bm
    n_blocks = cap // bm
    blk_e = jnp.minimum(
        jnp.searchsorted(ends, jnp.arange(n_blocks, dtype=jnp.int32) * bm, side="right"), N_EXPERTS - 1
    ).astype(jnp.int32)
    nact = (ends[-1:] // bm).astype(jnp.int32)

    n_sub = d // LANES
    dest_rows = dest * n_sub
    buf = _scatter_rows(hn, dest_rows, cap, tiles["scatter_ts"], n_sub)
    ybuf = _expert_ffn(buf, blk_e, nact, w_gate, w_up, w_down, bm, tiles["moe_tf"])
    return _combine(x2, route, dest_rows, ybuf, final_w, tiles["combine_tc"])


TILES = {
    "mixer_tl": 256,
    "ffn_tm": 512,
    "router_tm": 512,
    "moe_bm": 512, "moe_tf": 1792,
    "scatter_ts": 512,
    "combine_tc": 256,
}


def kernel(x, norm_mix, w_in, conv_a_w, conv_ssd_w, conv_ssd_b, dt_bias, a_log, d_skip, ssd_norm_w, conv_conf_w, conv_conf_b, conf_ln_g, conf_ln_b, w_out, norm_ffn, ffn_w_gate, ffn_w_up, ffn_w_down, moe_router, moe_w_gate, moe_w_up, moe_w_down, norm_final):
    bsz, seq, d = x.shape
    depth = norm_mix.shape[0]
    tiles = dict(TILES)
    tiles["mixer_tl"] = min(tiles["mixer_tl"], seq)
    for i in range(depth):
        x = _mixer_layer(x, norm_mix[i], w_in[i], conv_a_w[i], conv_ssd_w[i], conv_ssd_b[i], dt_bias[i], a_log[i],
                         d_skip[i], ssd_norm_w[i], conv_conf_w[i], conv_conf_b[i], conf_ln_g[i], conf_ln_b[i],
                         w_out[i], tiles["mixer_tl"])
        x2 = x.reshape(bsz * seq, d)
        final_w = norm_final if i == depth - 1 else None
        j = i // 2
        if i % 2 == 0:
            x2 = _dense_ffn(x2, norm_ffn[i], ffn_w_gate[j], ffn_w_up[j], ffn_w_down[j], final_w,
                            tiles["ffn_tm"])
        else:
            x2 = _moe_layer(x2, norm_ffn[i], moe_router[j], moe_w_gate[j], moe_w_up[j], moe_w_down[j], final_w, tiles)
        x = x2.reshape(bsz, seq, d)
    return x
```

```python
import functools

import jax
import jax.numpy as jnp
from jax import lax
from jax.experimental import pallas as pl
from jax.experimental.pallas import tpu as pltpu

F32 = jnp.float32
BF16 = jnp.bfloat16
EPS = 1e-5

LANES = 128
SUBLANES = 8
VMEM_LIMIT_BYTES = 56 * 1024 * 1024

GROUP_DIM = 64
A_CONV = 3
SSD_HEAD_DIM = 64
SSD_GROUPS = 2
SSD_STATE = 128
SSD_CONV = 4
CONF_KERNEL = 31
N_EXPERTS = 8
TOP_K = 2

SSD_CHUNK = 128
CONF_HALO = 32
SMALL_HALO = SUBLANES
CONV_ROWS = 32
ROW_BLOCK = 64
LANE_BLOCK = 256
DMA_UNROLL = 8


def _split3(v):
    hi = v.astype(BF16)
    r1 = v - hi.astype(F32)
    mid = r1.astype(BF16)
    lo = (r1 - mid.astype(F32)).astype(BF16)
    return hi, mid, lo


def _dot(a, b):
    return jnp.dot(a, b, preferred_element_type=F32)


def _dot_nt(a, b):
    return lax.dot_general(a, b, (((1,), (1,)), ((), ())), preferred_element_type=F32)


def _dot_sel_right(v, sel):
    hi, mid, lo = _split3(v)
    return _dot(hi, sel) + _dot(mid, sel) + _dot(lo, sel)


def _dot_sel_left(sel, v):
    hi, mid, lo = _split3(v)
    return _dot(sel, hi) + _dot(sel, mid) + _dot(sel, lo)


def _softplus(v):
    return jnp.maximum(v, 0.0) + jnp.log1p(jnp.exp(-jnp.abs(v)))


def _silu(v):
    return v * jax.nn.sigmoid(v)


def _rmsnorm(x, w):
    return x * lax.rsqrt(jnp.mean(x * x, axis=-1, keepdims=True) + EPS) * w


def _mixer_kernel(dims, x_ref, nw_ref, win_ref, wdtT_ref, caw_ref, csw_ref, csb_ref, dtbc_ref, dtbr_ref,
                  alc_ref, alr_ref, dsk_ref, snw_ref, ccw_ref, ccb_ref, lng_ref, lnb_ref, wout_ref,
                  out_ref,
                  hn_s, proj, buf_a, buf_c, shc, xact, dtr, ysd, ycat, hstate):
    tl, w_a, w_ssd, w_conf, n_heads = dims
    q = SSD_CHUNK
    n_bc = SSD_GROUPS * SSD_STATE
    xbc = w_ssd + 2 * n_bc
    hpg = n_heads // SSD_GROUPS
    gw = hpg * SSD_HEAD_DIM
    hb = SMALL_HALO
    o_z = 3 * w_a
    o_xbc = o_z + w_ssd
    o_c = o_xbc + xbc
    o_dt = o_c + 2 * w_conf
    lt = pl.program_id(1)

    @pl.when(lt == 0)
    def _():
        buf_a[0:hb, :] = jnp.zeros((hb, w_a), F32)
        proj[0:hb, o_xbc:o_xbc + xbc] = jnp.zeros((hb, xbc), F32)
        buf_c[0:CONF_HALO, :] = jnp.zeros((CONF_HALO, w_conf), F32)
        hstate[...] = jnp.zeros(hstate.shape, F32)

    @pl.when(lt > 0)
    def _():
        buf_a[0:hb, :] = buf_a[tl:tl + hb, :]
        proj[0:hb, o_xbc:o_xbc + xbc] = proj[tl:tl + hb, o_xbc:o_xbc + xbc]
        buf_c[0:CONF_HALO, :] = buf_c[tl:tl + CONF_HALO, :]

    hn_s[...] = _rmsnorm(x_ref[...], nw_ref[...]).astype(BF16)
    proj[hb:hb + tl, :] = _dot(hn_s[...], win_ref[...])
    dtr[...] = _dot_nt(wdtT_ref[...], hn_s[...])

    for rb in range(tl // ROW_BLOCK):
        r0 = hb + rb * ROW_BLOCK
        buf_a[r0:r0 + ROW_BLOCK, :] = proj[r0:r0 + ROW_BLOCK, w_a:2 * w_a] * proj[r0:r0 + ROW_BLOCK, 2 * w_a:3 * w_a]
        ya = jnp.zeros((ROW_BLOCK, w_a), F32)
        for j in range(A_CONV):
            s = r0 - (A_CONV - 1) + j
            ya = ya + caw_ref[j:j + 1, :] * buf_a[s:s + ROW_BLOCK, :]
        ycat[rb * ROW_BLOCK:(rb + 1) * ROW_BLOCK, 0:w_a] = (proj[r0:r0 + ROW_BLOCK, 0:w_a] * ya).astype(BF16)

    for rb in range(tl // ROW_BLOCK):
        r0 = hb + rb * ROW_BLOCK
        buf_c[CONF_HALO + rb * ROW_BLOCK:CONF_HALO + (rb + 1) * ROW_BLOCK, :] = (
            proj[r0:r0 + ROW_BLOCK, o_c:o_c + w_conf]
            * jax.nn.sigmoid(proj[r0:r0 + ROW_BLOCK, o_c + w_conf:o_c + 2 * w_conf]))
    n_sh = shc.shape[1]
    for r in range(1, SUBLANES):
        shc[r - 1] = buf_c[r:r + n_sh, :]
    for rb in range(tl // CONV_ROWS):
        acc = jnp.zeros((CONV_ROWS, w_conf), F32) + ccb_ref[...]
        for j in range(CONF_KERNEL):
            a, r = divmod(CONF_HALO - (CONF_KERNEL - 1) + j, SUBLANES)
            base = a * SUBLANES + rb * CONV_ROWS
            src = buf_c[base:base + CONV_ROWS, :] if r == 0 else shc[r - 1, base:base + CONV_ROWS, :]
            acc = acc + ccw_ref[j:j + 1, :] * src
        mu = jnp.mean(acc, axis=-1, keepdims=True)
        cen = acc - mu
        var = jnp.mean(cen * cen, axis=-1, keepdims=True)
        yc = cen * lax.rsqrt(var + EPS) * lng_ref[...] + lnb_ref[...]
        ycat[rb * CONV_ROWS:(rb + 1) * CONV_ROWS, w_a + w_ssd:w_a + w_ssd + w_conf] = _silu(yc).astype(BF16)

    for rb in range(tl // ROW_BLOCK):
        for lb in range(xbc // LANE_BLOCK):
            cols = slice(lb * LANE_BLOCK, (lb + 1) * LANE_BLOCK)
            pcols = slice(o_xbc + lb * LANE_BLOCK, o_xbc + (lb + 1) * LANE_BLOCK)
            xc = jnp.zeros((ROW_BLOCK, LANE_BLOCK), F32) + csb_ref[:, cols]
            for j in range(SSD_CONV):
                s = hb + rb * ROW_BLOCK - (SSD_CONV - 1) + j
                xc = xc + csw_ref[j:j + 1, cols] * proj[s:s + ROW_BLOCK, pcols]
            xact[rb * ROW_BLOCK:(rb + 1) * ROW_BLOCK, cols] = _silu(xc)

    a_col = -jnp.exp(alc_ref[...])
    a_row = -jnp.exp(alr_ref[...])
    ri = lax.broadcasted_iota(jnp.int32, (q, q), 0)
    ci = lax.broadcasted_iota(jnp.int32, (q, q), 1)
    causal = ri >= ci
    tril = jnp.where(causal, 1.0, 0.0).astype(BF16)
    triu = jnp.where(ri <= ci, 1.0, 0.0).astype(BF16)
    hd = SSD_HEAD_DIM

    for c in range(tl // q):
        rows = slice(c * q, (c + 1) * q)
        prow = slice(hb + c * q, hb + (c + 1) * q)
        dt_c = _softplus(proj[prow, o_dt:o_dt + LANES] + dtbc_ref[...])
        dt_r = _softplus(dtr[:, rows] + dtbr_ref[...])
        acum_c = _dot_sel_left(tril, dt_c * a_col)
        acum_r = _dot_sel_right(dt_r * a_row, triu)
        to_end_r = dt_r * jnp.exp(acum_r[:, q - 1:q] - acum_r)
        for g in range(SSD_GROUPS):
            bg = xact[rows, w_ssd + g * SSD_STATE:w_ssd + (g + 1) * SSD_STATE]
            cg = xact[rows, w_ssd + n_bc + g * SSD_STATE:w_ssd + n_bc + (g + 1) * SSD_STATE].astype(BF16)
            cb = _dot_nt(cg, bg.astype(BF16))
            bgt = bg.T
            hg = hstate[g]
            yoff = _dot(cg, hg.astype(BF16))
            for r in range(hpg):
                h = g * hpg + r
                xh = xact[rows, h * hd:(h + 1) * hd].astype(BF16)
                acol = jnp.broadcast_to(acum_c[:, h:h + 1], (q, q))
                dec = jnp.where(causal, jnp.exp(acol - acum_r[h:h + 1, :]), 0.0)
                yd = _dot((cb * dec * dt_r[h:h + 1, :]).astype(BF16), xh)
                ysd[rows, h * hd:(h + 1) * hd] = yd + yoff[:, r * hd:(r + 1) * hd] * jnp.exp(acol[:, 0:hd])
                st = _dot((bgt * to_end_r[h:h + 1, :]).astype(BF16), xh)
                hstate[g, :, r * hd:(r + 1) * hd] = jnp.exp(acol[q - 1:q, 0:hd]) * hg[:, r * hd:(r + 1) * hd] + st
        xs = xact[rows, 0:w_ssd]
        yz = (ysd[rows, :] + dsk_ref[...] * xs) * _silu(proj[prow, o_z:o_z + w_ssd])
        for g in range(SSD_GROUPS):
            yg = yz[:, g * gw:(g + 1) * gw]
            yn = yg * lax.rsqrt(jnp.mean(yg * yg, axis=-1, keepdims=True) + EPS) * snw_ref[:, g * gw:(g + 1) * gw]
            ycat[rows, w_a + g * gw:w_a + (g + 1) * gw] = yn.astype(BF16)

    out_ref[...] = x_ref[...] + _dot(ycat[...], wout_ref[...])


def _mixer_layer(x, norm_w, w_in, conv_a_w, conv_ssd_w, conv_ssd_b, dt_bias, a_log, d_skip, ssd_norm_w,
                 conv_conf_w, conv_conf_b, conf_ln_g, conf_ln_b, w_out, tl):
    bsz, seq, d = x.shape
    w_a = conv_a_w.shape[1]
    w_ssd = ssd_norm_w.shape[0]
    w_conf = conv_conf_w.shape[1]
    n_heads = dt_bias.shape[0]
    n_bc = SSD_GROUPS * SSD_STATE
    xbc = w_ssd + 2 * n_bc
    assert n_heads == SUBLANES and n_heads * SSD_HEAD_DIM == w_ssd and seq % tl == 0 and tl % SSD_CHUNK == 0
    assert tl % ROW_BLOCK == 0 and xbc % LANE_BLOCK == 0 and conv_a_w.shape[0] == A_CONV and conv_ssd_w.shape[0] == SSD_CONV
    assert conv_conf_w.shape[0] == CONF_KERNEL
    o_dt = 3 * w_a + w_ssd + xbc
    w_dt = w_in[:, o_dt:o_dt + n_heads]
    w_main = jnp.concatenate(
        [w_in[:, :o_dt], w_in[:, o_dt + n_heads:], jnp.pad(w_dt, ((0, 0), (0, LANES - n_heads)))], axis=1).astype(BF16)
    n_cols = w_main.shape[1]
    w_dt_t = w_dt.T.astype(BF16)
    row = lambda v: v.reshape(1, -1).astype(F32)
    pad_lanes = lambda v: jnp.pad(v.reshape(1, -1).astype(F32), ((0, 0), (0, LANES - v.shape[0])))
    col = lambda v: v.reshape(-1, 1).astype(F32)
    d_skip_x = jnp.repeat(d_skip.astype(F32), SSD_HEAD_DIM).reshape(1, w_ssd)

    const = lambda shape: pl.BlockSpec(shape, lambda b, l: (0,) * len(shape))
    dims = (tl, w_a, w_ssd, w_conf, n_heads)
    return pl.pallas_call(
        functools.partial(_mixer_kernel, dims),
        out_shape=jax.ShapeDtypeStruct((bsz, seq, d), F32),
        grid=(bsz, seq // tl),
        in_specs=[
            pl.BlockSpec((None, tl, d), lambda b, l: (b, l, 0)),
            const((1, d)), const((d, n_cols)), const((n_heads, d)),
            const((A_CONV, w_a)), const((SSD_CONV, xbc)), const((1, xbc)),
            const((1, LANES)), const((n_heads, 1)), const((1, LANES)), const((n_heads, 1)),
            const((1, w_ssd)), const((1, w_ssd)),
            const((CONF_KERNEL, w_conf)), const((1, w_conf)), const((1, w_conf)), const((1, w_conf)),
            const((d, d)),
        ],
        out_specs=pl.BlockSpec((None, tl, d), lambda b, l: (b, l, 0)),
        scratch_shapes=[
            pltpu.VMEM((tl, d), BF16),
            pltpu.VMEM((SMALL_HALO + tl, n_cols), F32),
            pltpu.VMEM((SMALL_HALO + tl, w_a), F32),
            pltpu.VMEM((CONF_HALO + tl, w_conf), F32),
            pltpu.VMEM((SUBLANES - 1, tl + CONF_HALO - SUBLANES, w_conf), F32),
            pltpu.VMEM((tl, xbc), F32),
            pltpu.VMEM((n_heads, tl), F32),
            pltpu.VMEM((tl, w_ssd), F32),
            pltpu.VMEM((tl, d), BF16),
            pltpu.VMEM((SSD_GROUPS, SSD_STATE, w_ssd // SSD_GROUPS), F32),
        ],
        compiler_params=pltpu.CompilerParams(
            dimension_semantics=("arbitrary", "arbitrary"), vmem_limit_bytes=VMEM_LIMIT_BYTES),
        name="mixer",
    )(x, row(norm_w), w_main, w_dt_t, conv_a_w.astype(F32), conv_ssd_w.astype(F32), row(conv_ssd_b),
      pad_lanes(dt_bias), col(dt_bias), pad_lanes(a_log), col(a_log), d_skip_x, row(ssd_norm_w),
      conv_conf_w.astype(F32), row(conv_conf_b), row(conf_ln_g), row(conf_ln_b), w_out.astype(BF16))


def _dense_ffn_kernel(final_norm, x_ref, nw_ref, wg_ref, wu_ref, wd_ref, fw_ref, out_ref, hn):
    hn[...] = _rmsnorm(x_ref[...], nw_ref[...]).astype(BF16)
    a = (_silu(_dot(hn[...], wg_ref[...])) * _dot(hn[...], wu_ref[...])).astype(BF16)
    y = x_ref[...] + _dot(a, wd_ref[...])
    out_ref[...] = _rmsnorm(y, fw_ref[...]) if final_norm else y


def _dense_ffn(x2, norm_w, w_gate, w_up, w_down, final_w, tm):
    t, d = x2.shape
    ff = w_gate.shape[1]
    assert t % tm == 0
    final_norm = final_w is not None
    fw = (final_w if final_norm else jnp.ones((d,), F32)).reshape(1, d).astype(F32)
    resident = lambda shape: pl.BlockSpec(shape, lambda i: (0, 0), pipeline_mode=pl.Buffered(1))
    return pl.pallas_call(
        functools.partial(_dense_ffn_kernel, final_norm),
        out_shape=jax.ShapeDtypeStruct((t, d), F32),
        grid=(t // tm,),
        in_specs=[
            pl.BlockSpec((tm, d), lambda i: (i, 0)),
            resident((1, d)), resident((d, ff)), resident((d, ff)), resident((ff, d)), resident((1, d)),
        ],
        out_specs=pl.BlockSpec((tm, d), lambda i: (i, 0)),
        scratch_shapes=[pltpu.VMEM((tm, d), BF16)],
        compiler_params=pltpu.CompilerParams(
            dimension_semantics=("arbitrary",), vmem_limit_bytes=VMEM_LIMIT_BYTES),
        name="dense_ffn",
    )(x2, norm_w.reshape(1, d).astype(F32), w_gate.astype(BF16), w_up.astype(BF16), w_down.astype(BF16), fw)


def _rows_to_tiles(tile_ref, val):
    n, d = val.shape
    n_sub = d // LANES
    for c in range(n_sub):
        tile_ref[pl.ds(c, n, stride=n_sub), :] = val[:, c * LANES:(c + 1) * LANES]


def _tiles_to_rows(tile_ref, n_sub):
    n = tile_ref.shape[0] // n_sub
    return jnp.concatenate([tile_ref[pl.ds(c, n, stride=n_sub), :] for c in range(n_sub)], axis=1)


def _expert_ffn_kernel(blk_e_ref, nact_ref, xb_ref, wg_ref, wu_ref, wd_ref, out_ref, hb, acc):
    i = pl.program_id(0)
    j = pl.program_id(1)

    @pl.when(i < nact_ref[0])
    def _():
        @pl.when(j == 0)
        def _():
            hb[...] = _tiles_to_rows(xb_ref, hb.shape[1] // LANES).astype(BF16)
            acc[...] = jnp.zeros(acc.shape, F32)

        h = hb[...]
        a = (_silu(_dot(h, wg_ref[...])) * _dot(h, wu_ref[...])).astype(BF16)
        acc[...] += _dot(a, wd_ref[...])

        @pl.when(j == pl.num_programs(1) - 1)
        def _():
            _rows_to_tiles(out_ref, acc[...])

    @pl.when((i >= nact_ref[0]) & (j == 0))
    def _():
        out_ref[...] = jnp.zeros(out_ref.shape, F32)


def _expert_ffn(buf, blk_e, nact, w_gate, w_up, w_down, bm, tf):
    d, ff = w_gate.shape[1], w_gate.shape[2]
    n_sub = d // LANES
    cap = buf.shape[0] // n_sub
    n_blocks = cap // bm
    n_j = ff // tf
    assert cap % bm == 0 and ff % tf == 0
    row_blk = lambda i, j, be, na: (jnp.minimum(i, na[0] - 1), 0)
    snake = lambda i, j: jnp.where(i % 2 == 0, j, n_j - 1 - j)
    col_j = lambda i, j, na: jnp.where(i < na[0], snake(i, j), snake(na[0] - 1, n_j - 1))
    return pl.pallas_call(
        _expert_ffn_kernel,
        out_shape=jax.ShapeDtypeStruct((cap * n_sub, LANES), F32),
        grid_spec=pltpu.PrefetchScalarGridSpec(
            num_scalar_prefetch=2,
            grid=(n_blocks, n_j),
            in_specs=[
                pl.BlockSpec((bm * n_sub, LANES), row_blk),
                pl.BlockSpec((None, d, tf), lambda i, j, be, na: (be[i], 0, col_j(i, j, na))),
                pl.BlockSpec((None, d, tf), lambda i, j, be, na: (be[i], 0, col_j(i, j, na))),
                pl.BlockSpec((None, tf, d), lambda i, j, be, na: (be[i], col_j(i, j, na), 0)),
            ],
            out_specs=pl.BlockSpec((bm * n_sub, LANES), lambda i, j, be, na: (i, 0)),
            scratch_shapes=[pltpu.VMEM((bm, d), BF16), pltpu.VMEM((bm, d), F32)],
        ),
        compiler_params=pltpu.CompilerParams(
            dimension_semantics=("arbitrary", "arbitrary"), vmem_limit_bytes=VMEM_LIMIT_BYTES),
        name="expert_ffn",
    )(blk_e, nact, buf, w_gate.astype(BF16), w_up.astype(BF16), w_down.astype(BF16))


def _router_kernel(x_ref, nw_ref, wr_ref, hn_ref, route_ref):
    h = _rmsnorm(x_ref[...], nw_ref[...])
    _rows_to_tiles(hn_ref, h)
    h3 = _split3(h)
    w3 = _split3(wr_ref[...])
    logits = jnp.zeros((h.shape[0], LANES), F32)
    for a in range(3):
        for b in range(3 - a):
            logits = logits + _dot(h3[a], w3[b])
    lane = lax.broadcasted_iota(jnp.int32, logits.shape, 1)
    neg = jnp.float32(-jnp.inf)
    logits = jnp.where(lane < N_EXPERTS, logits, neg)
    m1 = jnp.max(logits, axis=-1, keepdims=True)
    i1 = jnp.min(jnp.where(logits == m1, lane, LANES), axis=-1, keepdims=True)
    rest = jnp.where(lane == i1, neg, logits)
    m2 = jnp.max(rest, axis=-1, keepdims=True)
    i2 = jnp.min(jnp.where(rest == m2, lane, LANES), axis=-1, keepdims=True)
    e2 = jnp.exp(m2 - m1)
    g1 = 1.0 / (1.0 + e2)
    g2 = e2 * g1
    route = jnp.where(lane == 0, i1.astype(F32), jnp.where(lane == 1, i2.astype(F32),
                      jnp.where(lane == 2, g1, jnp.where(lane == 3, g2, 0.0))))
    route_ref[...] = route


def _router(x2, norm_w, w_router, tm):
    t, d = x2.shape
    n_e = w_router.shape[1]
    assert n_e == N_EXPERTS and t % tm == 0 and d % LANES == 0
    n_sub = d // LANES
    wr = jnp.pad(w_router.astype(F32), ((0, 0), (0, LANES - n_e)))
    return pl.pallas_call(
        _router_kernel,
        out_shape=(jax.ShapeDtypeStruct((t * n_sub, LANES), F32), jax.ShapeDtypeStruct((t, LANES), F32)),
        grid=(t // tm,),
        in_specs=[
            pl.BlockSpec((tm, d), lambda i: (i, 0)),
            pl.BlockSpec((1, d), lambda i: (0, 0)),
            pl.BlockSpec((d, LANES), lambda i: (0, 0)),
        ],
        out_specs=(pl.BlockSpec((tm * n_sub, LANES), lambda i: (i, 0)),
                   pl.BlockSpec((tm, LANES), lambda i: (i, 0))),
        compiler_params=pltpu.CompilerParams(
            dimension_semantics=("arbitrary",), vmem_limit_bytes=VMEM_LIMIT_BYTES),
        name="router",
    )(x2, norm_w.reshape(1, d).astype(F32), wr)


def _row_copy(n_sub, src_ref, src_row, dst_ref, dst_row, sem):
    return pltpu.make_async_copy(src_ref.at[pl.ds(pl.multiple_of(src_row, n_sub), n_sub)],
                                 dst_ref.at[pl.ds(pl.multiple_of(dst_row, n_sub), n_sub)], sem)


def _tile_indices(dest, tile):
    k, t = dest.shape
    return dest.reshape(k, t // tile, tile).transpose(1, 0, 2).reshape(t // tile, 1, k * tile)


def _scatter_kernel(ts, n_sub, dest_ref, pad_ref, hn_ref, buf_ref, zero_tile, sem, pad_sem):
    @pl.when(pl.program_id(0) == 0)
    def _():
        zero_tile[...] = jnp.zeros(zero_tile.shape, F32)
        n_ranges = pad_ref.shape[0] // 2
        for e in range(n_ranges):
            first, count = pad_ref[e], pad_ref[n_ranges + e]

            def fill(p, carry):
                _row_copy(n_sub, zero_tile, 0, buf_ref, (first + p) * n_sub, pad_sem).start()
                return carry

            def fill_done(p, carry):
                _row_copy(n_sub, zero_tile, 0, buf_ref, 0, pad_sem).wait()
                return carry

            lax.fori_loop(0, count, fill, 0)
            lax.fori_loop(0, count, fill_done, 0)

    def issue(rb, carry):
        for u in range(DMA_UNROLL):
            r = rb * DMA_UNROLL + u
            for k in range(TOP_K):
                _row_copy(n_sub, hn_ref, r * n_sub, buf_ref, dest_ref[0, 0, k * ts + r], sem).start(priority=k % 2)
        return carry

    lax.fori_loop(0, ts // DMA_UNROLL, issue, 0)
    for k in range(TOP_K):
        pltpu.make_async_copy(hn_ref, buf_ref.at[pl.ds(0, ts * n_sub)], sem).wait()


def _scatter_rows(hn, dest, pad_slots, cap, ts, n_sub):
    t = hn.shape[0] // n_sub
    assert t % ts == 0 and ts % DMA_UNROLL == 0
    return pl.pallas_call(
        functools.partial(_scatter_kernel, ts, n_sub),
        out_shape=jax.ShapeDtypeStruct((cap * n_sub, LANES), F32),
        grid=(t // ts,),
        in_specs=[
            pl.BlockSpec((1, 1, TOP_K * ts), lambda i: (i, 0, 0), memory_space=pltpu.SMEM),
            pl.BlockSpec(memory_space=pltpu.SMEM),
            pl.BlockSpec((ts * n_sub, LANES), lambda i: (i, 0)),
        ],
        out_specs=pl.BlockSpec(memory_space=pl.ANY),
        scratch_shapes=[pltpu.VMEM((n_sub, LANES), F32), pltpu.SemaphoreType.DMA(()), pltpu.SemaphoreType.DMA(())],
        compiler_params=pltpu.CompilerParams(
            dimension_semantics=("arbitrary",), vmem_limit_bytes=VMEM_LIMIT_BYTES),
        name="scatter_rows",
    )(_tile_indices(dest, ts), pad_slots, hn)


def _combine_kernel(tc, final_norm, dest_ref, dest_next_ref, x_ref, route_ref, fw_ref, ybuf_ref, out_ref, rows, sems):
    n_sub = x_ref.shape[1] // LANES
    i = pl.program_id(0)
    slot = i % 2

    def gather(idx_ref, s):
        def issue(rb, carry):
            for u in range(DMA_UNROLL):
                r = rb * DMA_UNROLL + u
                for k in range(TOP_K):
                    _row_copy(n_sub, ybuf_ref, idx_ref[0, 0, k * tc + r], rows.at[s, k], r * n_sub,
                              sems.at[s]).start(priority=k % 2)
            return carry

        lax.fori_loop(0, tc // DMA_UNROLL, issue, 0)

    @pl.when(i == 0)
    def _():
        gather(dest_ref, 0)

    @pl.when(i + 1 < pl.num_programs(0))
    def _():
        gather(dest_next_ref, 1 - slot)

    for k in range(TOP_K):
        pltpu.make_async_copy(ybuf_ref.at[pl.ds(0, tc * n_sub)], rows.at[slot, k], sems.at[slot]).wait()

    route = route_ref[...]
    y = x_ref[...]
    for k in range(TOP_K):
        y = y + route[:, TOP_K + k:TOP_K + k + 1] * _tiles_to_rows(rows.at[slot, k], n_sub)
    out_ref[...] = _rmsnorm(y, fw_ref[...]) if final_norm else y


def _combine(x2, route, dest, ybuf, final_w, tc):
    t, d = x2.shape
    assert t % tc == 0 and tc % DMA_UNROLL == 0
    n_tiles = t // tc
    final_norm = final_w is not None
    fw = (final_w if final_norm else jnp.ones((d,), F32)).reshape(1, d).astype(F32)
    idx = _tile_indices(dest, tc)
    return pl.pallas_call(
        functools.partial(_combine_kernel, tc, final_norm),
        out_shape=jax.ShapeDtypeStruct((t, d), F32),
        grid=(n_tiles,),
        in_specs=[
            pl.BlockSpec((1, 1, TOP_K * tc), lambda i: (i, 0, 0), memory_space=pltpu.SMEM),
            pl.BlockSpec((1, 1, TOP_K * tc), lambda i: (jnp.minimum(i + 1, n_tiles - 1), 0, 0), memory_space=pltpu.SMEM),
            pl.BlockSpec((tc, d), lambda i: (i, 0)),
            pl.BlockSpec((tc, LANES), lambda i: (i, 0)),
            pl.BlockSpec((1, d), lambda i: (0, 0)),
            pl.BlockSpec(memory_space=pl.ANY),
        ],
        out_specs=pl.BlockSpec((tc, d), lambda i: (i, 0)),
        scratch_shapes=[pltpu.VMEM((2, TOP_K, tc * (d // LANES), LANES), F32), pltpu.SemaphoreType.DMA((2,))],
        compiler_params=pltpu.CompilerParams(
            dimension_semantics=("arbitrary",), vmem_limit_bytes=VMEM_LIMIT_BYTES),
        name="combine",
    )(idx, idx, x2, route, fw, ybuf)


def _moe_layer(x2, norm_w, w_router, w_gate, w_up, w_down, final_w, tiles):
    t, d = x2.shape
    bm = tiles["moe_bm"]
    hn, route = _router(x2, norm_w, w_router, tiles["router_tm"])
    e_idx = route[:, :TOP_K].astype(jnp.int32).T
    onehot = (e_idx.reshape(-1)[None, :] == jnp.arange(N_EXPERTS, dtype=jnp.int32)[:, None]).astype(jnp.int32)
    csum = jnp.cumsum(onehot, axis=1)
    rank = jnp.sum(csum * onehot, axis=0) - 1
    sizes = csum[:, -1]
    padded = ((sizes + bm - 1) // bm) * bm
    ends = jnp.cumsum(padded)
    starts = ends - padded
    dest = (jnp.sum(starts[:, None] * onehot, axis=0) + rank).astype(jnp.int32).reshape(TOP_K, t)
    cap = TOP_K * t + N_EXPERTS * bm
    n_blocks = cap // bm
    blk_e = jnp.minimum(
        jnp.searchsorted(ends, jnp.arange(n_blocks, dtype=jnp.int32) * bm, side="right"), N_EXPERTS - 1
    ).astype(jnp.int32)
    nact = (ends[-1:] // bm).astype(jnp.int32)

    n_sub = d // LANES
    dest_rows = dest * n_sub
    pad_slots = jnp.concatenate([starts + sizes, ends[-1:], padded - sizes, cap - ends[-1:]]).astype(jnp.int32)
    buf = _scatter_rows(hn, dest_rows, pad_slots, cap, tiles["scatter_ts"], n_sub)
    ybuf = _expert_ffn(buf, blk_e, nact, w_gate, w_up, w_down, bm, tiles["moe_tf"])
    return _combine(x2, route, dest_rows, ybuf, final_w, tiles["combine_tc"])


TILES = {
    "mixer_tl": 256,
    "ffn_tm": 512,
    "router_tm": 512,
    "moe_bm": 512, "moe_tf": 1792,
    "scatter_ts": 1024,
    "combine_tc": 256,
}


def kernel(x, norm_mix, w_in, conv_a_w, conv_ssd_w, conv_ssd_b, dt_bias, a_log, d_skip, ssd_norm_w, conv_conf_w, conv_conf_b, conf_ln_g, conf_ln_b, w_out, norm_ffn, ffn_w_gate, ffn_w_up, ffn_w_down, moe_router, moe_w_gate, moe_w_up, moe_w_down, norm_final):
    bsz, seq, d = x.shape
    depth = norm_mix.shape[0]
    tiles = dict(TILES)
    tiles["mixer_tl"] = min(tiles["mixer_tl"], seq)
    for i in range(depth):
        x = _mixer_layer(x, norm_mix[i], w_in[i], conv_a_w[i], conv_ssd_w[i], conv_ssd_b[i], dt_bias[i], a_log[i],
                         d_skip[i], ssd_norm_w[i], conv_conf_w[i], conv_conf_b[i], conf_ln_g[i], conf_ln_b[i],
                         w_out[i], tiles["mixer_tl"])
        x2 = x.reshape(bsz * seq, d)
        final_w = norm_final if i == depth - 1 else None
        j = i // 2
        if i % 2 == 0:
            x2 = _dense_ffn(x2, norm_ffn[i], ffn_w_gate[j], ffn_w_up[j], ffn_w_down[j], final_w,
                            tiles["ffn_tm"])
        else:
            x2 = _moe_layer(x2, norm_ffn[i], moe_router[j], moe_w_gate[j], moe_w_up[j], moe_w_down[j], final_w, tiles)
        x = x2.reshape(bsz, seq, d)
    return x
```

```python
import functools

import jax
import jax.numpy as jnp
from jax import lax
from jax.experimental import pallas as pl
from jax.experimental.pallas import tpu as pltpu

F32 = jnp.float32
BF16 = jnp.bfloat16
EPS = 1e-5

LANES = 128
SUBLANES = 8
VMEM_LIMIT_BYTES = 56 * 1024 * 1024

GROUP_DIM = 64
A_CONV = 3
SSD_HEAD_DIM = 64
SSD_GROUPS = 2
SSD_STATE = 128
SSD_CONV = 4
CONF_KERNEL = 31
N_EXPERTS = 8
TOP_K = 2

SSD_CHUNK = 128
CONF_HALO = 32
SMALL_HALO = SUBLANES
CONV_ROWS = 32
ROW_BLOCK = 64
LANE_BLOCK = 256
DMA_UNROLL = 8


def _split3(v):
    hi = v.astype(BF16)
    r1 = v - hi.astype(F32)
    mid = r1.astype(BF16)
    lo = (r1 - mid.astype(F32)).astype(BF16)
    return hi, mid, lo


def _dot(a, b):
    return jnp.dot(a, b, preferred_element_type=F32)


def _dot_nt(a, b):
    return lax.dot_general(a, b, (((1,), (1,)), ((), ())), preferred_element_type=F32)


def _dot_sel_right(v, sel):
    hi, mid, lo = _split3(v)
    return _dot(hi, sel) + _dot(mid, sel) + _dot(lo, sel)


def _dot_sel_left(sel, v):
    hi, mid, lo = _split3(v)
    return _dot(sel, hi) + _dot(sel, mid) + _dot(sel, lo)


def _softplus(v):
    return jnp.maximum(v, 0.0) + jnp.log1p(jnp.exp(-jnp.abs(v)))


def _silu(v):
    return v * jax.nn.sigmoid(v)


def _rmsnorm(x, w):
    return x * lax.rsqrt(jnp.mean(x * x, axis=-1, keepdims=True) + EPS) * w


def _mixer_kernel(dims, x_ref, nw_ref, win_ref, wdtT_ref, caw_ref, csw_ref, csb_ref, dtbc_ref, dtbr_ref,
                  alc_ref, alr_ref, dsk_ref, snw_ref, ccw_ref, ccb_ref, lng_ref, lnb_ref, wout_ref,
                  out_ref,
                  hn_s, p_a, p_c, p_s, p_z, p_dt, dtr, buf_a, buf_c, shc, xact, ysd, ycat, hstate):
    tl, w_a, w_ssd, w_conf, n_heads, n_l = dims
    q = SSD_CHUNK
    n_bc = SSD_GROUPS * SSD_STATE
    xbc = w_ssd + 2 * n_bc
    hpg = n_heads // SSD_GROUPS
    gw = hpg * SSD_HEAD_DIM
    hb = SMALL_HALO
    o_z = 3 * w_a
    o_xbc = o_z + w_ssd
    o_c = o_xbc + xbc
    o_dt = o_c + 2 * w_conf
    lt = lax.rem(pl.program_id(0), n_l)

    @pl.when(lt == 0)
    def _():
        buf_a[0:hb, :] = jnp.zeros((hb, w_a), F32)
        p_s[0:hb, :] = jnp.zeros((hb, xbc), F32)
        buf_c[0:CONF_HALO, :] = jnp.zeros((CONF_HALO, w_conf), F32)
        hstate[...] = jnp.zeros(hstate.shape, F32)

    @pl.when(lt > 0)
    def _():
        buf_a[0:hb, :] = buf_a[tl:tl + hb, :]
        p_s[0:hb, :] = p_s[tl:tl + hb, :]
        buf_c[0:CONF_HALO, :] = buf_c[tl:tl + CONF_HALO, :]

    hn_s[...] = _rmsnorm(x_ref[...], nw_ref[...]).astype(BF16)

    p_a[...] = _dot(hn_s[...], win_ref[:, 0:3 * w_a])
    for rb in range(tl // ROW_BLOCK):
        r0 = rb * ROW_BLOCK
        b0 = hb + r0
        buf_a[b0:b0 + ROW_BLOCK, :] = p_a[r0:r0 + ROW_BLOCK, w_a:2 * w_a] * p_a[r0:r0 + ROW_BLOCK, 2 * w_a:3 * w_a]
        ya = jnp.zeros((ROW_BLOCK, w_a), F32)
        for j in range(A_CONV):
            s = b0 - (A_CONV - 1) + j
            ya = ya + caw_ref[j:j + 1, :] * buf_a[s:s + ROW_BLOCK, :]
        ycat[r0:r0 + ROW_BLOCK, 0:w_a] = (p_a[r0:r0 + ROW_BLOCK, 0:w_a] * ya).astype(BF16)

    p_c[...] = _dot(hn_s[...], win_ref[:, o_c:o_c + 2 * w_conf])
    for rb in range(tl // ROW_BLOCK):
        r0 = rb * ROW_BLOCK
        buf_c[CONF_HALO + r0:CONF_HALO + r0 + ROW_BLOCK, :] = (
            p_c[r0:r0 + ROW_BLOCK, 0:w_conf] * jax.nn.sigmoid(p_c[r0:r0 + ROW_BLOCK, w_conf:2 * w_conf]))
    n_sh = shc.shape[1]
    for r in range(1, SUBLANES):
        shc[r - 1] = buf_c[r:r + n_sh, :]
    for rb in range(tl // CONV_ROWS):
        acc = jnp.zeros((CONV_ROWS, w_conf), F32) + ccb_ref[...]
        for j in range(CONF_KERNEL):
            a, r = divmod(CONF_HALO - (CONF_KERNEL - 1) + j, SUBLANES)
            base = a * SUBLANES + rb * CONV_ROWS
            src = buf_c[base:base + CONV_ROWS, :] if r == 0 else shc[r - 1, base:base + CONV_ROWS, :]
            acc = acc + ccw_ref[j:j + 1, :] * src
        mu = jnp.mean(acc, axis=-1, keepdims=True)
        cen = acc - mu
        var = jnp.mean(cen * cen, axis=-1, keepdims=True)
        yc = cen * lax.rsqrt(var + EPS) * lng_ref[...] + lnb_ref[...]
        ycat[rb * CONV_ROWS:(rb + 1) * CONV_ROWS, w_a + w_ssd:w_a + w_ssd + w_conf] = _silu(yc).astype(BF16)

    p_s[hb:hb + tl, :] = _dot(hn_s[...], win_ref[:, o_xbc:o_xbc + xbc])
    for rb in range(tl // ROW_BLOCK):
        for lb in range(xbc // LANE_BLOCK):
            cols = slice(lb * LANE_BLOCK, (lb + 1) * LANE_BLOCK)
            xc = jnp.zeros((ROW_BLOCK, LANE_BLOCK), F32) + csb_ref[:, cols]
            for j in range(SSD_CONV):
                s = hb + rb * ROW_BLOCK - (SSD_CONV - 1) + j
                xc = xc + csw_ref[j:j + 1, cols] * p_s[s:s + ROW_BLOCK, cols]
            xact[rb * ROW_BLOCK:(rb + 1) * ROW_BLOCK, cols] = _silu(xc)
    p_z[...] = _dot(hn_s[...], win_ref[:, o_z:o_z + w_ssd])
    p_dt[...] = _dot(hn_s[...], win_ref[:, o_dt:o_dt + LANES])
    dtr[...] = _dot_nt(wdtT_ref[...], hn_s[...])

    a_col = -jnp.exp(alc_ref[...])
    a_row = -jnp.exp(alr_ref[...])
    ri = lax.broadcasted_iota(jnp.int32, (q, q), 0)
    ci = lax.broadcasted_iota(jnp.int32, (q, q), 1)
    causal = ri >= ci
    tril = jnp.where(causal, 1.0, 0.0).astype(BF16)
    triu = jnp.where(ri <= ci, 1.0, 0.0).astype(BF16)
    hd = SSD_HEAD_DIM

    for c in range(tl // q):
        rows = slice(c * q, (c + 1) * q)
        dt_c = _softplus(p_dt[rows, :] + dtbc_ref[...])
        dt_r = _softplus(dtr[:, rows] + dtbr_ref[...])
        acum_c = _dot_sel_left(tril, dt_c * a_col)
        acum_r = _dot_sel_right(dt_r * a_row, triu)
        to_end_r = dt_r * jnp.exp(acum_r[:, q - 1:q] - acum_r)
        for g in range(SSD_GROUPS):
            bg = xact[rows, w_ssd + g * SSD_STATE:w_ssd + (g + 1) * SSD_STATE]
            cg = xact[rows, w_ssd + n_bc + g * SSD_STATE:w_ssd + n_bc + (g + 1) * SSD_STATE].astype(BF16)
            cb = _dot_nt(cg, bg.astype(BF16))
            bgt = bg.T
            hg = hstate[g]
            yoff = _dot(cg, hg.astype(BF16))
            for r in range(hpg):
                h = g * hpg + r
                xh = xact[rows, h * hd:(h + 1) * hd].astype(BF16)
                acol = jnp.broadcast_to(acum_c[:, h:h + 1], (q, q))
                dec = jnp.where(causal, jnp.exp(acol - acum_r[h:h + 1, :]), 0.0)
                yd = _dot((cb * dec * dt_r[h:h + 1, :]).astype(BF16), xh)
                ysd[rows, h * hd:(h + 1) * hd] = yd + yoff[:, r * hd:(r + 1) * hd] * jnp.exp(acol[:, 0:hd])
                st = _dot((bgt * to_end_r[h:h + 1, :]).astype(BF16), xh)
                hstate[g, :, r * hd:(r + 1) * hd] = jnp.exp(acol[q - 1:q, 0:hd]) * hg[:, r * hd:(r + 1) * hd] + st
        xs = xact[rows, 0:w_ssd]
        yz = (ysd[rows, :] + dsk_ref[...] * xs) * _silu(p_z[rows, :])
        for g in range(SSD_GROUPS):
            yg = yz[:, g * gw:(g + 1) * gw]
            yn = yg * lax.rsqrt(jnp.mean(yg * yg, axis=-1, keepdims=True) + EPS) * snw_ref[:, g * gw:(g + 1) * gw]
            ycat[rows, w_a + g * gw:w_a + (g + 1) * gw] = yn.astype(BF16)

    out_ref[...] = x_ref[...] + _dot(ycat[...], wout_ref[...])


def _mixer_layer(x, norm_w, w_in, conv_a_w, conv_ssd_w, conv_ssd_b, dt_bias, a_log, d_skip, ssd_norm_w,
                 conv_conf_w, conv_conf_b, conf_ln_g, conf_ln_b, w_out, tl):
    bsz, seq, d = x.shape
    w_a = conv_a_w.shape[1]
    w_ssd = ssd_norm_w.shape[0]
    w_conf = conv_conf_w.shape[1]
    n_heads = dt_bias.shape[0]
    n_bc = SSD_GROUPS * SSD_STATE
    xbc = w_ssd + 2 * n_bc
    assert n_heads == SUBLANES and n_heads * SSD_HEAD_DIM == w_ssd and seq % tl == 0 and tl % SSD_CHUNK == 0
    assert tl % ROW_BLOCK == 0 and xbc % LANE_BLOCK == 0 and conv_a_w.shape[0] == A_CONV and conv_ssd_w.shape[0] == SSD_CONV
    assert conv_conf_w.shape[0] == CONF_KERNEL
    o_dt = 3 * w_a + w_ssd + xbc
    w_dt = w_in[:, o_dt:o_dt + n_heads]
    w_main = jnp.concatenate(
        [w_in[:, :o_dt], w_in[:, o_dt + n_heads:], jnp.pad(w_dt, ((0, 0), (0, LANES - n_heads)))], axis=1).astype(BF16)
    n_cols = w_main.shape[1]
    w_dt_t = w_dt.T.astype(BF16)
    row = lambda v: v.reshape(1, -1).astype(F32)
    pad_lanes = lambda v: jnp.pad(v.reshape(1, -1).astype(F32), ((0, 0), (0, LANES - v.shape[0])))
    col = lambda v: v.reshape(-1, 1).astype(F32)
    d_skip_x = jnp.repeat(d_skip.astype(F32), SSD_HEAD_DIM).reshape(1, w_ssd)

    const = lambda shape: pl.BlockSpec(shape, lambda s: (0,) * len(shape))
    n_l = seq // tl
    n_tiles = bsz * n_l
    dims = (tl, w_a, w_ssd, w_conf, n_heads, n_l)
    x2 = x.reshape(bsz * seq, d)
    out = pl.pallas_call(
        functools.partial(_mixer_kernel, dims),
        out_shape=jax.ShapeDtypeStruct((bsz * seq, d), F32),
        grid=(n_tiles,),
        in_specs=[
            pl.BlockSpec((tl, d), lambda s: (s, 0)),
            const((1, d)), const((d, n_cols)), const((n_heads, d)),
            const((A_CONV, w_a)), const((SSD_CONV, xbc)), const((1, xbc)),
            const((1, LANES)), const((n_heads, 1)), const((1, LANES)), const((n_heads, 1)),
            const((1, w_ssd)), const((1, w_ssd)),
            const((CONF_KERNEL, w_conf)), const((1, w_conf)), const((1, w_conf)), const((1, w_conf)),
            const((d, d)),
        ],
        out_specs=pl.BlockSpec((tl, d), lambda s: (s, 0)),
        scratch_shapes=[
            pltpu.VMEM((tl, d), BF16),
            pltpu.VMEM((tl, 3 * w_a), F32),
            pltpu.VMEM((tl, 2 * w_conf), F32),
            pltpu.VMEM((SMALL_HALO + tl, xbc), F32),
            pltpu.VMEM((tl, w_ssd), F32),
            pltpu.VMEM((tl, LANES), F32),
            pltpu.VMEM((n_heads, tl), F32),
            pltpu.VMEM((SMALL_HALO + tl, w_a), F32),
            pltpu.VMEM((CONF_HALO + tl, w_conf), F32),
            pltpu.VMEM((SUBLANES - 1, tl + CONF_HALO - SUBLANES, w_conf), F32),
            pltpu.VMEM((tl, xbc), F32),
            pltpu.VMEM((tl, w_ssd), F32),
            pltpu.VMEM((tl, d), BF16),
            pltpu.VMEM((SSD_GROUPS, SSD_STATE, w_ssd // SSD_GROUPS), F32),
        ],
        compiler_params=pltpu.CompilerParams(
            dimension_semantics=("arbitrary",), vmem_limit_bytes=VMEM_LIMIT_BYTES),
        name="mixer",
    )(x2, row(norm_w), w_main, w_dt_t, conv_a_w.astype(F32), conv_ssd_w.astype(F32), row(conv_ssd_b),
      pad_lanes(dt_bias), col(dt_bias), pad_lanes(a_log), col(a_log), d_skip_x, row(ssd_norm_w),
      conv_conf_w.astype(F32), row(conv_conf_b), row(conf_ln_g), row(conf_ln_b), w_out.astype(BF16))
    return out.reshape(bsz, seq, d)


def _dense_ffn_kernel(final_norm, n_cast, x_ref, nw_ref, wg_ref, wu_ref, wd_ref, fw_ref, *rest):
    cast_in, out_ref, cast_out, hn = rest[:n_cast], rest[n_cast], rest[n_cast + 1:2 * n_cast + 1], rest[-1]
    hn[...] = _rmsnorm(x_ref[...], nw_ref[...]).astype(BF16)
    a = (_silu(_dot(hn[...], wg_ref[...])) * _dot(hn[...], wu_ref[...])).astype(BF16)
    y = x_ref[...] + _dot(a, wd_ref[...])
    out_ref[...] = _rmsnorm(y, fw_ref[...]) if final_norm else y
    for src, dst in zip(cast_in, cast_out):
        dst[...] = src[...].astype(BF16)


def _dense_ffn(x2, norm_w, w_gate, w_up, w_down, final_w, tm, later=()):
    t, d = x2.shape
    ff = w_gate.shape[1]
    assert t % tm == 0
    n_steps = t // tm
    final_norm = final_w is not None
    fw = (final_w if final_norm else jnp.ones((d,), F32)).reshape(1, d).astype(F32)
    resident = lambda shape: pl.BlockSpec(shape, lambda i: (0, 0), pipeline_mode=pl.Buffered(1))
    bf16_rows = 2 * SUBLANES
    slabs = []
    for w in later:
        rows = w.size // w.shape[-1]
        assert rows % (n_steps * bf16_rows) == 0
        slabs.append((rows // n_steps, w.shape[-1]))
    slab_specs = [pl.BlockSpec(s, lambda i: (i, 0)) for s in slabs]
    res = pl.pallas_call(
        functools.partial(_dense_ffn_kernel, final_norm, len(later)),
        out_shape=[jax.ShapeDtypeStruct((t, d), F32)]
        + [jax.ShapeDtypeStruct((n_steps * r, c), BF16) for r, c in slabs],
        grid=(n_steps,),
        in_specs=[
            pl.BlockSpec((tm, d), lambda i: (i, 0)),
            resident((1, d)), resident((d, ff)), resident((d, ff)), resident((ff, d)), resident((1, d)),
        ] + slab_specs,
        out_specs=[pl.BlockSpec((tm, d), lambda i: (i, 0))] + slab_specs,
        scratch_shapes=[pltpu.VMEM((tm, d), BF16)],
        compiler_params=pltpu.CompilerParams(
            dimension_semantics=("arbitrary",), vmem_limit_bytes=VMEM_LIMIT_BYTES),
        name="dense_ffn",
    )(x2, norm_w.reshape(1, d).astype(F32), w_gate.astype(BF16), w_up.astype(BF16), w_down.astype(BF16), fw,
      *[w.reshape(n_steps * r, c) for w, (r, c) in zip(later, slabs)])
    return res[0], [b.reshape(w.shape) for b, w in zip(res[1:], later)]


def _rows_to_tiles(tile_ref, val):
    n, d = val.shape
    n_sub = d // LANES
    for c in range(n_sub):
        tile_ref[pl.ds(c, n, stride=n_sub), :] = val[:, c * LANES:(c + 1) * LANES]


def _tiles_to_rows(tile_ref, n_sub):
    n = tile_ref.shape[0] // n_sub
    return jnp.concatenate([tile_ref[pl.ds(c, n, stride=n_sub), :] for c in range(n_sub)], axis=1)


def _expert_ffn_kernel(blk_e_ref, nact_ref, xb_ref, wg_ref, wu_ref, wd_ref, out_ref, hb, acc):
    i = pl.program_id(0)
    j = pl.program_id(1)

    @pl.when(i < nact_ref[0])
    def _():
        @pl.when(j == 0)
        def _():
            hb[...] = _tiles_to_rows(xb_ref, hb.shape[1] // LANES).astype(BF16)
            acc[...] = jnp.zeros(acc.shape, F32)

        h = hb[...]
        a = (_silu(_dot(h, wg_ref[...])) * _dot(h, wu_ref[...])).astype(BF16)
        acc[...] += _dot(a, wd_ref[...])

        @pl.when(j == pl.num_programs(1) - 1)
        def _():
            _rows_to_tiles(out_ref, acc[...])

    @pl.when((i >= nact_ref[0]) & (j == 0))
    def _():
        out_ref[...] = jnp.zeros(out_ref.shape, F32)


def _expert_ffn(buf, blk_e, nact, w_gate, w_up, w_down, bm, tf):
    d, ff = w_gate.shape[1], w_gate.shape[2]
    n_sub = d // LANES
    cap = buf.shape[0] // n_sub
    n_blocks = cap // bm
    n_j = ff // tf
    assert cap % bm == 0 and ff % tf == 0
    row_blk = lambda i, j, be, na: (jnp.minimum(i, na[0] - 1), 0)
    snake = lambda i, j: jnp.where(i % 2 == 0, j, n_j - 1 - j)
    col_j = lambda i, j, na: jnp.where(i < na[0], snake(i, j), snake(na[0] - 1, n_j - 1))
    return pl.pallas_call(
        _expert_ffn_kernel,
        out_shape=jax.ShapeDtypeStruct((cap * n_sub, LANES), F32),
        grid_spec=pltpu.PrefetchScalarGridSpec(
            num_scalar_prefetch=2,
            grid=(n_blocks, n_j),
            in_specs=[
                pl.BlockSpec((bm * n_sub, LANES), row_blk),
                pl.BlockSpec((None, d, tf), lambda i, j, be, na: (be[i], 0, col_j(i, j, na))),
                pl.BlockSpec((None, d, tf), lambda i, j, be, na: (be[i], 0, col_j(i, j, na))),
                pl.BlockSpec((None, tf, d), lambda i, j, be, na: (be[i], col_j(i, j, na), 0)),
            ],
            out_specs=pl.BlockSpec((bm * n_sub, LANES), lambda i, j, be, na: (i, 0)),
            scratch_shapes=[pltpu.VMEM((bm, d), BF16), pltpu.VMEM((bm, d), F32)],
        ),
        compiler_params=pltpu.CompilerParams(
            dimension_semantics=("arbitrary", "arbitrary"), vmem_limit_bytes=VMEM_LIMIT_BYTES),
        name="expert_ffn",
    )(blk_e, nact, buf, w_gate, w_up, w_down)


def _router_kernel(x_ref, nw_ref, wr_ref, hn_ref, route_ref):
    h = _rmsnorm(x_ref[...], nw_ref[...])
    _rows_to_tiles(hn_ref, h)
    h_hi, h_lo, _ = _split3(h)
    w_hi, w_lo, _ = _split3(wr_ref[...])
    logits = _dot(h_hi, w_hi) + _dot(h_hi, w_lo) + _dot(h_lo, w_hi)
    lane = lax.broadcasted_iota(jnp.int32, logits.shape, 1)
    neg = jnp.float32(-jnp.inf)
    logits = jnp.where(lane < N_EXPERTS, logits, neg)
    m1 = jnp.max(logits, axis=-1, keepdims=True)
    i1 = jnp.min(jnp.where(logits == m1, lane, LANES), axis=-1, keepdims=True)
    rest = jnp.where(lane == i1, neg, logits)
    m2 = jnp.max(rest, axis=-1, keepdims=True)
    i2 = jnp.min(jnp.where(rest == m2, lane, LANES), axis=-1, keepdims=True)
    e2 = jnp.exp(m2 - m1)
    g1 = 1.0 / (1.0 + e2)
    g2 = e2 * g1
    route = jnp.where(lane == 0, i1.astype(F32), jnp.where(lane == 1, i2.astype(F32),
                      jnp.where(lane == 2, g1, jnp.where(lane == 3, g2, 0.0))))
    route_ref[...] = route


def _router(x2, norm_w, w_router, tm):
    t, d = x2.shape
    n_e = w_router.shape[1]
    assert n_e == N_EXPERTS and t % tm == 0 and d % LANES == 0
    n_sub = d // LANES
    wr = jnp.pad(w_router.astype(F32), ((0, 0), (0, LANES - n_e)))
    return pl.pallas_call(
        _router_kernel,
        out_shape=(jax.ShapeDtypeStruct((t * n_sub, LANES), F32), jax.ShapeDtypeStruct((t, LANES), F32)),
        grid=(t // tm,),
        in_specs=[
            pl.BlockSpec((tm, d), lambda i: (i, 0)),
            pl.BlockSpec((1, d), lambda i: (0, 0)),
            pl.BlockSpec((d, LANES), lambda i: (0, 0)),
        ],
        out_specs=(pl.BlockSpec((tm * n_sub, LANES), lambda i: (i, 0)),
                   pl.BlockSpec((tm, LANES), lambda i: (i, 0))),
        compiler_params=pltpu.CompilerParams(
            dimension_semantics=("arbitrary",), vmem_limit_bytes=VMEM_LIMIT_BYTES),
        name="router",
    )(x2, norm_w.reshape(1, d).astype(F32), wr)


def _row_copy(n_sub, src_ref, src_row, dst_ref, dst_row, sem):
    return pltpu.make_async_copy(src_ref.at[pl.ds(pl.multiple_of(src_row, n_sub), n_sub)],
                                 dst_ref.at[pl.ds(pl.multiple_of(dst_row, n_sub), n_sub)], sem)


def _tile_indices(dest, tile):
    k, t = dest.shape
    return dest.reshape(k, t // tile, tile).transpose(1, 0, 2).reshape(t // tile, 1, k * tile)


def _scatter_kernel(ts, n_sub, dest_ref, pad_ref, hn_ref, buf_ref, zero_tile, sem, pad_sem):
    @pl.when(pl.program_id(0) == 0)
    def _():
        zero_tile[...] = jnp.zeros(zero_tile.shape, F32)
        n_ranges = pad_ref.shape[0] // 2
        for e in range(n_ranges):
            first, count = pad_ref[e], pad_ref[n_ranges + e]

            def fill(p, carry):
                _row_copy(n_sub, zero_tile, 0, buf_ref, (first + p) * n_sub, pad_sem).start()
                return carry

            def fill_done(p, carry):
                _row_copy(n_sub, zero_tile, 0, buf_ref, 0, pad_sem).wait()
                return carry

            lax.fori_loop(0, count, fill, 0)
            lax.fori_loop(0, count, fill_done, 0)

    def issue(rb, carry):
        for u in range(DMA_UNROLL):
            r = rb * DMA_UNROLL + u
            for k in range(TOP_K):
                _row_copy(n_sub, hn_ref, r * n_sub, buf_ref, dest_ref[0, 0, k * ts + r], sem).start(priority=k % 2)
        return carry

    lax.fori_loop(0, ts // DMA_UNROLL, issue, 0)
    for k in range(TOP_K):
        pltpu.make_async_copy(hn_ref, buf_ref.at[pl.ds(0, ts * n_sub)], sem).wait()


def _scatter_rows(hn, dest, pad_slots, cap, ts, n_sub):
    t = hn.shape[0] // n_sub
    assert t % ts == 0 and ts % DMA_UNROLL == 0
    return pl.pallas_call(
        functools.partial(_scatter_kernel, ts, n_sub),
        out_shape=jax.ShapeDtypeStruct((cap * n_sub, LANES), F32),
        grid=(t // ts,),
        in_specs=[
            pl.BlockSpec((1, 1, TOP_K * ts), lambda i: (i, 0, 0), memory_space=pltpu.SMEM),
            pl.BlockSpec(memory_space=pltpu.SMEM),
            pl.BlockSpec((ts * n_sub, LANES), lambda i: (i, 0)),
        ],
        out_specs=pl.BlockSpec(memory_space=pl.ANY),
        scratch_shapes=[pltpu.VMEM((n_sub, LANES), F32), pltpu.SemaphoreType.DMA(()), pltpu.SemaphoreType.DMA(())],
        compiler_params=pltpu.CompilerParams(
            dimension_semantics=("arbitrary",), vmem_limit_bytes=VMEM_LIMIT_BYTES),
        name="scatter_rows",
    )(_tile_indices(dest, ts), pad_slots, hn)


def _combine_kernel(tc, final_norm, dest_ref, dest_next_ref, x_ref, route_ref, fw_ref, ybuf_ref, out_ref, rows, sems):
    n_sub = x_ref.shape[1] // LANES
    i = pl.program_id(0)
    slot = i % 2

    def gather(idx_ref, s):
        def issue(rb, carry):
            for u in range(DMA_UNROLL):
                r = rb * DMA_UNROLL + u
                for k in range(TOP_K):
                    _row_copy(n_sub, ybuf_ref, idx_ref[0, 0, k * tc + r], rows.at[s, k], r * n_sub,
                              sems.at[s]).start(priority=k % 2)
            return carry

        lax.fori_loop(0, tc // DMA_UNROLL, issue, 0)

    @pl.when(i == 0)
    def _():
        gather(dest_ref, 0)

    @pl.when(i + 1 < pl.num_programs(0))
    def _():
        gather(dest_next_ref, 1 - slot)

    for k in range(TOP_K):
        pltpu.make_async_copy(ybuf_ref.at[pl.ds(0, tc * n_sub)], rows.at[slot, k], sems.at[slot]).wait()

    route = route_ref[...]
    y = x_ref[...]
    for k in range(TOP_K):
        y = y + route[:, TOP_K + k:TOP_K + k + 1] * _tiles_to_rows(rows.at[slot, k], n_sub)
    out_ref[...] = _rmsnorm(y, fw_ref[...]) if final_norm else y


def _combine(x2, route, dest, ybuf, final_w, tc):
    t, d = x2.shape
    assert t % tc == 0 and tc % DMA_UNROLL == 0
    n_tiles = t // tc
    final_norm = final_w is not None
    fw = (final_w if final_norm else jnp.ones((d,), F32)).reshape(1, d).astype(F32)
    idx = _tile_indices(dest, tc)
    return pl.pallas_call(
        functools.partial(_combine_kernel, tc, final_norm),
        out_shape=jax.ShapeDtypeStruct((t, d), F32),
        grid=(n_tiles,),
        in_specs=[
            pl.BlockSpec((1, 1, TOP_K * tc), lambda i: (i, 0, 0), memory_space=pltpu.SMEM),
            pl.BlockSpec((1, 1, TOP_K * tc), lambda i: (jnp.minimum(i + 1, n_tiles - 1), 0, 0), memory_space=pltpu.SMEM),
            pl.BlockSpec((tc, d), lambda i: (i, 0)),
            pl.BlockSpec((tc, LANES), lambda i: (i, 0)),
            pl.BlockSpec((1, d), lambda i: (0, 0)),
            pl.BlockSpec(memory_space=pl.ANY),
        ],
        out_specs=pl.BlockSpec((tc, d), lambda i: (i, 0)),
        scratch_shapes=[pltpu.VMEM((2, TOP_K, tc * (d // LANES), LANES), F32), pltpu.SemaphoreType.DMA((2,))],
        compiler_params=pltpu.CompilerParams(
            dimension_semantics=("arbitrary",), vmem_limit_bytes=VMEM_LIMIT_BYTES),
        name="combine",
    )(idx, idx, x2, route, fw, ybuf)


def _moe_layer(x2, norm_w, w_router, w_gate, w_up, w_down, final_w, tiles):
    t, d = x2.shape
    bm = tiles["moe_bm"]
    hn, route = _router(x2, norm_w, w_router, tiles["router_tm"])
    e_idx = route[:, :TOP_K].astype(jnp.int32).T
    onehot = (e_idx.reshape(-1)[None, :] == jnp.arange(N_EXPERTS, dtype=jnp.int32)[:, None]).astype(jnp.int32)
    csum = jnp.cumsum(onehot, axis=1)
    rank = jnp.sum(csum * onehot, axis=0) - 1
    sizes = csum[:, -1]
    padded = ((sizes + bm - 1) // bm) * bm
    ends = jnp.cumsum(padded)
    starts = ends - padded
    dest = (jnp.sum(starts[:, None] * onehot, axis=0) + rank).astype(jnp.int32).reshape(TOP_K, t)
    cap = TOP_K * t + N_EXPERTS * bm
    n_blocks = cap // bm
    blk_e = jnp.minimum(
        jnp.searchsorted(ends, jnp.arange(n_blocks, dtype=jnp.int32) * bm, side="right"), N_EXPERTS - 1
    ).astype(jnp.int32)
    nact = (ends[-1:] // bm).astype(jnp.int32)

    n_sub = d // LANES
    dest_rows = dest * n_sub
    pad_slots = jnp.concatenate([starts + sizes, ends[-1:], padded - sizes, cap - ends[-1:]]).astype(jnp.int32)
    buf = _scatter_rows(hn, dest_rows, pad_slots, cap, tiles["scatter_ts"], n_sub)
    ybuf = _expert_ffn(buf, blk_e, nact, w_gate, w_up, w_down, bm, tiles["moe_tf"])
    return _combine(x2, route, dest_rows, ybuf, final_w, tiles["combine_tc"])


TILES = {
    "mixer_tl": 256,
    "ffn_tm": 512,
    "router_tm": 512,
    "moe_bm": 512, "moe_tf": 1792,
    "scatter_ts": 1024,
    "combine_tc": 256,
}


def kernel(x, norm_mix, w_in, conv_a_w, conv_ssd_w, conv_ssd_b, dt_bias, a_log, d_skip, ssd_norm_w, conv_conf_w, conv_conf_b, conf_ln_g, conf_ln_b, w_out, norm_ffn, ffn_w_gate, ffn_w_up, ffn_w_down, moe_router, moe_w_gate, moe_w_up, moe_w_down, norm_final):
    bsz, seq, d = x.shape
    depth = norm_mix.shape[0]
    tiles = dict(TILES)
    tiles["mixer_tl"] = min(tiles["mixer_tl"], seq)
    expert_w = None
    for i in range(depth):
        x = _mixer_layer(x, norm_mix[i], w_in[i], conv_a_w[i], conv_ssd_w[i], conv_ssd_b[i], dt_bias[i], a_log[i],
                         d_skip[i], ssd_norm_w[i], conv_conf_w[i], conv_conf_b[i], conf_ln_g[i], conf_ln_b[i],
                         w_out[i], tiles["mixer_tl"])
        x2 = x.reshape(bsz * seq, d)
        final_w = norm_final if i == depth - 1 else None
        j = i // 2
        if i % 2 == 0:
            later = (moe_w_gate[j], moe_w_up[j], moe_w_down[j]) if i + 1 < depth else ()
            x2, expert_w = _dense_ffn(x2, norm_ffn[i], ffn_w_gate[j], ffn_w_up[j], ffn_w_down[j], final_w,
                                      tiles["ffn_tm"], later)
        else:
            if not expert_w:
                expert_w = [w.astype(BF16) for w in (moe_w_gate[j], moe_w_up[j], moe_w_down[j])]
            x2 = _moe_layer(x2, norm_ffn[i], moe_router[j], *expert_w, final_w, tiles)
            expert_w = None
        x = x2.reshape(bsz, seq, d)
    return x
```

```python
import functools

import jax
import jax.numpy as jnp
from jax import lax
from jax.experimental import pallas as pl
from jax.experimental.pallas import tpu as pltpu

F32 = jnp.float32
BF16 = jnp.bfloat16
EPS = 1e-5

LANES = 128
SUBLANES = 8
VMEM_LIMIT_BYTES = 56 * 1024 * 1024

GROUP_DIM = 64
A_CONV = 3
SSD_HEAD_DIM = 64
SSD_GROUPS = 2
SSD_STATE = 128
SSD_CONV = 4
CONF_KERNEL = 31
N_EXPERTS = 8
TOP_K = 2

SSD_CHUNK = 128
CONF_HALO = 32
SMALL_HALO = SUBLANES
CONV_ROWS = 32
ROW_BLOCK = 64
LANE_BLOCK = 256
DMA_UNROLL = 16


def _split3(v):
    hi = v.astype(BF16)
    r1 = v - hi.astype(F32)
    mid = r1.astype(BF16)
    lo = (r1 - mid.astype(F32)).astype(BF16)
    return hi, mid, lo


def _dot(a, b):
    return jnp.dot(a, b, preferred_element_type=F32)


def _dot_nt(a, b):
    return lax.dot_general(a, b, (((1,), (1,)), ((), ())), preferred_element_type=F32)


def _dot_sel_right(v, sel):
    hi, mid, lo = _split3(v)
    return _dot(hi, sel) + _dot(mid, sel) + _dot(lo, sel)


def _dot_sel_left(sel, v):
    hi, mid, lo = _split3(v)
    return _dot(sel, hi) + _dot(sel, mid) + _dot(sel, lo)


def _softplus(v):
    return jnp.maximum(v, 0.0) + jnp.log1p(jnp.exp(-jnp.abs(v)))


def _silu(v):
    return v * jax.nn.sigmoid(v)


def _rmsnorm(x, w):
    return x * lax.rsqrt(jnp.mean(x * x, axis=-1, keepdims=True) + EPS) * w


def _mixer_kernel(dims, x_ref, nw_ref, win_ref, wdtT_ref, caw_ref, csw_ref, csb_ref, dtbc_ref, dtbr_ref,
                  alc_ref, alr_ref, dsk_ref, snw_ref, ccw_ref, ccb_ref, lng_ref, lnb_ref, wout_ref,
                  out_ref,
                  hn_s, p_a, p_c, p_s, p_z, p_dt, dtr, buf_a, buf_c, shc, xact, ysd, ycat, hstate):
    tl, w_a, w_ssd, w_conf, n_heads, n_l = dims
    q = SSD_CHUNK
    n_bc = SSD_GROUPS * SSD_STATE
    xbc = w_ssd + 2 * n_bc
    hpg = n_heads // SSD_GROUPS
    gw = hpg * SSD_HEAD_DIM
    hb = SMALL_HALO
    o_z = 3 * w_a
    o_xbc = o_z + w_ssd
    o_c = o_xbc + xbc
    o_dt = o_c + 2 * w_conf
    lt = lax.rem(pl.program_id(0), n_l)

    @pl.when(lt == 0)
    def _():
        buf_a[0:hb, :] = jnp.zeros((hb, w_a), F32)
        p_s[0:hb, :] = jnp.zeros((hb, xbc), F32)
        buf_c[0:CONF_HALO, :] = jnp.zeros((CONF_HALO, w_conf), F32)
        hstate[...] = jnp.zeros(hstate.shape, F32)

    @pl.when(lt > 0)
    def _():
        buf_a[0:hb, :] = buf_a[tl:tl + hb, :]
        p_s[0:hb, :] = p_s[tl:tl + hb, :]
        buf_c[0:CONF_HALO, :] = buf_c[tl:tl + CONF_HALO, :]

    hn_s[...] = _rmsnorm(x_ref[...], nw_ref[...]).astype(BF16)

    p_a[...] = _dot(hn_s[...], win_ref[:, 0:3 * w_a])
    for rb in range(tl // ROW_BLOCK):
        r0 = rb * ROW_BLOCK
        b0 = hb + r0
        buf_a[b0:b0 + ROW_BLOCK, :] = p_a[r0:r0 + ROW_BLOCK, w_a:2 * w_a] * p_a[r0:r0 + ROW_BLOCK, 2 * w_a:3 * w_a]
        ya = jnp.zeros((ROW_BLOCK, w_a), F32)
        for j in range(A_CONV):
            s = b0 - (A_CONV - 1) + j
            ya = ya + caw_ref[j:j + 1, :] * buf_a[s:s + ROW_BLOCK, :]
        ycat[r0:r0 + ROW_BLOCK, 0:w_a] = (p_a[r0:r0 + ROW_BLOCK, 0:w_a] * ya).astype(BF16)

    p_c[...] = _dot(hn_s[...], win_ref[:, o_c:o_c + 2 * w_conf])
    for rb in range(tl // ROW_BLOCK):
        r0 = rb * ROW_BLOCK
        buf_c[CONF_HALO + r0:CONF_HALO + r0 + ROW_BLOCK, :] = (
            p_c[r0:r0 + ROW_BLOCK, 0:w_conf] * jax.nn.sigmoid(p_c[r0:r0 + ROW_BLOCK, w_conf:2 * w_conf]))
    n_sh = shc.shape[1]
    for r in range(1, SUBLANES):
        shc[r - 1] = buf_c[r:r + n_sh, :]
    for rb in range(tl // CONV_ROWS):
        acc = jnp.zeros((CONV_ROWS, w_conf), F32) + ccb_ref[...]
        for j in range(CONF_KERNEL):
            a, r = divmod(CONF_HALO - (CONF_KERNEL - 1) + j, SUBLANES)
            base = a * SUBLANES + rb * CONV_ROWS
            src = buf_c[base:base + CONV_ROWS, :] if r == 0 else shc[r - 1, base:base + CONV_ROWS, :]
            acc = acc + ccw_ref[j:j + 1, :] * src
        mu = jnp.mean(acc, axis=-1, keepdims=True)
        cen = acc - mu
        var = jnp.mean(cen * cen, axis=-1, keepdims=True)
        yc = cen * lax.rsqrt(var + EPS) * lng_ref[...] + lnb_ref[...]
        ycat[rb * CONV_ROWS:(rb + 1) * CONV_ROWS, w_a + w_ssd:w_a + w_ssd + w_conf] = _silu(yc).astype(BF16)

    p_s[hb:hb + tl, :] = _dot(hn_s[...], win_ref[:, o_xbc:o_xbc + xbc])
    for rb in range(tl // ROW_BLOCK):
        for lb in range(xbc // LANE_BLOCK):
            cols = slice(lb * LANE_BLOCK, (lb + 1) * LANE_BLOCK)
            xc = jnp.zeros((ROW_BLOCK, LANE_BLOCK), F32) + csb_ref[:, cols]
            for j in range(SSD_CONV):
                s = hb + rb * ROW_BLOCK - (SSD_CONV - 1) + j
                xc = xc + csw_ref[j:j + 1, cols] * p_s[s:s + ROW_BLOCK, cols]
            xact[rb * ROW_BLOCK:(rb + 1) * ROW_BLOCK, cols] = _silu(xc)
    p_z[...] = _dot(hn_s[...], win_ref[:, o_z:o_z + w_ssd])
    p_dt[...] = _dot(hn_s[...], win_ref[:, o_dt:o_dt + LANES])
    dtr[...] = _dot_nt(wdtT_ref[...], hn_s[...])

    a_col = -jnp.exp(alc_ref[...])
    a_row = -jnp.exp(alr_ref[...])
    ri = lax.broadcasted_iota(jnp.int32, (q, q), 0)
    ci = lax.broadcasted_iota(jnp.int32, (q, q), 1)
    causal = ri >= ci
    tril = jnp.where(causal, 1.0, 0.0).astype(BF16)
    triu = jnp.where(ri <= ci, 1.0, 0.0).astype(BF16)
    hd = SSD_HEAD_DIM

    for c in range(tl // q):
        rows = slice(c * q, (c + 1) * q)
        dt_c = _softplus(p_dt[rows, :] + dtbc_ref[...])
        dt_r = _softplus(dtr[:, rows] + dtbr_ref[...])
        acum_c = _dot_sel_left(tril, dt_c * a_col)
        acum_r = _dot_sel_right(dt_r * a_row, triu)
        to_end_r = dt_r * jnp.exp(acum_r[:, q - 1:q] - acum_r)
        for g in range(SSD_GROUPS):
            bg = xact[rows, w_ssd + g * SSD_STATE:w_ssd + (g + 1) * SSD_STATE]
            cg = xact[rows, w_ssd + n_bc + g * SSD_STATE:w_ssd + n_bc + (g + 1) * SSD_STATE].astype(BF16)
            cb = _dot_nt(cg, bg.astype(BF16))
            bgt = bg.T
            hg = hstate[g]
            yoff = _dot(cg, hg.astype(BF16))
            for r in range(hpg):
                h = g * hpg + r
                xh = xact[rows, h * hd:(h + 1) * hd].astype(BF16)
                acol = jnp.broadcast_to(acum_c[:, h:h + 1], (q, q))
                dec = jnp.where(causal, jnp.exp(acol - acum_r[h:h + 1, :]), 0.0)
                yd = _dot((cb * dec * dt_r[h:h + 1, :]).astype(BF16), xh)
                ysd[rows, h * hd:(h + 1) * hd] = yd + yoff[:, r * hd:(r + 1) * hd] * jnp.exp(acol[:, 0:hd])
                st = _dot((bgt * to_end_r[h:h + 1, :]).astype(BF16), xh)
                hstate[g, :, r * hd:(r + 1) * hd] = jnp.exp(acol[q - 1:q, 0:hd]) * hg[:, r * hd:(r + 1) * hd] + st
        xs = xact[rows, 0:w_ssd]
        yz = (ysd[rows, :] + dsk_ref[...] * xs) * _silu(p_z[rows, :])
        for g in range(SSD_GROUPS):
            yg = yz[:, g * gw:(g + 1) * gw]
            yn = yg * lax.rsqrt(jnp.mean(yg * yg, axis=-1, keepdims=True) + EPS) * snw_ref[:, g * gw:(g + 1) * gw]
            ycat[rows, w_a + g * gw:w_a + (g + 1) * gw] = yn.astype(BF16)

    out_ref[...] = x_ref[...] + _dot(ycat[...], wout_ref[...])


def _mixer_layer(x, norm_w, w_in, conv_a_w, conv_ssd_w, conv_ssd_b, dt_bias, a_log, d_skip, ssd_norm_w,
                 conv_conf_w, conv_conf_b, conf_ln_g, conf_ln_b, w_out, tl):
    bsz, seq, d = x.shape
    w_a = conv_a_w.shape[1]
    w_ssd = ssd_norm_w.shape[0]
    w_conf = conv_conf_w.shape[1]
    n_heads = dt_bias.shape[0]
    n_bc = SSD_GROUPS * SSD_STATE
    xbc = w_ssd + 2 * n_bc
    assert n_heads == SUBLANES and n_heads * SSD_HEAD_DIM == w_ssd and seq % tl == 0 and tl % SSD_CHUNK == 0
    assert tl % ROW_BLOCK == 0 and xbc % LANE_BLOCK == 0 and conv_a_w.shape[0] == A_CONV and conv_ssd_w.shape[0] == SSD_CONV
    assert conv_conf_w.shape[0] == CONF_KERNEL
    o_dt = 3 * w_a + w_ssd + xbc
    w_dt = w_in[:, o_dt:o_dt + n_heads]
    w_main = jnp.concatenate(
        [w_in[:, :o_dt], w_in[:, o_dt + n_heads:], jnp.pad(w_dt, ((0, 0), (0, LANES - n_heads)))], axis=1).astype(BF16)
    n_cols = w_main.shape[1]
    w_dt_t = w_dt.T.astype(BF16)
    row = lambda v: v.reshape(1, -1).astype(F32)
    pad_lanes = lambda v: jnp.pad(v.reshape(1, -1).astype(F32), ((0, 0), (0, LANES - v.shape[0])))
    col = lambda v: v.reshape(-1, 1).astype(F32)
    d_skip_x = jnp.repeat(d_skip.astype(F32), SSD_HEAD_DIM).reshape(1, w_ssd)

    const = lambda shape: pl.BlockSpec(shape, lambda s: (0,) * len(shape))
    n_l = seq // tl
    n_tiles = bsz * n_l
    dims = (tl, w_a, w_ssd, w_conf, n_heads, n_l)
    x2 = x.reshape(bsz * seq, d)
    out = pl.pallas_call(
        functools.partial(_mixer_kernel, dims),
        out_shape=jax.ShapeDtypeStruct((bsz * seq, d), F32),
        grid=(n_tiles,),
        in_specs=[
            pl.BlockSpec((tl, d), lambda s: (s, 0)),
            const((1, d)), const((d, n_cols)), const((n_heads, d)),
            const((A_CONV, w_a)), const((SSD_CONV, xbc)), const((1, xbc)),
            const((1, LANES)), const((n_heads, 1)), const((1, LANES)), const((n_heads, 1)),
            const((1, w_ssd)), const((1, w_ssd)),
            const((CONF_KERNEL, w_conf)), const((1, w_conf)), const((1, w_conf)), const((1, w_conf)),
            const((d, d)),
        ],
        out_specs=pl.BlockSpec((tl, d), lambda s: (s, 0)),
        scratch_shapes=[
            pltpu.VMEM((tl, d), BF16),
            pltpu.VMEM((tl, 3 * w_a), F32),
            pltpu.VMEM((tl, 2 * w_conf), F32),
            pltpu.VMEM((SMALL_HALO + tl, xbc), F32),
            pltpu.VMEM((tl, w_ssd), F32),
            pltpu.VMEM((tl, LANES), F32),
            pltpu.VMEM((n_heads, tl), F32),
            pltpu.VMEM((SMALL_HALO + tl, w_a), F32),
            pltpu.VMEM((CONF_HALO + tl, w_conf), F32),
            pltpu.VMEM((SUBLANES - 1, tl + CONF_HALO - SUBLANES, w_conf), F32),
            pltpu.VMEM((tl, xbc), F32),
            pltpu.VMEM((tl, w_ssd), F32),
            pltpu.VMEM((tl, d), BF16),
            pltpu.VMEM((SSD_GROUPS, SSD_STATE, w_ssd // SSD_GROUPS), F32),
        ],
        compiler_params=pltpu.CompilerParams(
            dimension_semantics=("arbitrary",), vmem_limit_bytes=VMEM_LIMIT_BYTES),
        name="mixer",
    )(x2, row(norm_w), w_main, w_dt_t, conv_a_w.astype(F32), conv_ssd_w.astype(F32), row(conv_ssd_b),
      pad_lanes(dt_bias), col(dt_bias), pad_lanes(a_log), col(a_log), d_skip_x, row(ssd_norm_w),
      conv_conf_w.astype(F32), row(conv_conf_b), row(conf_ln_g), row(conf_ln_b), w_out.astype(BF16))
    return out.reshape(bsz, seq, d)


def _dense_ffn_kernel(final_norm, n_cast, x_ref, nw_ref, wg_ref, wu_ref, wd_ref, fw_ref, *rest):
    cast_in, out_ref, cast_out, hn = rest[:n_cast], rest[n_cast], rest[n_cast + 1:2 * n_cast + 1], rest[-1]
    hn[...] = _rmsnorm(x_ref[...], nw_ref[...]).astype(BF16)
    a = (_silu(_dot(hn[...], wg_ref[...])) * _dot(hn[...], wu_ref[...])).astype(BF16)
    y = x_ref[...] + _dot(a, wd_ref[...])
    out_ref[...] = _rmsnorm(y, fw_ref[...]) if final_norm else y
    for src, dst in zip(cast_in, cast_out):
        dst[...] = src[...].astype(BF16)


def _dense_ffn(x2, norm_w, w_gate, w_up, w_down, final_w, tm, later=()):
    t, d = x2.shape
    ff = w_gate.shape[1]
    assert t % tm == 0
    n_steps = t // tm
    final_norm = final_w is not None
    fw = (final_w if final_norm else jnp.ones((d,), F32)).reshape(1, d).astype(F32)
    resident = lambda shape: pl.BlockSpec(shape, lambda i: (0, 0), pipeline_mode=pl.Buffered(1))
    bf16_rows = 2 * SUBLANES
    slabs = []
    for w in later:
        rows = w.size // w.shape[-1]
        assert rows % (n_steps * bf16_rows) == 0
        slabs.append((rows // n_steps, w.shape[-1]))
    slab_specs = [pl.BlockSpec(s, lambda i: (i, 0)) for s in slabs]
    res = pl.pallas_call(
        functools.partial(_dense_ffn_kernel, final_norm, len(later)),
        out_shape=[jax.ShapeDtypeStruct((t, d), F32)]
        + [jax.ShapeDtypeStruct((n_steps * r, c), BF16) for r, c in slabs],
        grid=(n_steps,),
        in_specs=[
            pl.BlockSpec((tm, d), lambda i: (i, 0)),
            resident((1, d)), resident((d, ff)), resident((d, ff)), resident((ff, d)), resident((1, d)),
        ] + slab_specs,
        out_specs=[pl.BlockSpec((tm, d), lambda i: (i, 0))] + slab_specs,
        scratch_shapes=[pltpu.VMEM((tm, d), BF16)],
        compiler_params=pltpu.CompilerParams(
            dimension_semantics=("arbitrary",), vmem_limit_bytes=VMEM_LIMIT_BYTES),
        name="dense_ffn",
    )(x2, norm_w.reshape(1, d).astype(F32), w_gate.astype(BF16), w_up.astype(BF16), w_down.astype(BF16), fw,
      *[w.reshape(n_steps * r, c) for w, (r, c) in zip(later, slabs)])
    return res[0], [b.reshape(w.shape) for b, w in zip(res[1:], later)]


def _rows_to_tiles(tile_ref, val):
    n, d = val.shape
    n_sub = d // LANES
    for c in range(n_sub):
        tile_ref[pl.ds(c, n, stride=n_sub), :] = val[:, c * LANES:(c + 1) * LANES]


def _tiles_to_rows(tile_ref, n_sub):
    n = tile_ref.shape[0] // n_sub
    return jnp.concatenate([tile_ref[pl.ds(c, n, stride=n_sub), :] for c in range(n_sub)], axis=1)


def _expert_ffn_kernel(blk_e_ref, nact_ref, xb_ref, wg_ref, wu_ref, wd_ref, out_ref, hb, acc):
    i = pl.program_id(0)
    j = pl.program_id(1)

    @pl.when(i < nact_ref[0])
    def _():
        @pl.when(j == 0)
        def _():
            hb[...] = _tiles_to_rows(xb_ref, hb.shape[1] // LANES).astype(BF16)
            acc[...] = jnp.zeros(acc.shape, F32)

        h = hb[...]
        a = (_silu(_dot(h, wg_ref[...])) * _dot(h, wu_ref[...])).astype(BF16)
        acc[...] += _dot(a, wd_ref[...])

        @pl.when(j == pl.num_programs(1) - 1)
        def _():
            _rows_to_tiles(out_ref, acc[...])

    @pl.when((i >= nact_ref[0]) & (j == 0))
    def _():
        out_ref[...] = jnp.zeros(out_ref.shape, F32)


def _expert_ffn(buf, blk_e, nact, w_gate, w_up, w_down, bm, tf):
    d, ff = w_gate.shape[1], w_gate.shape[2]
    n_sub = d // LANES
    cap = buf.shape[0] // n_sub
    n_blocks = cap // bm
    n_j = ff // tf
    assert cap % bm == 0 and ff % tf == 0
    row_blk = lambda i, j, be, na: (jnp.minimum(i, na[0] - 1), 0)
    snake = lambda i, j: jnp.where(i % 2 == 0, j, n_j - 1 - j)
    col_j = lambda i, j, na: jnp.where(i < na[0], snake(i, j), snake(na[0] - 1, n_j - 1))
    return pl.pallas_call(
        _expert_ffn_kernel,
        out_shape=jax.ShapeDtypeStruct((cap * n_sub, LANES), F32),
        grid_spec=pltpu.PrefetchScalarGridSpec(
            num_scalar_prefetch=2,
            grid=(n_blocks, n_j),
            in_specs=[
                pl.BlockSpec((bm * n_sub, LANES), row_blk),
                pl.BlockSpec((None, d, tf), lambda i, j, be, na: (be[i], 0, col_j(i, j, na))),
                pl.BlockSpec((None, d, tf), lambda i, j, be, na: (be[i], 0, col_j(i, j, na))),
                pl.BlockSpec((None, tf, d), lambda i, j, be, na: (be[i], col_j(i, j, na), 0)),
            ],
            out_specs=pl.BlockSpec((bm * n_sub, LANES), lambda i, j, be, na: (i, 0)),
            scratch_shapes=[pltpu.VMEM((bm, d), BF16), pltpu.VMEM((bm, d), F32)],
        ),
        compiler_params=pltpu.CompilerParams(
            dimension_semantics=("arbitrary", "arbitrary"), vmem_limit_bytes=VMEM_LIMIT_BYTES),
        name="expert_ffn",
    )(blk_e, nact, buf, w_gate, w_up, w_down)


def _router_kernel(x_ref, nw_ref, wr_ref, hn_ref, route_ref):
    h = _rmsnorm(x_ref[...], nw_ref[...])
    _rows_to_tiles(hn_ref, h)
    h_hi, h_lo, _ = _split3(h)
    w_hi, w_lo, _ = _split3(wr_ref[...])
    logits = _dot(h_hi, w_hi) + _dot(h_hi, w_lo) + _dot(h_lo, w_hi)
    lane = lax.broadcasted_iota(jnp.int32, logits.shape, 1)
    neg = jnp.float32(-jnp.inf)
    logits = jnp.where(lane < N_EXPERTS, logits, neg)
    m1 = jnp.max(logits, axis=-1, keepdims=True)
    i1 = jnp.min(jnp.where(logits == m1, lane, LANES), axis=-1, keepdims=True)
    rest = jnp.where(lane == i1, neg, logits)
    m2 = jnp.max(rest, axis=-1, keepdims=True)
    i2 = jnp.min(jnp.where(rest == m2, lane, LANES), axis=-1, keepdims=True)
    e2 = jnp.exp(m2 - m1)
    g1 = 1.0 / (1.0 + e2)
    g2 = e2 * g1
    route = jnp.where(lane == 0, i1.astype(F32), jnp.where(lane == 1, i2.astype(F32),
                      jnp.where(lane == 2, g1, jnp.where(lane == 3, g2, 0.0))))
    route_ref[...] = route


def _router(x2, norm_w, w_router, tm):
    t, d = x2.shape
    n_e = w_router.shape[1]
    assert n_e == N_EXPERTS and t % tm == 0 and d % LANES == 0
    n_sub = d // LANES
    wr = jnp.pad(w_router.astype(F32), ((0, 0), (0, LANES - n_e)))
    return pl.pallas_call(
        _router_kernel,
        out_shape=(jax.ShapeDtypeStruct((t * n_sub, LANES), F32), jax.ShapeDtypeStruct((t, LANES), F32)),
        grid=(t // tm,),
        in_specs=[
            pl.BlockSpec((tm, d), lambda i: (i, 0)),
            pl.BlockSpec((1, d), lambda i: (0, 0)),
            pl.BlockSpec((d, LANES), lambda i: (0, 0)),
        ],
        out_specs=(pl.BlockSpec((tm * n_sub, LANES), lambda i: (i, 0)),
                   pl.BlockSpec((tm, LANES), lambda i: (i, 0))),
        compiler_params=pltpu.CompilerParams(
            dimension_semantics=("arbitrary",), vmem_limit_bytes=VMEM_LIMIT_BYTES),
        name="router",
    )(x2, norm_w.reshape(1, d).astype(F32), wr)


def _row_copy(n_sub, src_ref, src_row, dst_ref, dst_row, sem):
    return pltpu.make_async_copy(src_ref.at[pl.ds(pl.multiple_of(src_row, n_sub), n_sub)],
                                 dst_ref.at[pl.ds(pl.multiple_of(dst_row, n_sub), n_sub)], sem)


def _tile_indices(dest, tile):
    k, t = dest.shape
    return dest.reshape(k, t // tile, tile).transpose(1, 0, 2).reshape(t // tile, 1, k * tile)


def _scatter_kernel(ts, n_sub, dest_ref, pad_ref, hn_ref, buf_ref, zero_tile, sem, pad_sem):
    @pl.when(pl.program_id(0) == 0)
    def _():
        zero_tile[...] = jnp.zeros(zero_tile.shape, F32)
        n_ranges = pad_ref.shape[0] // 2
        for e in range(n_ranges):
            first, count = pad_ref[e], pad_ref[n_ranges + e]

            def fill(p, carry):
                _row_copy(n_sub, zero_tile, 0, buf_ref, (first + p) * n_sub, pad_sem).start()
                return carry

            def fill_done(p, carry):
                _row_copy(n_sub, zero_tile, 0, buf_ref, 0, pad_sem).wait()
                return carry

            lax.fori_loop(0, count, fill, 0)
            lax.fori_loop(0, count, fill_done, 0)

    def issue(rb, carry):
        for u in range(DMA_UNROLL):
            r = rb * DMA_UNROLL + u
            for k in range(TOP_K):
                _row_copy(n_sub, hn_ref, r * n_sub, buf_ref, dest_ref[0, 0, k * ts + r], sem).start(priority=k % 2)
        return carry

    lax.fori_loop(0, ts // DMA_UNROLL, issue, 0)
    for k in range(TOP_K):
        pltpu.make_async_copy(hn_ref, buf_ref.at[pl.ds(0, ts * n_sub)], sem).wait()


def _scatter_rows(hn, dest, pad_slots, cap, ts, n_sub):
    t = hn.shape[0] // n_sub
    assert t % ts == 0 and ts % DMA_UNROLL == 0
    return pl.pallas_call(
        functools.partial(_scatter_kernel, ts, n_sub),
        out_shape=jax.ShapeDtypeStruct((cap * n_sub, LANES), F32),
        grid=(t // ts,),
        in_specs=[
            pl.BlockSpec((1, 1, TOP_K * ts), lambda i: (i, 0, 0), memory_space=pltpu.SMEM),
            pl.BlockSpec(memory_space=pltpu.SMEM),
            pl.BlockSpec((ts * n_sub, LANES), lambda i: (i, 0)),
        ],
        out_specs=pl.BlockSpec(memory_space=pl.ANY),
        scratch_shapes=[pltpu.VMEM((n_sub, LANES), F32), pltpu.SemaphoreType.DMA(()), pltpu.SemaphoreType.DMA(())],
        compiler_params=pltpu.CompilerParams(
            dimension_semantics=("arbitrary",), vmem_limit_bytes=VMEM_LIMIT_BYTES),
        name="scatter_rows",
    )(_tile_indices(dest, ts), pad_slots, hn)


def _combine_kernel(tc, final_norm, dest_ref, dest_next_ref, x_ref, route_ref, fw_ref, ybuf_ref, out_ref, rows, sems):
    n_sub = x_ref.shape[1] // LANES
    i = pl.program_id(0)
    slot = i % 2

    def gather(idx_ref, s):
        def issue(rb, carry):
            for u in range(DMA_UNROLL):
                r = rb * DMA_UNROLL + u
                for k in range(TOP_K):
                    _row_copy(n_sub, ybuf_ref, idx_ref[0, 0, k * tc + r], rows.at[s, k], r * n_sub,
                              sems.at[s]).start(priority=k % 2)
            return carry

        lax.fori_loop(0, tc // DMA_UNROLL, issue, 0)

    @pl.when(i == 0)
    def _():
        gather(dest_ref, 0)

    @pl.when(i + 1 < pl.num_programs(0))
    def _():
        gather(dest_next_ref, 1 - slot)

    for k in range(TOP_K):
        pltpu.make_async_copy(ybuf_ref.at[pl.ds(0, tc * n_sub)], rows.at[slot, k], sems.at[slot]).wait()

    route = route_ref[...]
    y = x_ref[...]
    for k in range(TOP_K):
        y = y + route[:, TOP_K + k:TOP_K + k + 1] * _tiles_to_rows(rows.at[slot, k], n_sub)
    out_ref[...] = _rmsnorm(y, fw_ref[...]) if final_norm else y


def _combine(x2, route, dest, ybuf, final_w, tc):
    t, d = x2.shape
    assert t % tc == 0 and tc % DMA_UNROLL == 0
    n_tiles = t // tc
    final_norm = final_w is not None
    fw = (final_w if final_norm else jnp.ones((d,), F32)).reshape(1, d).astype(F32)
    idx = _tile_indices(dest, tc)
    return pl.pallas_call(
        functools.partial(_combine_kernel, tc, final_norm),
        out_shape=jax.ShapeDtypeStruct((t, d), F32),
        grid=(n_tiles,),
        in_specs=[
            pl.BlockSpec((1, 1, TOP_K * tc), lambda i: (i, 0, 0), memory_space=pltpu.SMEM),
            pl.BlockSpec((1, 1, TOP_K * tc), lambda i: (jnp.minimum(i + 1, n_tiles - 1), 0, 0), memory_space=pltpu.SMEM),
            pl.BlockSpec((tc, d), lambda i: (i, 0)),
            pl.BlockSpec((tc, LANES), lambda i: (i, 0)),
            pl.BlockSpec((1, d), lambda i: (0, 0)),
            pl.BlockSpec(memory_space=pl.ANY),
        ],
        out_specs=pl.BlockSpec((tc, d), lambda i: (i, 0)),
        scratch_shapes=[pltpu.VMEM((2, TOP_K, tc * (d // LANES), LANES), F32), pltpu.SemaphoreType.DMA((2,))],
        compiler_params=pltpu.CompilerParams(
            dimension_semantics=("arbitrary",), vmem_limit_bytes=VMEM_LIMIT_BYTES),
        name="combine",
    )(idx, idx, x2, route, fw, ybuf)


def _moe_layer(x2, norm_w, w_router, w_gate, w_up, w_down, final_w, tiles):
    t, d = x2.shape
    bm = tiles["moe_bm"]
    hn, route = _router(x2, norm_w, w_router, tiles["router_tm"])
    e_idx = route[:, :TOP_K].astype(jnp.int32).T
    onehot = (e_idx.reshape(-1)[None, :] == jnp.arange(N_EXPERTS, dtype=jnp.int32)[:, None]).astype(jnp.int32)
    csum = jnp.cumsum(onehot, axis=1)
    rank = jnp.sum(csum * onehot, axis=0) - 1
    sizes = csum[:, -1]
    padded = ((sizes + bm - 1) // bm) * bm
    ends = jnp.cumsum(padded)
    starts = ends - padded
    dest = (jnp.sum(starts[:, None] * onehot, axis=0) + rank).astype(jnp.int32).reshape(TOP_K, t)
    cap = TOP_K * t + N_EXPERTS * bm
    n_blocks = cap // bm
    blk_e = jnp.minimum(
        jnp.searchsorted(ends, jnp.arange(n_blocks, dtype=jnp.int32) * bm, side="right"), N_EXPERTS - 1
    ).astype(jnp.int32)
    nact = (ends[-1:] // bm).astype(jnp.int32)

    n_sub = d // LANES
    dest_rows = dest * n_sub
    pad_slots = jnp.concatenate([starts + sizes, ends[-1:], padded - sizes, cap - ends[-1:]]).astype(jnp.int32)
    buf = _scatter_rows(hn, dest_rows, pad_slots, cap, tiles["scatter_ts"], n_sub)
    ybuf = _expert_ffn(buf, blk_e, nact, w_gate, w_up, w_down, bm, tiles["moe_tf"])
    return _combine(x2, route, dest_rows, ybuf, final_w, tiles["combine_tc"])


TILES = {
    "mixer_tl": 512,
    "ffn_tm": 512,
    "router_tm": 1024,
    "moe_bm": 512, "moe_tf": 1792,
    "scatter_ts": 1024,
    "combine_tc": 512,
}


def kernel(x, norm_mix, w_in, conv_a_w, conv_ssd_w, conv_ssd_b, dt_bias, a_log, d_skip, ssd_norm_w, conv_conf_w, conv_conf_b, conf_ln_g, conf_ln_b, w_out, norm_ffn, ffn_w_gate, ffn_w_up, ffn_w_down, moe_router, moe_w_gate, moe_w_up, moe_w_down, norm_final):
    bsz, seq, d = x.shape
    depth = norm_mix.shape[0]
    tiles = dict(TILES)
    tiles["mixer_tl"] = min(tiles["mixer_tl"], seq)
    expert_w = None
    for i in range(depth):
        x = _mixer_layer(x, norm_mix[i], w_in[i], conv_a_w[i], conv_ssd_w[i], conv_ssd_b[i], dt_bias[i], a_log[i],
                         d_skip[i], ssd_norm_w[i], conv_conf_w[i], conv_conf_b[i], conf_ln_g[i], conf_ln_b[i],
                         w_out[i], tiles["mixer_tl"])
        x2 = x.reshape(bsz * seq, d)
        final_w = norm_final if i == depth - 1 else None
        j = i // 2
        if i % 2 == 0:
            later = (moe_w_gate[j], moe_w_up[j], moe_w_down[j]) if i + 1 < depth else ()
            x2, expert_w = _dense_ffn(x2, norm_ffn[i], ffn_w_gate[j], ffn_w_up[j], ffn_w_down[j], final_w,
                                      tiles["ffn_tm"], later)
        else:
            if not expert_w:
                expert_w = [w.astype(BF16) for w in (moe_w_gate[j], moe_w_up[j], moe_w_down[j])]
            x2 = _moe_layer(x2, norm_ffn[i], moe_router[j], *expert_w, final_w, tiles)
            expert_w = None
        x = x2.reshape(bsz, seq, d)
    return x
```

```python
import functools

import jax
import jax.numpy as jnp
from jax import lax
from jax.experimental import pallas as pl
from jax.experimental.pallas import tpu as pltpu

F32 = jnp.float32
BF16 = jnp.bfloat16
EPS = 1e-5

LANES = 128
SUBLANES = 8
VMEM_LIMIT_BYTES = 56 * 1024 * 1024

GROUP_DIM = 64
A_CONV = 3
SSD_HEAD_DIM = 64
SSD_GROUPS = 2
SSD_STATE = 128
SSD_CONV = 4
CONF_KERNEL = 31
N_EXPERTS = 8
TOP_K = 2

SSD_CHUNK = 128
CONF_HALO = 32
SMALL_HALO = SUBLANES
CONV_ROWS = 32
ROW_BLOCK = 64
LANE_BLOCK = 256
DMA_UNROLL = 16


def _split3(v):
    hi = v.astype(BF16)
    r1 = v - hi.astype(F32)
    mid = r1.astype(BF16)
    lo = (r1 - mid.astype(F32)).astype(BF16)
    return hi, mid, lo


def _dot(a, b):
    return jnp.dot(a, b, preferred_element_type=F32)


def _dot_nt(a, b):
    return lax.dot_general(a, b, (((1,), (1,)), ((), ())), preferred_element_type=F32)


def _dot_sel_right(v, sel):
    hi, mid, lo = _split3(v)
    return _dot(hi, sel) + _dot(mid, sel) + _dot(lo, sel)


def _dot_sel_left(sel, v):
    hi, mid, lo = _split3(v)
    return _dot(sel, hi) + _dot(sel, mid) + _dot(sel, lo)


def _softplus(v):
    return jnp.maximum(v, 0.0) + jnp.log1p(jnp.exp(-jnp.abs(v)))


def _silu(v):
    return v * jax.nn.sigmoid(v)


def _rmsnorm(x, w):
    return x * lax.rsqrt(jnp.mean(x * x, axis=-1, keepdims=True) + EPS) * w


def _mixer_kernel(dims, x_ref, nw_ref, win_ref, wdtT_ref, caw_ref, csw_ref, csb_ref, dtbc_ref, dtbr_ref,
                  alc_ref, alr_ref, dsk_ref, snw_ref, ccw_ref, ccb_ref, lng_ref, lnb_ref, wout_ref,
                  out_ref,
                  hn_s, p_a, p_c, p_s, p_z, p_dt, dtr, buf_a, buf_c, shc, xact, ysd, ycat, hstate):
    tl, w_a, w_ssd, w_conf, n_heads, n_l = dims
    q = SSD_CHUNK
    n_bc = SSD_GROUPS * SSD_STATE
    xbc = w_ssd + 2 * n_bc
    hpg = n_heads // SSD_GROUPS
    gw = hpg * SSD_HEAD_DIM
    hb = SMALL_HALO
    o_z = 3 * w_a
    o_xbc = o_z + w_ssd
    o_c = o_xbc + xbc
    o_dt = o_c + 2 * w_conf
    lt = lax.rem(pl.program_id(0), n_l)

    @pl.when(lt == 0)
    def _():
        buf_a[0:hb, :] = jnp.zeros((hb, w_a), F32)
        p_s[0:hb, :] = jnp.zeros((hb, xbc), F32)
        buf_c[0:CONF_HALO, :] = jnp.zeros((CONF_HALO, w_conf), F32)
        hstate[...] = jnp.zeros(hstate.shape, F32)

    @pl.when(lt > 0)
    def _():
        buf_a[0:hb, :] = buf_a[tl:tl + hb, :]
        p_s[0:hb, :] = p_s[tl:tl + hb, :]
        buf_c[0:CONF_HALO, :] = buf_c[tl:tl + CONF_HALO, :]

    hn_s[...] = _rmsnorm(x_ref[...], nw_ref[...]).astype(BF16)

    p_a[...] = _dot(hn_s[...], win_ref[:, 0:3 * w_a])
    for rb in range(tl // ROW_BLOCK):
        r0 = rb * ROW_BLOCK
        b0 = hb + r0
        buf_a[b0:b0 + ROW_BLOCK, :] = p_a[r0:r0 + ROW_BLOCK, w_a:2 * w_a] * p_a[r0:r0 + ROW_BLOCK, 2 * w_a:3 * w_a]
        ya = jnp.zeros((ROW_BLOCK, w_a), F32)
        for j in range(A_CONV):
            s = b0 - (A_CONV - 1) + j
            ya = ya + caw_ref[j:j + 1, :] * buf_a[s:s + ROW_BLOCK, :]
        ycat[r0:r0 + ROW_BLOCK, 0:w_a] = (p_a[r0:r0 + ROW_BLOCK, 0:w_a] * ya).astype(BF16)

    p_c[...] = _dot(hn_s[...], win_ref[:, o_c:o_c + 2 * w_conf])
    for rb in range(tl // ROW_BLOCK):
        r0 = rb * ROW_BLOCK
        buf_c[CONF_HALO + r0:CONF_HALO + r0 + ROW_BLOCK, :] = (
            p_c[r0:r0 + ROW_BLOCK, 0:w_conf] * jax.nn.sigmoid(p_c[r0:r0 + ROW_BLOCK, w_conf:2 * w_conf]))
    n_sh = shc.shape[1]
    for r in range(1, SUBLANES):
        shc[r - 1] = buf_c[r:r + n_sh, :]
    for rb in range(tl // CONV_ROWS):
        acc = jnp.zeros((CONV_ROWS, w_conf), F32) + ccb_ref[...]
        for j in range(CONF_KERNEL):
            a, r = divmod(CONF_HALO - (CONF_KERNEL - 1) + j, SUBLANES)
            base = a * SUBLANES + rb * CONV_ROWS
            src = buf_c[base:base + CONV_ROWS, :] if r == 0 else shc[r - 1, base:base + CONV_ROWS, :]
            acc = acc + ccw_ref[j:j + 1, :] * src
        mu = jnp.mean(acc, axis=-1, keepdims=True)
        cen = acc - mu
        var = jnp.mean(cen * cen, axis=-1, keepdims=True)
        yc = cen * lax.rsqrt(var + EPS) * lng_ref[...] + lnb_ref[...]
        ycat[rb * CONV_ROWS:(rb + 1) * CONV_ROWS, w_a + w_ssd:w_a + w_ssd + w_conf] = _silu(yc).astype(BF16)

    p_s[hb:hb + tl, :] = _dot(hn_s[...], win_ref[:, o_xbc:o_xbc + xbc])
    for rb in range(tl // ROW_BLOCK):
        for lb in range(xbc // LANE_BLOCK):
            cols = slice(lb * LANE_BLOCK, (lb + 1) * LANE_BLOCK)
            xc = jnp.zeros((ROW_BLOCK, LANE_BLOCK), F32) + csb_ref[:, cols]
            for j in range(SSD_CONV):
                s = hb + rb * ROW_BLOCK - (SSD_CONV - 1) + j
                xc = xc + csw_ref[j:j + 1, cols] * p_s[s:s + ROW_BLOCK, cols]
            xact[rb * ROW_BLOCK:(rb + 1) * ROW_BLOCK, cols] = _silu(xc)
    p_z[...] = _dot(hn_s[...], win_ref[:, o_z:o_z + w_ssd])
    p_dt[...] = _dot(hn_s[...], win_ref[:, o_dt:o_dt + LANES])
    dtr[...] = _dot_nt(wdtT_ref[...], hn_s[...])

    a_col = -jnp.exp(alc_ref[...])
    a_row = -jnp.exp(alr_ref[...])
    ri = lax.broadcasted_iota(jnp.int32, (q, q), 0)
    ci = lax.broadcasted_iota(jnp.int32, (q, q), 1)
    causal = ri >= ci
    tril = jnp.where(causal, 1.0, 0.0).astype(BF16)
    triu = jnp.where(ri <= ci, 1.0, 0.0).astype(BF16)
    hd = SSD_HEAD_DIM

    for c in range(tl // q):
        rows = slice(c * q, (c + 1) * q)
        dt_c = _softplus(p_dt[rows, :] + dtbc_ref[...])
        dt_r = _softplus(dtr[:, rows] + dtbr_ref[...])
        acum_c = _dot_sel_left(tril, dt_c * a_col)
        acum_r = _dot_sel_right(dt_r * a_row, triu)
        to_end_r = dt_r * jnp.exp(acum_r[:, q - 1:q] - acum_r)
        for g in range(SSD_GROUPS):
            bg = xact[rows, w_ssd + g * SSD_STATE:w_ssd + (g + 1) * SSD_STATE]
            cg = xact[rows, w_ssd + n_bc + g * SSD_STATE:w_ssd + n_bc + (g + 1) * SSD_STATE].astype(BF16)
            cb = _dot_nt(cg, bg.astype(BF16))
            bgt = bg.T
            hg = hstate[g]
            yoff = _dot(cg, hg.astype(BF16))
            for r in range(hpg):
                h = g * hpg + r
                xh = xact[rows, h * hd:(h + 1) * hd].astype(BF16)
                acol = jnp.broadcast_to(acum_c[:, h:h + 1], (q, q))
                dec = jnp.where(causal, jnp.exp(acol - acum_r[h:h + 1, :]), 0.0)
                yd = _dot((cb * dec * dt_r[h:h + 1, :]).astype(BF16), xh)
                ysd[rows, h * hd:(h + 1) * hd] = yd + yoff[:, r * hd:(r + 1) * hd] * jnp.exp(acol[:, 0:hd])
                st = _dot((bgt * to_end_r[h:h + 1, :]).astype(BF16), xh)
                hstate[g, :, r * hd:(r + 1) * hd] = jnp.exp(acol[q - 1:q, 0:hd]) * hg[:, r * hd:(r + 1) * hd] + st
        xs = xact[rows, 0:w_ssd]
        yz = (ysd[rows, :] + dsk_ref[...] * xs) * _silu(p_z[rows, :])
        for g in range(SSD_GROUPS):
            yg = yz[:, g * gw:(g + 1) * gw]
            yn = yg * lax.rsqrt(jnp.mean(yg * yg, axis=-1, keepdims=True) + EPS) * snw_ref[:, g * gw:(g + 1) * gw]
            ycat[rows, w_a + g * gw:w_a + (g + 1) * gw] = yn.astype(BF16)

    out_ref[...] = x_ref[...] + _dot(ycat[...], wout_ref[...])


def _mixer_layer(x, norm_w, w_in, conv_a_w, conv_ssd_w, conv_ssd_b, dt_bias, a_log, d_skip, ssd_norm_w,
                 conv_conf_w, conv_conf_b, conf_ln_g, conf_ln_b, w_out, tl):
    bsz, seq, d = x.shape
    w_a = conv_a_w.shape[1]
    w_ssd = ssd_norm_w.shape[0]
    w_conf = conv_conf_w.shape[1]
    n_heads = dt_bias.shape[0]
    n_bc = SSD_GROUPS * SSD_STATE
    xbc = w_ssd + 2 * n_bc
    assert n_heads == SUBLANES and n_heads * SSD_HEAD_DIM == w_ssd and seq % tl == 0 and tl % SSD_CHUNK == 0
    assert tl % ROW_BLOCK == 0 and xbc % LANE_BLOCK == 0 and conv_a_w.shape[0] == A_CONV and conv_ssd_w.shape[0] == SSD_CONV
    assert conv_conf_w.shape[0] == CONF_KERNEL
    o_dt = 3 * w_a + w_ssd + xbc
    w_dt = w_in[:, o_dt:o_dt + n_heads]
    w_main = jnp.concatenate(
        [w_in[:, :o_dt], w_in[:, o_dt + n_heads:], jnp.pad(w_dt, ((0, 0), (0, LANES - n_heads)))], axis=1).astype(BF16)
    n_cols = w_main.shape[1]
    w_dt_t = w_dt.T.astype(BF16)
    row = lambda v: v.reshape(1, -1).astype(F32)
    pad_lanes = lambda v: jnp.pad(v.reshape(1, -1).astype(F32), ((0, 0), (0, LANES - v.shape[0])))
    col = lambda v: v.reshape(-1, 1).astype(F32)
    d_skip_x = jnp.repeat(d_skip.astype(F32), SSD_HEAD_DIM).reshape(1, w_ssd)

    const = lambda shape: pl.BlockSpec(shape, lambda s: (0,) * len(shape))
    n_l = seq // tl
    n_tiles = bsz * n_l
    dims = (tl, w_a, w_ssd, w_conf, n_heads, n_l)
    x2 = x.reshape(bsz * seq, d)
    out = pl.pallas_call(
        functools.partial(_mixer_kernel, dims),
        out_shape=jax.ShapeDtypeStruct((bsz * seq, d), F32),
        grid=(n_tiles,),
        in_specs=[
            pl.BlockSpec((tl, d), lambda s: (s, 0)),
            const((1, d)), const((d, n_cols)), const((n_heads, d)),
            const((A_CONV, w_a)), const((SSD_CONV, xbc)), const((1, xbc)),
            const((1, LANES)), const((n_heads, 1)), const((1, LANES)), const((n_heads, 1)),
            const((1, w_ssd)), const((1, w_ssd)),
            const((CONF_KERNEL, w_conf)), const((1, w_conf)), const((1, w_conf)), const((1, w_conf)),
            const((d, d)),
        ],
        out_specs=pl.BlockSpec((tl, d), lambda s: (s, 0)),
        scratch_shapes=[
            pltpu.VMEM((tl, d), BF16),
            pltpu.VMEM((tl, 3 * w_a), F32),
            pltpu.VMEM((tl, 2 * w_conf), F32),
            pltpu.VMEM((SMALL_HALO + tl, xbc), F32),
            pltpu.VMEM((tl, w_ssd), F32),
            pltpu.VMEM((tl, LANES), F32),
            pltpu.VMEM((n_heads, tl), F32),
            pltpu.VMEM((SMALL_HALO + tl, w_a), F32),
            pltpu.VMEM((CONF_HALO + tl, w_conf), F32),
            pltpu.VMEM((SUBLANES - 1, tl + CONF_HALO - SUBLANES, w_conf), F32),
            pltpu.VMEM((tl, xbc), F32),
            pltpu.VMEM((tl, w_ssd), F32),
            pltpu.VMEM((tl, d), BF16),
            pltpu.VMEM((SSD_GROUPS, SSD_STATE, w_ssd // SSD_GROUPS), F32),
        ],
        compiler_params=pltpu.CompilerParams(
            dimension_semantics=("arbitrary",), vmem_limit_bytes=VMEM_LIMIT_BYTES),
        name="mixer",
    )(x2, row(norm_w), w_main, w_dt_t, conv_a_w.astype(F32), conv_ssd_w.astype(F32), row(conv_ssd_b),
      pad_lanes(dt_bias), col(dt_bias), pad_lanes(a_log), col(a_log), d_skip_x, row(ssd_norm_w),
      conv_conf_w.astype(F32), row(conv_conf_b), row(conf_ln_g), row(conf_ln_b), w_out.astype(BF16))
    return out.reshape(bsz, seq, d)


def _dense_ffn_kernel(final_norm, n_cast, x_ref, nw_ref, wg_ref, wu_ref, wd_ref, fw_ref, *rest):
    cast_in, out_ref, cast_out, hn = rest[:n_cast], rest[n_cast], rest[n_cast + 1:2 * n_cast + 1], rest[-1]
    hn[...] = _rmsnorm(x_ref[...], nw_ref[...]).astype(BF16)
    a = (_silu(_dot(hn[...], wg_ref[...])) * _dot(hn[...], wu_ref[...])).astype(BF16)
    y = x_ref[...] + _dot(a, wd_ref[...])
    out_ref[...] = _rmsnorm(y, fw_ref[...]) if final_norm else y
    for src, dst in zip(cast_in, cast_out):
        dst[...] = src[...].astype(BF16)


def _dense_ffn(x2, norm_w, w_gate, w_up, w_down, final_w, tm, later=()):
    t, d = x2.shape
    ff = w_gate.shape[1]
    assert t % tm == 0
    n_steps = t // tm
    final_norm = final_w is not None
    fw = (final_w if final_norm else jnp.ones((d,), F32)).reshape(1, d).astype(F32)
    resident = lambda shape: pl.BlockSpec(shape, lambda i: (0, 0), pipeline_mode=pl.Buffered(1))
    bf16_rows = 2 * SUBLANES
    slabs = []
    for w in later:
        rows = w.size // w.shape[-1]
        assert rows % (n_steps * bf16_rows) == 0
        slabs.append((rows // n_steps, w.shape[-1]))
    slab_specs = [pl.BlockSpec(s, lambda i: (i, 0)) for s in slabs]
    res = pl.pallas_call(
        functools.partial(_dense_ffn_kernel, final_norm, len(later)),
        out_shape=[jax.ShapeDtypeStruct((t, d), F32)]
        + [jax.ShapeDtypeStruct((n_steps * r, c), BF16) for r, c in slabs],
        grid=(n_steps,),
        in_specs=[
            pl.BlockSpec((tm, d), lambda i: (i, 0)),
            resident((1, d)), resident((d, ff)), resident((d, ff)), resident((ff, d)), resident((1, d)),
        ] + slab_specs,
        out_specs=[pl.BlockSpec((tm, d), lambda i: (i, 0))] + slab_specs,
        scratch_shapes=[pltpu.VMEM((tm, d), BF16)],
        compiler_params=pltpu.CompilerParams(
            dimension_semantics=("arbitrary",), vmem_limit_bytes=VMEM_LIMIT_BYTES),
        name="dense_ffn",
    )(x2, norm_w.reshape(1, d).astype(F32), w_gate.astype(BF16), w_up.astype(BF16), w_down.astype(BF16), fw,
      *[w.reshape(n_steps * r, c) for w, (r, c) in zip(later, slabs)])
    return res[0], [b.reshape(w.shape) for b, w in zip(res[1:], later)]


def _rows_to_tiles(tile_ref, val):
    n, d = val.shape
    n_sub = d // LANES
    for c in range(n_sub):
        tile_ref[pl.ds(c, n, stride=n_sub), :] = val[:, c * LANES:(c + 1) * LANES]


def _tiles_to_rows(tile_ref, n_sub):
    n = tile_ref.shape[0] // n_sub
    return jnp.concatenate([tile_ref[pl.ds(c, n, stride=n_sub), :] for c in range(n_sub)], axis=1)


def _expert_ffn_kernel(blk_e_ref, nact_ref, xb_ref, wg_ref, wu_ref, wd_ref, out_ref, hb, acc):
    i = pl.program_id(0)
    j = pl.program_id(1)

    n_j = pl.num_programs(1)
    active = i < nact_ref[0]

    def partial_out():
        h = hb[...]
        a = (_silu(_dot(h, wg_ref[...])) * _dot(h, wu_ref[...])).astype(BF16)
        return _dot(a, wd_ref[...])

    @pl.when(active & (j == 0))
    def _():
        hb[...] = _tiles_to_rows(xb_ref, hb.shape[1] // LANES).astype(BF16)
        acc[...] = partial_out()

    @pl.when(active & (j > 0) & (j < n_j - 1))
    def _():
        acc[...] += partial_out()

    @pl.when(active & (j == n_j - 1))
    def _():
        _rows_to_tiles(out_ref, acc[...] + partial_out())

    @pl.when((i >= nact_ref[0]) & (j == 0))
    def _():
        out_ref[...] = jnp.zeros(out_ref.shape, F32)


def _expert_ffn(buf, blk_e, nact, w_gate, w_up, w_down, bm, tf):
    d, ff = w_gate.shape[1], w_gate.shape[2]
    n_sub = d // LANES
    cap = buf.shape[0] // n_sub
    n_blocks = cap // bm
    n_j = ff // tf
    assert cap % bm == 0 and ff % tf == 0 and n_j >= 2 and w_gate.dtype == BF16
    row_blk = lambda i, j, be, na: (jnp.minimum(i, na[0] - 1), 0)
    snake = lambda i, j: jnp.where(i % 2 == 0, j, n_j - 1 - j)
    col_j = lambda i, j, na: jnp.where(i < na[0], snake(i, j), snake(na[0] - 1, n_j - 1))
    return pl.pallas_call(
        _expert_ffn_kernel,
        out_shape=jax.ShapeDtypeStruct((cap * n_sub, LANES), F32),
        grid_spec=pltpu.PrefetchScalarGridSpec(
            num_scalar_prefetch=2,
            grid=(n_blocks, n_j),
            in_specs=[
                pl.BlockSpec((bm * n_sub, LANES), row_blk),
                pl.BlockSpec((None, d, tf), lambda i, j, be, na: (be[i], 0, col_j(i, j, na))),
                pl.BlockSpec((None, d, tf), lambda i, j, be, na: (be[i], 0, col_j(i, j, na))),
                pl.BlockSpec((None, tf, d), lambda i, j, be, na: (be[i], col_j(i, j, na), 0)),
            ],
            out_specs=pl.BlockSpec((bm * n_sub, LANES), lambda i, j, be, na: (i, 0)),
            scratch_shapes=[pltpu.VMEM((bm, d), BF16), pltpu.VMEM((bm, d), F32)],
        ),
        compiler_params=pltpu.CompilerParams(
            dimension_semantics=("arbitrary", "arbitrary"), vmem_limit_bytes=VMEM_LIMIT_BYTES),
        name="expert_ffn",
    )(blk_e, nact, buf, w_gate, w_up, w_down)


def _router_kernel(x_ref, nw_ref, wr_ref, hn_ref, route_ref):
    h = _rmsnorm(x_ref[...], nw_ref[...])
    _rows_to_tiles(hn_ref, h)
    h_hi, h_lo, _ = _split3(h)
    w_hi, w_lo, _ = _split3(wr_ref[...])
    logits = _dot(h_hi, w_hi) + _dot(h_hi, w_lo) + _dot(h_lo, w_hi)
    lane = lax.broadcasted_iota(jnp.int32, logits.shape, 1)
    neg = jnp.float32(-jnp.inf)
    logits = jnp.where(lane < N_EXPERTS, logits, neg)
    m1 = jnp.max(logits, axis=-1, keepdims=True)
    i1 = jnp.min(jnp.where(logits == m1, lane, LANES), axis=-1, keepdims=True)
    rest = jnp.where(lane == i1, neg, logits)
    m2 = jnp.max(rest, axis=-1, keepdims=True)
    i2 = jnp.min(jnp.where(rest == m2, lane, LANES), axis=-1, keepdims=True)
    e2 = jnp.exp(m2 - m1)
    g1 = 1.0 / (1.0 + e2)
    g2 = e2 * g1
    route = jnp.where(lane == 0, i1.astype(F32), jnp.where(lane == 1, i2.astype(F32),
                      jnp.where(lane == 2, g1, jnp.where(lane == 3, g2, 0.0))))
    route_ref[...] = route


def _router(x2, norm_w, w_router, tm):
    t, d = x2.shape
    n_e = w_router.shape[1]
    assert n_e == N_EXPERTS and t % tm == 0 and d % LANES == 0
    n_sub = d // LANES
    wr = jnp.pad(w_router.astype(F32), ((0, 0), (0, LANES - n_e)))
    return pl.pallas_call(
        _router_kernel,
        out_shape=(jax.ShapeDtypeStruct((t * n_sub, LANES), F32), jax.ShapeDtypeStruct((t, LANES), F32)),
        grid=(t // tm,),
        in_specs=[
            pl.BlockSpec((tm, d), lambda i: (i, 0)),
            pl.BlockSpec((1, d), lambda i: (0, 0)),
            pl.BlockSpec((d, LANES), lambda i: (0, 0)),
        ],
        out_specs=(pl.BlockSpec((tm * n_sub, LANES), lambda i: (i, 0)),
                   pl.BlockSpec((tm, LANES), lambda i: (i, 0))),
        compiler_params=pltpu.CompilerParams(
            dimension_semantics=("arbitrary",), vmem_limit_bytes=VMEM_LIMIT_BYTES),
        name="router",
    )(x2, norm_w.reshape(1, d).astype(F32), wr)


def _row_copy(n_sub, src_ref, src_row, dst_ref, dst_row, sem):
    return pltpu.make_async_copy(src_ref.at[pl.ds(pl.multiple_of(src_row, n_sub), n_sub)],
                                 dst_ref.at[pl.ds(pl.multiple_of(dst_row, n_sub), n_sub)], sem)


def _tile_indices(dest, tile):
    k, t = dest.shape
    return dest.reshape(k, t // tile, tile).transpose(1, 0, 2).reshape(t // tile, 1, k * tile)


def _scatter_kernel(ts, n_sub, dest_ref, pad_ref, hn_ref, buf_ref, zero_tile, sem, pad_sem):
    @pl.when(pl.program_id(0) == 0)
    def _():
        zero_tile[...] = jnp.zeros(zero_tile.shape, F32)
        n_ranges = pad_ref.shape[0] // 2
        for e in range(n_ranges):
            first, count = pad_ref[e], pad_ref[n_ranges + e]

            def fill(p, carry):
                _row_copy(n_sub, zero_tile, 0, buf_ref, (first + p) * n_sub, pad_sem).start()
                return carry

            def fill_done(p, carry):
                _row_copy(n_sub, zero_tile, 0, buf_ref, 0, pad_sem).wait()
                return carry

            lax.fori_loop(0, count, fill, 0)
            lax.fori_loop(0, count, fill_done, 0)

    def issue(rb, carry):
        for u in range(DMA_UNROLL):
            r = rb * DMA_UNROLL + u
            for k in range(TOP_K):
                _row_copy(n_sub, hn_ref, r * n_sub, buf_ref, dest_ref[0, 0, k * ts + r], sem).start(priority=k % 2)
        return carry

    lax.fori_loop(0, ts // DMA_UNROLL, issue, 0)
    for k in range(TOP_K):
        pltpu.make_async_copy(hn_ref, buf_ref.at[pl.ds(0, ts * n_sub)], sem).wait()


def _scatter_rows(hn, dest, pad_slots, cap, ts, n_sub):
    t = hn.shape[0] // n_sub
    assert t % ts == 0 and ts % DMA_UNROLL == 0
    return pl.pallas_call(
        functools.partial(_scatter_kernel, ts, n_sub),
        out_shape=jax.ShapeDtypeStruct((cap * n_sub, LANES), F32),
        grid=(t // ts,),
        in_specs=[
            pl.BlockSpec((1, 1, TOP_K * ts), lambda i: (i, 0, 0), memory_space=pltpu.SMEM),
            pl.BlockSpec(memory_space=pltpu.SMEM),
            pl.BlockSpec((ts * n_sub, LANES), lambda i: (i, 0)),
        ],
        out_specs=pl.BlockSpec(memory_space=pl.ANY),
        scratch_shapes=[pltpu.VMEM((n_sub, LANES), F32), pltpu.SemaphoreType.DMA(()), pltpu.SemaphoreType.DMA(())],
        compiler_params=pltpu.CompilerParams(
            dimension_semantics=("arbitrary",), vmem_limit_bytes=VMEM_LIMIT_BYTES),
        name="scatter_rows",
    )(_tile_indices(dest, ts), pad_slots, hn)


def _combine_kernel(tc, final_norm, dest_ref, dest_next_ref, x_ref, route_ref, fw_ref, ybuf_ref, out_ref, rows, sems):
    n_sub = x_ref.shape[1] // LANES
    i = pl.program_id(0)
    slot = i % 2

    def gather(idx_ref, s):
        def issue(rb, carry):
            for u in range(DMA_UNROLL):
                r = rb * DMA_UNROLL + u
                for k in range(TOP_K):
                    _row_copy(n_sub, ybuf_ref, idx_ref[0, 0, k * tc + r], rows.at[s, k], r * n_sub,
                              sems.at[s]).start(priority=k % 2)
            return carry

        lax.fori_loop(0, tc // DMA_UNROLL, issue, 0)

    @pl.when(i == 0)
    def _():
        gather(dest_ref, 0)

    @pl.when(i + 1 < pl.num_programs(0))
    def _():
        gather(dest_next_ref, 1 - slot)

    for k in range(TOP_K):
        pltpu.make_async_copy(ybuf_ref.at[pl.ds(0, tc * n_sub)], rows.at[slot, k], sems.at[slot]).wait()

    route = route_ref[...]
    y = x_ref[...]
    for k in range(TOP_K):
        y = y + route[:, TOP_K + k:TOP_K + k + 1] * _tiles_to_rows(rows.at[slot, k], n_sub)
    out_ref[...] = _rmsnorm(y, fw_ref[...]) if final_norm else y


def _combine(x2, route, dest, ybuf, final_w, tc):
    t, d = x2.shape
    assert t % tc == 0 and tc % DMA_UNROLL == 0
    n_tiles = t // tc
    final_norm = final_w is not None
    fw = (final_w if final_norm else jnp.ones((d,), F32)).reshape(1, d).astype(F32)
    idx = _tile_indices(dest, tc)
    return pl.pallas_call(
        functools.partial(_combine_kernel, tc, final_norm),
        out_shape=jax.ShapeDtypeStruct((t, d), F32),
        grid=(n_tiles,),
        in_specs=[
            pl.BlockSpec((1, 1, TOP_K * tc), lambda i: (i, 0, 0), memory_space=pltpu.SMEM),
            pl.BlockSpec((1, 1, TOP_K * tc), lambda i: (jnp.minimum(i + 1, n_tiles - 1), 0, 0), memory_space=pltpu.SMEM),
            pl.BlockSpec((tc, d), lambda i: (i, 0)),
            pl.BlockSpec((tc, LANES), lambda i: (i, 0)),
            pl.BlockSpec((1, d), lambda i: (0, 0)),
            pl.BlockSpec(memory_space=pl.ANY),
        ],
        out_specs=pl.BlockSpec((tc, d), lambda i: (i, 0)),
        scratch_shapes=[pltpu.VMEM((2, TOP_K, tc * (d // LANES), LANES), F32), pltpu.SemaphoreType.DMA((2,))],
        compiler_params=pltpu.CompilerParams(
            dimension_semantics=("arbitrary",), vmem_limit_bytes=VMEM_LIMIT_BYTES),
        name="combine",
    )(idx, idx, x2, route, fw, ybuf)


def _moe_layer(x2, norm_w, w_router, w_gate, w_up, w_down, final_w, tiles):
    t, d = x2.shape
    bm = tiles["moe_bm"]
    hn, route = _router(x2, norm_w, w_router, tiles["router_tm"])
    e_idx = route[:, :TOP_K].astype(jnp.int32).T
    onehot = (e_idx.reshape(-1)[None, :] == jnp.arange(N_EXPERTS, dtype=jnp.int32)[:, None]).astype(jnp.int32)
    csum = jnp.cumsum(onehot, axis=1)
    rank = jnp.sum(csum * onehot, axis=0) - 1
    sizes = csum[:, -1]
    padded = ((sizes + bm - 1) // bm) * bm
    ends = jnp.cumsum(padded)
    starts = ends - padded
    dest = (jnp.sum(starts[:, None] * onehot, axis=0) + rank).astype(jnp.int32).reshape(TOP_K, t)
    cap = TOP_K * t + N_EXPERTS * bm
    n_blocks = cap // bm
    blk_e = jnp.minimum(
        jnp.searchsorted(ends, jnp.arange(n_blocks, dtype=jnp.int32) * bm, side="right"), N_EXPERTS - 1
    ).astype(jnp.int32)
    nact = (ends[-1:] // bm).astype(jnp.int32)

    n_sub = d // LANES
    dest_rows = dest * n_sub
    pad_slots = jnp.concatenate([starts + sizes, ends[-1:], padded - sizes, cap - ends[-1:]]).astype(jnp.int32)
    buf = _scatter_rows(hn, dest_rows, pad_slots, cap, tiles["scatter_ts"], n_sub)
    ybuf = _expert_ffn(buf, blk_e, nact, w_gate, w_up, w_down, bm, tiles["moe_tf"])
    return _combine(x2, route, dest_rows, ybuf, final_w, tiles["combine_tc"])


TILES = {
    "mixer_tl": 512,
    "ffn_tm": 512,
    "router_tm": 1024,
    "moe_bm": 512, "moe_tf": 1792,
    "scatter_ts": 1024,
    "combine_tc": 512,
}


def kernel(x, norm_mix, w_in, conv_a_w, conv_ssd_w, conv_ssd_b, dt_bias, a_log, d_skip, ssd_norm_w, conv_conf_w, conv_conf_b, conf_ln_g, conf_ln_b, w_out, norm_ffn, ffn_w_gate, ffn_w_up, ffn_w_down, moe_router, moe_w_gate, moe_w_up, moe_w_down, norm_final):
    bsz, seq, d = x.shape
    depth = norm_mix.shape[0]
    tiles = dict(TILES)
    tiles["mixer_tl"] = min(tiles["mixer_tl"], seq)
    expert_w = None
    for i in range(depth):
        x = _mixer_layer(x, norm_mix[i], w_in[i], conv_a_w[i], conv_ssd_w[i], conv_ssd_b[i], dt_bias[i], a_log[i],
                         d_skip[i], ssd_norm_w[i], conv_conf_w[i], conv_conf_b[i], conf_ln_g[i], conf_ln_b[i],
                         w_out[i], tiles["mixer_tl"])
        x2 = x.reshape(bsz * seq, d)
        final_w = norm_final if i == depth - 1 else None
        j = i // 2
        if i % 2 == 0:
            later = (moe_w_gate[j], moe_w_up[j], moe_w_down[j]) if i + 1 < depth else ()
            x2, expert_w = _dense_ffn(x2, norm_ffn[i], ffn_w_gate[j], ffn_w_up[j], ffn_w_down[j], final_w,
                                      tiles["ffn_tm"], later)
        else:
            if not expert_w:
                expert_w = [w.astype(BF16) for w in (moe_w_gate[j], moe_w_up[j], moe_w_down[j])]
            x2 = _moe_layer(x2, norm_ffn[i], moe_router[j], *expert_w, final_w, tiles)
            expert_w = None
        x = x2.reshape(bsz, seq, d)
    return x
```

```python
import functools

import jax
import jax.numpy as jnp
from jax import lax
from jax.experimental import pallas as pl
from jax.experimental.pallas import tpu as pltpu

F32 = jnp.float32
BF16 = jnp.bfloat16
EPS = 1e-5

LANES = 128
SUBLANES = 8
VMEM_LIMIT_BYTES = 56 * 1024 * 1024

GROUP_DIM = 64
A_CONV = 3
SSD_HEAD_DIM = 64
SSD_GROUPS = 2
SSD_STATE = 128
SSD_CONV = 4
CONF_KERNEL = 31
N_EXPERTS = 8
TOP_K = 2

SSD_CHUNK = 128
CONF_HALO = 32
SMALL_HALO = SUBLANES
CONV_ROWS = 32
ROW_BLOCK = 64
LANE_BLOCK = 256
DMA_UNROLL = 16


def _split3(v):
    hi = v.astype(BF16)
    r1 = v - hi.astype(F32)
    mid = r1.astype(BF16)
    lo = (r1 - mid.astype(F32)).astype(BF16)
    return hi, mid, lo


def _dot(a, b):
    return jnp.dot(a, b, preferred_element_type=F32)


def _dot_nt(a, b):
    return lax.dot_general(a, b, (((1,), (1,)), ((), ())), preferred_element_type=F32)


def _dot_sel_right(v, sel):
    hi, mid, lo = _split3(v)
    return _dot(hi, sel) + _dot(mid, sel) + _dot(lo, sel)


def _dot_sel_left(sel, v):
    hi, mid, lo = _split3(v)
    return _dot(sel, hi) + _dot(sel, mid) + _dot(sel, lo)


def _softplus(v):
    return jnp.maximum(v, 0.0) + jnp.log1p(jnp.exp(-jnp.abs(v)))


def _silu(v):
    return v * jax.nn.sigmoid(v)


def _rmsnorm(x, w):
    return x * lax.rsqrt(jnp.mean(x * x, axis=-1, keepdims=True) + EPS) * w


def _mixer_kernel(dims, x_ref, nw_ref, win_ref, wdtT_ref, caw_ref, csw_ref, csb_ref, dtbc_ref, dtbr_ref,
                  alc_ref, alr_ref, dsk_ref, snw_ref, ccw_ref, ccb_ref, lng_ref, lnb_ref, wout_ref,
                  out_ref,
                  hn_s, p_a, p_c, p_s, p_z, p_dt, dtr, buf_a, buf_c, shc, xact, ysd, ycat, hstate):
    tl, w_a, w_ssd, w_conf, n_heads, n_l = dims
    q = SSD_CHUNK
    n_bc = SSD_GROUPS * SSD_STATE
    xbc = w_ssd + 2 * n_bc
    hpg = n_heads // SSD_GROUPS
    gw = hpg * SSD_HEAD_DIM
    hb = SMALL_HALO
    o_z = 3 * w_a
    o_xbc = o_z + w_ssd
    o_c = o_xbc + xbc
    o_dt = o_c + 2 * w_conf
    lt = lax.rem(pl.program_id(0), n_l)

    @pl.when(lt == 0)
    def _():
        buf_a[0:hb, :] = jnp.zeros((hb, w_a), F32)
        p_s[0:hb, :] = jnp.zeros((hb, xbc), F32)
        buf_c[0:CONF_HALO, :] = jnp.zeros((CONF_HALO, w_conf), F32)
        hstate[...] = jnp.zeros(hstate.shape, F32)

    @pl.when(lt > 0)
    def _():
        buf_a[0:hb, :] = buf_a[tl:tl + hb, :]
        p_s[0:hb, :] = p_s[tl:tl + hb, :]
        buf_c[0:CONF_HALO, :] = buf_c[tl:tl + CONF_HALO, :]

    hn_s[...] = _rmsnorm(x_ref[...], nw_ref[...]).astype(BF16)

    p_s[hb:hb + tl, :] = _dot(hn_s[...], win_ref[:, o_xbc:o_xbc + xbc])
    for rb in range(tl // ROW_BLOCK):
        for lb in range(xbc // LANE_BLOCK):
            cols = slice(lb * LANE_BLOCK, (lb + 1) * LANE_BLOCK)
            xc = jnp.zeros((ROW_BLOCK, LANE_BLOCK), F32) + csb_ref[:, cols]
            for j in range(SSD_CONV):
                s = hb + rb * ROW_BLOCK - (SSD_CONV - 1) + j
                xc = xc + csw_ref[j:j + 1, cols] * p_s[s:s + ROW_BLOCK, cols]
            xact[rb * ROW_BLOCK:(rb + 1) * ROW_BLOCK, cols] = _silu(xc)
    p_z[...] = _dot(hn_s[...], win_ref[:, o_z:o_z + w_ssd])
    p_dt[...] = _dot(hn_s[...], win_ref[:, o_dt:o_dt + LANES])
    dtr[...] = _dot_nt(wdtT_ref[...], hn_s[...])

    p_c[...] = _dot(hn_s[...], win_ref[:, o_c:o_c + 2 * w_conf])
    for rb in range(tl // ROW_BLOCK):
        r0 = rb * ROW_BLOCK
        buf_c[CONF_HALO + r0:CONF_HALO + r0 + ROW_BLOCK, :] = (
            p_c[r0:r0 + ROW_BLOCK, 0:w_conf] * jax.nn.sigmoid(p_c[r0:r0 + ROW_BLOCK, w_conf:2 * w_conf]))
    n_sh = shc.shape[1]
    for r in range(1, SUBLANES):
        shc[r - 1] = buf_c[r:r + n_sh, :]
    for rb in range(tl // CONV_ROWS):
        acc = jnp.zeros((CONV_ROWS, w_conf), F32) + ccb_ref[...]
        for j in range(CONF_KERNEL):
            a, r = divmod(CONF_HALO - (CONF_KERNEL - 1) + j, SUBLANES)
            base = a * SUBLANES + rb * CONV_ROWS
            src = buf_c[base:base + CONV_ROWS, :] if r == 0 else shc[r - 1, base:base + CONV_ROWS, :]
            acc = acc + ccw_ref[j:j + 1, :] * src
        mu = jnp.mean(acc, axis=-1, keepdims=True)
        cen = acc - mu
        var = jnp.mean(cen * cen, axis=-1, keepdims=True)
        yc = cen * lax.rsqrt(var + EPS) * lng_ref[...] + lnb_ref[...]
        ycat[rb * CONV_ROWS:(rb + 1) * CONV_ROWS, w_a + w_ssd:w_a + w_ssd + w_conf] = _silu(yc).astype(BF16)

    p_a[...] = _dot(hn_s[...], win_ref[:, 0:3 * w_a])
    for rb in range(tl // ROW_BLOCK):
        r0 = rb * ROW_BLOCK
        b0 = hb + r0
        buf_a[b0:b0 + ROW_BLOCK, :] = p_a[r0:r0 + ROW_BLOCK, w_a:2 * w_a] * p_a[r0:r0 + ROW_BLOCK, 2 * w_a:3 * w_a]
        ya = jnp.zeros((ROW_BLOCK, w_a), F32)
        for j in range(A_CONV):
            s = b0 - (A_CONV - 1) + j
            ya = ya + caw_ref[j:j + 1, :] * buf_a[s:s + ROW_BLOCK, :]
        ycat[r0:r0 + ROW_BLOCK, 0:w_a] = (p_a[r0:r0 + ROW_BLOCK, 0:w_a] * ya).astype(BF16)

    a_col = -jnp.exp(alc_ref[...])
    a_row = -jnp.exp(alr_ref[...])
    ri = lax.broadcasted_iota(jnp.int32, (q, q), 0)
    ci = lax.broadcasted_iota(jnp.int32, (q, q), 1)
    causal = ri >= ci
    tril = jnp.where(causal, 1.0, 0.0).astype(BF16)
    triu = jnp.where(ri <= ci, 1.0, 0.0).astype(BF16)
    hd = SSD_HEAD_DIM

    for c in range(tl // q):
        rows = slice(c * q, (c + 1) * q)
        dt_c = _softplus(p_dt[rows, :] + dtbc_ref[...])
        dt_r = _softplus(dtr[:, rows] + dtbr_ref[...])
        acum_c = _dot_sel_left(tril, dt_c * a_col)
        acum_r = _dot_sel_right(dt_r * a_row, triu)
        to_end_r = dt_r * jnp.exp(acum_r[:, q - 1:q] - acum_r)
        for g in range(SSD_GROUPS):
            bg = xact[rows, w_ssd + g * SSD_STATE:w_ssd + (g + 1) * SSD_STATE]
            cg = xact[rows, w_ssd + n_bc + g * SSD_STATE:w_ssd + n_bc + (g + 1) * SSD_STATE].astype(BF16)
            cb = _dot_nt(cg, bg.astype(BF16))
            bgt = bg.T
            hg = hstate[g]
            yoff = _dot(cg, hg.astype(BF16))
            for r in range(hpg):
                h = g * hpg + r
                xh = xact[rows, h * hd:(h + 1) * hd].astype(BF16)
                acol = jnp.broadcast_to(acum_c[:, h:h + 1], (q, q))
                dec = jnp.where(causal, jnp.exp(acol - acum_r[h:h + 1, :]), 0.0)
                yd = _dot((cb * dec * dt_r[h:h + 1, :]).astype(BF16), xh)
                ysd[rows, h * hd:(h + 1) * hd] = yd + yoff[:, r * hd:(r + 1) * hd] * jnp.exp(acol[:, 0:hd])
                st = _dot((bgt * to_end_r[h:h + 1, :]).astype(BF16), xh)
                hstate[g, :, r * hd:(r + 1) * hd] = jnp.exp(acol[q - 1:q, 0:hd]) * hg[:, r * hd:(r + 1) * hd] + st
        xs = xact[rows, 0:w_ssd]
        yz = (ysd[rows, :] + dsk_ref[...] * xs) * _silu(p_z[rows, :])
        for g in range(SSD_GROUPS):
            yg = yz[:, g * gw:(g + 1) * gw]
            yn = yg * lax.rsqrt(jnp.mean(yg * yg, axis=-1, keepdims=True) + EPS) * snw_ref[:, g * gw:(g + 1) * gw]
            ycat[rows, w_a + g * gw:w_a + (g + 1) * gw] = yn.astype(BF16)

    out_ref[...] = x_ref[...] + _dot(ycat[...], wout_ref[...])


def _mixer_layer(x, norm_w, w_in, conv_a_w, conv_ssd_w, conv_ssd_b, dt_bias, a_log, d_skip, ssd_norm_w,
                 conv_conf_w, conv_conf_b, conf_ln_g, conf_ln_b, w_out, tl):
    bsz, seq, d = x.shape
    w_a = conv_a_w.shape[1]
    w_ssd = ssd_norm_w.shape[0]
    w_conf = conv_conf_w.shape[1]
    n_heads = dt_bias.shape[0]
    n_bc = SSD_GROUPS * SSD_STATE
    xbc = w_ssd + 2 * n_bc
    assert n_heads == SUBLANES and n_heads * SSD_HEAD_DIM == w_ssd and seq % tl == 0 and tl % SSD_CHUNK == 0
    assert tl % ROW_BLOCK == 0 and xbc % LANE_BLOCK == 0 and conv_a_w.shape[0] == A_CONV and conv_ssd_w.shape[0] == SSD_CONV
    assert conv_conf_w.shape[0] == CONF_KERNEL
    o_dt = 3 * w_a + w_ssd + xbc
    w_dt = w_in[:, o_dt:o_dt + n_heads]
    w_main = jnp.concatenate(
        [w_in[:, :o_dt], w_in[:, o_dt + n_heads:], jnp.pad(w_dt, ((0, 0), (0, LANES - n_heads)))], axis=1).astype(BF16)
    n_cols = w_main.shape[1]
    w_dt_t = w_dt.T.astype(BF16)
    row = lambda v: v.reshape(1, -1).astype(F32)
    pad_lanes = lambda v: jnp.pad(v.reshape(1, -1).astype(F32), ((0, 0), (0, LANES - v.shape[0])))
    col = lambda v: v.reshape(-1, 1).astype(F32)
    d_skip_x = jnp.repeat(d_skip.astype(F32), SSD_HEAD_DIM).reshape(1, w_ssd)

    const = lambda shape: pl.BlockSpec(shape, lambda s: (0,) * len(shape))
    n_l = seq // tl
    n_tiles = bsz * n_l
    dims = (tl, w_a, w_ssd, w_conf, n_heads, n_l)
    x2 = x.reshape(bsz * seq, d)
    out = pl.pallas_call(
        functools.partial(_mixer_kernel, dims),
        out_shape=jax.ShapeDtypeStruct((bsz * seq, d), F32),
        grid=(n_tiles,),
        in_specs=[
            pl.BlockSpec((tl, d), lambda s: (s, 0)),
            const((1, d)), const((d, n_cols)), const((n_heads, d)),
            const((A_CONV, w_a)), const((SSD_CONV, xbc)), const((1, xbc)),
            const((1, LANES)), const((n_heads, 1)), const((1, LANES)), const((n_heads, 1)),
            const((1, w_ssd)), const((1, w_ssd)),
            const((CONF_KERNEL, w_conf)), const((1, w_conf)), const((1, w_conf)), const((1, w_conf)),
            const((d, d)),
        ],
        out_specs=pl.BlockSpec((tl, d), lambda s: (s, 0)),
        scratch_shapes=[
            pltpu.VMEM((tl, d), BF16),
            pltpu.VMEM((tl, 3 * w_a), F32),
            pltpu.VMEM((tl, 2 * w_conf), F32),
            pltpu.VMEM((SMALL_HALO + tl, xbc), F32),
            pltpu.VMEM((tl, w_ssd), F32),
            pltpu.VMEM((tl, LANES), F32),
            pltpu.VMEM((n_heads, tl), F32),
            pltpu.VMEM((SMALL_HALO + tl, w_a), F32),
            pltpu.VMEM((CONF_HALO + tl, w_conf), F32),
            pltpu.VMEM((SUBLANES - 1, tl + CONF_HALO - SUBLANES, w_conf), F32),
            pltpu.VMEM((tl, xbc), F32),
            pltpu.VMEM((tl, w_ssd), F32),
            pltpu.VMEM((tl, d), BF16),
            pltpu.VMEM((SSD_GROUPS, SSD_STATE, w_ssd // SSD_GROUPS), F32),
        ],
        compiler_params=pltpu.CompilerParams(
            dimension_semantics=("arbitrary",), vmem_limit_bytes=VMEM_LIMIT_BYTES),
        name="mixer",
    )(x2, row(norm_w), w_main, w_dt_t, conv_a_w.astype(F32), conv_ssd_w.astype(F32), row(conv_ssd_b),
      pad_lanes(dt_bias), col(dt_bias), pad_lanes(a_log), col(a_log), d_skip_x, row(ssd_norm_w),
      conv_conf_w.astype(F32), row(conv_conf_b), row(conf_ln_g), row(conf_ln_b), w_out.astype(BF16))
    return out.reshape(bsz, seq, d)


def _dense_ffn_kernel(final_norm, n_cast, x_ref, nw_ref, wg_ref, wu_ref, wd_ref, fw_ref, *rest):
    cast_in, out_ref, cast_out, hn = rest[:n_cast], rest[n_cast], rest[n_cast + 1:2 * n_cast + 1], rest[-1]
    hn[...] = _rmsnorm(x_ref[...], nw_ref[...]).astype(BF16)
    a = (_silu(_dot(hn[...], wg_ref[...])) * _dot(hn[...], wu_ref[...])).astype(BF16)
    y = x_ref[...] + _dot(a, wd_ref[...])
    out_ref[...] = _rmsnorm(y, fw_ref[...]) if final_norm else y
    for src, dst in zip(cast_in, cast_out):
        dst[...] = src[...].astype(BF16)


def _dense_ffn(x2, norm_w, w_gate, w_up, w_down, final_w, tm, later=()):
    t, d = x2.shape
    ff = w_gate.shape[1]
    assert t % tm == 0
    n_steps = t // tm
    final_norm = final_w is not None
    fw = (final_w if final_norm else jnp.ones((d,), F32)).reshape(1, d).astype(F32)
    resident = lambda shape: pl.BlockSpec(shape, lambda i: (0, 0), pipeline_mode=pl.Buffered(1))
    bf16_rows = 2 * SUBLANES
    slabs = []
    for w in later:
        rows = w.size // w.shape[-1]
        assert rows % (n_steps * bf16_rows) == 0
        slabs.append((rows // n_steps, w.shape[-1]))
    slab_specs = [pl.BlockSpec(s, lambda i: (i, 0)) for s in slabs]
    res = pl.pallas_call(
        functools.partial(_dense_ffn_kernel, final_norm, len(later)),
        out_shape=[jax.ShapeDtypeStruct((t, d), F32)]
        + [jax.ShapeDtypeStruct((n_steps * r, c), BF16) for r, c in slabs],
        grid=(n_steps,),
        in_specs=[
            pl.BlockSpec((tm, d), lambda i: (i, 0)),
            resident((1, d)), resident((d, ff)), resident((d, ff)), resident((ff, d)), resident((1, d)),
        ] + slab_specs,
        out_specs=[pl.BlockSpec((tm, d), lambda i: (i, 0))] + slab_specs,
        scratch_shapes=[pltpu.VMEM((tm, d), BF16)],
        compiler_params=pltpu.CompilerParams(
            dimension_semantics=("arbitrary",), vmem_limit_bytes=VMEM_LIMIT_BYTES),
        name="dense_ffn",
    )(x2, norm_w.reshape(1, d).astype(F32), w_gate.astype(BF16), w_up.astype(BF16), w_down.astype(BF16), fw,
      *[w.reshape(n_steps * r, c) for w, (r, c) in zip(later, slabs)])
    return res[0], [b.reshape(w.shape) for b, w in zip(res[1:], later)]


def _rows_to_tiles(tile_ref, val):
    n, d = val.shape
    n_sub = d // LANES
    for c in range(n_sub):
        tile_ref[pl.ds(c, n, stride=n_sub), :] = val[:, c * LANES:(c + 1) * LANES]


def _tiles_to_rows(tile_ref, n_sub):
    n = tile_ref.shape[0] // n_sub
    return jnp.concatenate([tile_ref[pl.ds(c, n, stride=n_sub), :] for c in range(n_sub)], axis=1)


def _expert_ffn_kernel(blk_e_ref, nact_ref, xb_ref, wg_ref, wu_ref, wd_ref, out_ref, hb, acc):
    i = pl.program_id(0)
    j = pl.program_id(1)

    n_j = pl.num_programs(1)
    active = i < nact_ref[0]

    def partial_out():
        h = hb[...]
        a = (_silu(_dot(h, wg_ref[...])) * _dot(h, wu_ref[...])).astype(BF16)
        return _dot(a, wd_ref[...])

    @pl.when(active & (j == 0))
    def _():
        hb[...] = _tiles_to_rows(xb_ref, hb.shape[1] // LANES).astype(BF16)
        acc[...] = partial_out()

    @pl.when(active & (j > 0) & (j < n_j - 1))
    def _():
        acc[...] += partial_out()

    @pl.when(active & (j == n_j - 1))
    def _():
        _rows_to_tiles(out_ref, acc[...] + partial_out())

    @pl.when((i >= nact_ref[0]) & (j == 0))
    def _():
        out_ref[...] = jnp.zeros(out_ref.shape, F32)


def _expert_ffn(buf, blk_e, nact, w_gate, w_up, w_down, bm, tf):
    d, ff = w_gate.shape[1], w_gate.shape[2]
    n_sub = d // LANES
    cap = buf.shape[0] // n_sub
    n_blocks = cap // bm
    n_j = ff // tf
    assert cap % bm == 0 and ff % tf == 0 and n_j >= 2 and w_gate.dtype == BF16
    row_blk = lambda i, j, be, na: (jnp.minimum(i, na[0] - 1), 0)
    snake = lambda i, j: jnp.where(i % 2 == 0, j, n_j - 1 - j)
    col_j = lambda i, j, na: jnp.where(i < na[0], snake(i, j), snake(na[0] - 1, n_j - 1))
    return pl.pallas_call(
        _expert_ffn_kernel,
        out_shape=jax.ShapeDtypeStruct((cap * n_sub, LANES), F32),
        grid_spec=pltpu.PrefetchScalarGridSpec(
            num_scalar_prefetch=2,
            grid=(n_blocks, n_j),
            in_specs=[
                pl.BlockSpec((bm * n_sub, LANES), row_blk),
                pl.BlockSpec((None, d, tf), lambda i, j, be, na: (be[i], 0, col_j(i, j, na))),
                pl.BlockSpec((None, d, tf), lambda i, j, be, na: (be[i], 0, col_j(i, j, na))),
                pl.BlockSpec((None, tf, d), lambda i, j, be, na: (be[i], col_j(i, j, na), 0)),
            ],
            out_specs=pl.BlockSpec((bm * n_sub, LANES), lambda i, j, be, na: (i, 0)),
            scratch_shapes=[pltpu.VMEM((bm, d), BF16), pltpu.VMEM((bm, d), F32)],
        ),
        compiler_params=pltpu.CompilerParams(
            dimension_semantics=("arbitrary", "arbitrary"), vmem_limit_bytes=VMEM_LIMIT_BYTES),
        name="expert_ffn",
    )(blk_e, nact, buf, w_gate, w_up, w_down)


RANK_BLOCK = 256


def _router_kernel(x_ref, nw_ref, wr_ref, hn_ref, route_ref, count_ref):
    h = _rmsnorm(x_ref[...], nw_ref[...])
    _rows_to_tiles(hn_ref, h)
    h_hi, h_lo, _ = _split3(h)
    w_hi, w_lo, _ = _split3(wr_ref[...])
    logits = _dot(h_hi, w_hi) + _dot(h_hi, w_lo) + _dot(h_lo, w_hi)
    lane = lax.broadcasted_iota(jnp.int32, logits.shape, 1)
    neg = jnp.float32(-jnp.inf)
    logits = jnp.where(lane < N_EXPERTS, logits, neg)
    m1 = jnp.max(logits, axis=-1, keepdims=True)
    i1 = jnp.min(jnp.where(logits == m1, lane, LANES), axis=-1, keepdims=True)
    rest = jnp.where(lane == i1, neg, logits)
    m2 = jnp.max(rest, axis=-1, keepdims=True)
    i2 = jnp.min(jnp.where(rest == m2, lane, LANES), axis=-1, keepdims=True)
    e2 = jnp.exp(m2 - m1)
    g1 = 1.0 / (1.0 + e2)
    g2 = e2 * g1
    pick1, pick2 = lane == i1, lane == i2
    onehot = jnp.where(pick1 | pick2, 1.0, 0.0)
    rb = RANK_BLOCK
    ri = lax.broadcasted_iota(jnp.int32, (rb, rb), 0)
    ci = lax.broadcasted_iota(jnp.int32, (rb, rb), 1)
    before = jnp.where(ri > ci, 1.0, 0.0).astype(BF16)
    total = jnp.zeros((1, LANES), F32)
    earlier = []
    for b in range(h.shape[0] // rb):
        oh_b = onehot[b * rb:(b + 1) * rb, :]
        earlier.append(_dot(before, oh_b.astype(BF16)) + total)
        total = total + jnp.sum(oh_b, axis=0, keepdims=True)
    earlier = jnp.concatenate(earlier, axis=0)
    r1 = jnp.sum(jnp.where(pick1, earlier, 0.0), axis=-1, keepdims=True)
    r2 = jnp.sum(jnp.where(pick2, earlier, 0.0), axis=-1, keepdims=True)
    cols = (i1.astype(F32), i2.astype(F32), g1, g2, r1, r2)
    route = jnp.zeros(logits.shape, F32)
    for c, v in enumerate(cols):
        route = jnp.where(lane == c, v, route)
    route_ref[...] = route
    count_ref[...] = jnp.broadcast_to(total, count_ref.shape)


def _router(x2, norm_w, w_router, tm):
    t, d = x2.shape
    n_e = w_router.shape[1]
    assert n_e == N_EXPERTS and t % tm == 0 and d % LANES == 0 and tm % RANK_BLOCK == 0
    n_sub = d // LANES
    wr = jnp.pad(w_router.astype(F32), ((0, 0), (0, LANES - n_e)))
    return pl.pallas_call(
        _router_kernel,
        out_shape=(jax.ShapeDtypeStruct((t * n_sub, LANES), F32), jax.ShapeDtypeStruct((t, LANES), F32),
                   jax.ShapeDtypeStruct((t // tm * SUBLANES, LANES), F32)),
        grid=(t // tm,),
        in_specs=[
            pl.BlockSpec((tm, d), lambda i: (i, 0)),
            pl.BlockSpec((1, d), lambda i: (0, 0)),
            pl.BlockSpec((d, LANES), lambda i: (0, 0)),
        ],
        out_specs=(pl.BlockSpec((tm * n_sub, LANES), lambda i: (i, 0)),
                   pl.BlockSpec((tm, LANES), lambda i: (i, 0)),
                   pl.BlockSpec((SUBLANES, LANES), lambda i: (i, 0))),
        compiler_params=pltpu.CompilerParams(
            dimension_semantics=("arbitrary",), vmem_limit_bytes=VMEM_LIMIT_BYTES),
        name="router",
    )(x2, norm_w.reshape(1, d).astype(F32), wr)


def _row_copy(n_sub, src_ref, src_row, dst_ref, dst_row, sem):
    return pltpu.make_async_copy(src_ref.at[pl.ds(pl.multiple_of(src_row, n_sub), n_sub)],
                                 dst_ref.at[pl.ds(pl.multiple_of(dst_row, n_sub), n_sub)], sem)


def _tile_indices(dest, tile):
    k, t = dest.shape
    return dest.reshape(k, t // tile, tile).transpose(1, 0, 2).reshape(t // tile, 1, k * tile)


def _scatter_kernel(ts, n_sub, dest_ref, pad_ref, hn_ref, buf_ref, zero_tile, sem, pad_sem):
    @pl.when(pl.program_id(0) == 0)
    def _():
        zero_tile[...] = jnp.zeros(zero_tile.shape, F32)
        n_ranges = pad_ref.shape[0] // 2
        for e in range(n_ranges):
            first, count = pad_ref[e], pad_ref[n_ranges + e]

            def fill(p, carry):
                _row_copy(n_sub, zero_tile, 0, buf_ref, (first + p) * n_sub, pad_sem).start()
                return carry

            def fill_done(p, carry):
                _row_copy(n_sub, zero_tile, 0, buf_ref, 0, pad_sem).wait()
                return carry

            lax.fori_loop(0, count, fill, 0)
            lax.fori_loop(0, count, fill_done, 0)

    def issue(rb, carry):
        for u in range(DMA_UNROLL):
            r = rb * DMA_UNROLL + u
            for k in range(TOP_K):
                _row_copy(n_sub, hn_ref, r * n_sub, buf_ref, dest_ref[0, 0, k * ts + r], sem).start(priority=k % 2)
        return carry

    lax.fori_loop(0, ts // DMA_UNROLL, issue, 0)
    for k in range(TOP_K):
        pltpu.make_async_copy(hn_ref, buf_ref.at[pl.ds(0, ts * n_sub)], sem).wait()


def _scatter_rows(hn, dest, pad_slots, cap, ts, n_sub):
    t = hn.shape[0] // n_sub
    assert t % ts == 0 and ts % DMA_UNROLL == 0
    return pl.pallas_call(
        functools.partial(_scatter_kernel, ts, n_sub),
        out_shape=jax.ShapeDtypeStruct((cap * n_sub, LANES), F32),
        grid=(t // ts,),
        in_specs=[
            pl.BlockSpec((1, 1, TOP_K * ts), lambda i: (i, 0, 0), memory_space=pltpu.SMEM),
            pl.BlockSpec(memory_space=pltpu.SMEM),
            pl.BlockSpec((ts * n_sub, LANES), lambda i: (i, 0)),
        ],
        out_specs=pl.BlockSpec(memory_space=pl.ANY),
        scratch_shapes=[pltpu.VMEM((n_sub, LANES), F32), pltpu.SemaphoreType.DMA(()), pltpu.SemaphoreType.DMA(())],
        compiler_params=pltpu.CompilerParams(
            dimension_semantics=("arbitrary",), vmem_limit_bytes=VMEM_LIMIT_BYTES),
        name="scatter_rows",
    )(_tile_indices(dest, ts), pad_slots, hn)


def _combine_kernel(tc, final_norm, dest_ref, dest_next_ref, x_ref, route_ref, fw_ref, ybuf_ref, out_ref, rows, sems):
    n_sub = x_ref.shape[1] // LANES
    i = pl.program_id(0)
    slot = i % 2

    def gather(idx_ref, s):
        def issue(rb, carry):
            for u in range(DMA_UNROLL):
                r = rb * DMA_UNROLL + u
                for k in range(TOP_K):
                    _row_copy(n_sub, ybuf_ref, idx_ref[0, 0, k * tc + r], rows.at[s, k], r * n_sub,
                              sems.at[s]).start(priority=k % 2)
            return carry

        lax.fori_loop(0, tc // DMA_UNROLL, issue, 0)

    @pl.when(i == 0)
    def _():
        gather(dest_ref, 0)

    @pl.when(i + 1 < pl.num_programs(0))
    def _():
        gather(dest_next_ref, 1 - slot)

    for k in range(TOP_K):
        pltpu.make_async_copy(ybuf_ref.at[pl.ds(0, tc * n_sub)], rows.at[slot, k], sems.at[slot]).wait()

    route = route_ref[...]
    y = x_ref[...]
    for k in range(TOP_K):
        y = y + route[:, TOP_K + k:TOP_K + k + 1] * _tiles_to_rows(rows.at[slot, k], n_sub)
    out_ref[...] = _rmsnorm(y, fw_ref[...]) if final_norm else y


def _combine(x2, route, dest, ybuf, final_w, tc):
    t, d = x2.shape
    assert t % tc == 0 and tc % DMA_UNROLL == 0
    n_tiles = t // tc
    final_norm = final_w is not None
    fw = (final_w if final_norm else jnp.ones((d,), F32)).reshape(1, d).astype(F32)
    idx = _tile_indices(dest, tc)
    return pl.pallas_call(
        functools.partial(_combine_kernel, tc, final_norm),
        out_shape=jax.ShapeDtypeStruct((t, d), F32),
        grid=(n_tiles,),
        in_specs=[
            pl.BlockSpec((1, 1, TOP_K * tc), lambda i: (i, 0, 0), memory_space=pltpu.SMEM),
            pl.BlockSpec((1, 1, TOP_K * tc), lambda i: (jnp.minimum(i + 1, n_tiles - 1), 0, 0), memory_space=pltpu.SMEM),
            pl.BlockSpec((tc, d), lambda i: (i, 0)),
            pl.BlockSpec((tc, LANES), lambda i: (i, 0)),
            pl.BlockSpec((1, d), lambda i: (0, 0)),
            pl.BlockSpec(memory_space=pl.ANY),
        ],
        out_specs=pl.BlockSpec((tc, d), lambda i: (i, 0)),
        scratch_shapes=[pltpu.VMEM((2, TOP_K, tc * (d // LANES), LANES), F32), pltpu.SemaphoreType.DMA((2,))],
        compiler_params=pltpu.CompilerParams(
            dimension_semantics=("arbitrary",), vmem_limit_bytes=VMEM_LIMIT_BYTES),
        name="combine",
    )(idx, idx, x2, route, fw, ybuf)


def _moe_layer(x2, norm_w, w_router, w_gate, w_up, w_down, final_w, tiles):
    t, d = x2.shape
    bm = tiles["moe_bm"]
    tm = tiles["router_tm"]
    hn, route, counts = _router(x2, norm_w, w_router, tm)
    counts = counts[::SUBLANES, :N_EXPERTS].astype(jnp.int32)
    sizes = jnp.sum(counts, axis=0)
    padded = ((sizes + bm - 1) // bm) * bm
    ends = jnp.cumsum(padded)
    starts = ends - padded
    tile_base = starts[None, :] + jnp.cumsum(counts, axis=0) - counts
    e_idx = route[:, :TOP_K].astype(jnp.int32).reshape(t // tm, tm, TOP_K)
    rank = route[:, 2 * TOP_K:3 * TOP_K].astype(jnp.int32).reshape(t // tm, tm, TOP_K)
    onehot = e_idx[..., None] == jnp.arange(N_EXPERTS, dtype=jnp.int32)
    base = jnp.sum(jnp.where(onehot, tile_base[:, None, None, :], 0), axis=-1)
    dest = (base + rank).reshape(t, TOP_K).T.astype(jnp.int32)
    cap = TOP_K * t + N_EXPERTS * bm
    n_blocks = cap // bm
    blk_e = jnp.minimum(
        jnp.searchsorted(ends, jnp.arange(n_blocks, dtype=jnp.int32) * bm, side="right"), N_EXPERTS - 1
    ).astype(jnp.int32)
    nact = (ends[-1:] // bm).astype(jnp.int32)

    n_sub = d // LANES
    dest_rows = dest * n_sub
    pad_slots = jnp.concatenate([starts + sizes, ends[-1:], padded - sizes, cap - ends[-1:]]).astype(jnp.int32)
    buf = _scatter_rows(hn, dest_rows, pad_slots, cap, tiles["scatter_ts"], n_sub)
    ybuf = _expert_ffn(buf, blk_e, nact, w_gate, w_up, w_down, bm, tiles["moe_tf"])
    return _combine(x2, route, dest_rows, ybuf, final_w, tiles["combine_tc"])


TILES = {
    "mixer_tl": 512,
    "ffn_tm": 512,
    "router_tm": 1024,
    "moe_bm": 512, "moe_tf": 1792,
    "scatter_ts": 1024,
    "combine_tc": 512,
}


def kernel(x, norm_mix, w_in, conv_a_w, conv_ssd_w, conv_ssd_b, dt_bias, a_log, d_skip, ssd_norm_w, conv_conf_w, conv_conf_b, conf_ln_g, conf_ln_b, w_out, norm_ffn, ffn_w_gate, ffn_w_up, ffn_w_down, moe_router, moe_w_gate, moe_w_up, moe_w_down, norm_final):
    bsz, seq, d = x.shape
    depth = norm_mix.shape[0]
    tiles = dict(TILES)
    tiles["mixer_tl"] = min(tiles["mixer_tl"], seq)
    expert_w = None
    for i in range(depth):
        x = _mixer_layer(x, norm_mix[i], w_in[i], conv_a_w[i], conv_ssd_w[i], conv_ssd_b[i], dt_bias[i], a_log[i],
                         d_skip[i], ssd_norm_w[i], conv_conf_w[i], conv_conf_b[i], conf_ln_g[i], conf_ln_b[i],
                         w_out[i], tiles["mixer_tl"])
        x2 = x.reshape(bsz * seq, d)
        final_w = norm_final if i == depth - 1 else None
        j = i // 2
        if i % 2 == 0:
            later = (moe_w_gate[j], moe_w_up[j], moe_w_down[j]) if i + 1 < depth else ()
            x2, expert_w = _dense_ffn(x2, norm_ffn[i], ffn_w_gate[j], ffn_w_up[j], ffn_w_down[j], final_w,
                                      tiles["ffn_tm"], later)
        else:
            if not expert_w:
                expert_w = [w.astype(BF16) for w in (moe_w_gate[j], moe_w_up[j], moe_w_down[j])]
            x2 = _moe_layer(x2, norm_ffn[i], moe_router[j], *expert_w, final_w, tiles)
            expert_w = None
        x = x2.reshape(bsz, seq, d)
    return x
```

```python
import functools

import jax
import jax.numpy as jnp
from jax import lax
from jax.experimental import pallas as pl
from jax.experimental.pallas import tpu as pltpu

F32 = jnp.float32
BF16 = jnp.bfloat16
EPS = 1e-5

LANES = 128
SUBLANES = 8
VMEM_LIMIT_BYTES = 56 * 1024 * 1024

GROUP_DIM = 64
A_CONV = 3
SSD_HEAD_DIM = 64
SSD_GROUPS = 2
SSD_STATE = 128
SSD_CONV = 4
CONF_KERNEL = 31
N_EXPERTS = 8
TOP_K = 2

SSD_CHUNK = 128
CONF_HALO = 32
SMALL_HALO = SUBLANES
CONV_ROWS = 32
ROW_BLOCK = 64
LANE_BLOCK = 256
DMA_UNROLL = 16


def _split3(v):
    hi = v.astype(BF16)
    r1 = v - hi.astype(F32)
    mid = r1.astype(BF16)
    lo = (r1 - mid.astype(F32)).astype(BF16)
    return hi, mid, lo


def _dot(a, b):
    return jnp.dot(a, b, preferred_element_type=F32)


def _dot_nt(a, b):
    return lax.dot_general(a, b, (((1,), (1,)), ((), ())), preferred_element_type=F32)


def _dot_sel_right(v, sel):
    hi, mid, lo = _split3(v)
    return _dot(hi, sel) + _dot(mid, sel) + _dot(lo, sel)


def _dot_sel_left(sel, v):
    hi, mid, lo = _split3(v)
    return _dot(sel, hi) + _dot(sel, mid) + _dot(sel, lo)


def _softplus(v):
    return jnp.maximum(v, 0.0) + jnp.log1p(jnp.exp(-jnp.abs(v)))


def _silu(v):
    return v * jax.nn.sigmoid(v)


def _rmsnorm(x, w):
    return x * lax.rsqrt(jnp.mean(x * x, axis=-1, keepdims=True) + EPS) * w


def _mixer_kernel(dims, x_ref, nw_ref, win_ref, wdtT_ref, caw_ref, csw_ref, csb_ref, dtbc_ref, dtbr_ref,
                  alc_ref, alr_ref, dsk_ref, snw_ref, ccw_ref, ccb_ref, lng_ref, lnb_ref, wout_ref,
                  out_ref,
                  hn_s, p_a, p_c, p_s, p_z, p_dt, dtr, buf_a, buf_c, shc, xact, ysd, ycat, hstate):
    tl, w_a, w_ssd, w_conf, n_heads, n_l = dims
    q = SSD_CHUNK
    n_bc = SSD_GROUPS * SSD_STATE
    xbc = w_ssd + 2 * n_bc
    hpg = n_heads // SSD_GROUPS
    gw = hpg * SSD_HEAD_DIM
    hb = SMALL_HALO
    o_z = 3 * w_a
    o_xbc = o_z + w_ssd
    o_c = o_xbc + xbc
    o_dt = o_c + 2 * w_conf
    lt = lax.rem(pl.program_id(0), n_l)

    @pl.when(lt == 0)
    def _():
        buf_a[0:hb, :] = jnp.zeros((hb, w_a), F32)
        p_s[0:hb, :] = jnp.zeros((hb, xbc), F32)
        buf_c[0:CONF_HALO, :] = jnp.zeros((CONF_HALO, w_conf), F32)
        hstate[...] = jnp.zeros(hstate.shape, F32)

    @pl.when(lt > 0)
    def _():
        buf_a[0:hb, :] = buf_a[tl:tl + hb, :]
        p_s[0:hb, :] = p_s[tl:tl + hb, :]
        buf_c[0:CONF_HALO, :] = buf_c[tl:tl + CONF_HALO, :]

    hn_s[...] = _rmsnorm(x_ref[...], nw_ref[...]).astype(BF16)

    p_s[hb:hb + tl, :] = _dot(hn_s[...], win_ref[:, o_xbc:o_xbc + xbc])
    for rb in range(tl // ROW_BLOCK):
        for lb in range(xbc // LANE_BLOCK):
            cols = slice(lb * LANE_BLOCK, (lb + 1) * LANE_BLOCK)
            xc = jnp.zeros((ROW_BLOCK, LANE_BLOCK), F32) + csb_ref[:, cols]
            for j in range(SSD_CONV):
                s = hb + rb * ROW_BLOCK - (SSD_CONV - 1) + j
                xc = xc + csw_ref[j:j + 1, cols] * p_s[s:s + ROW_BLOCK, cols]
            xact[rb * ROW_BLOCK:(rb + 1) * ROW_BLOCK, cols] = _silu(xc)
    p_z[...] = _dot(hn_s[...], win_ref[:, o_z:o_z + w_ssd])
    p_dt[...] = _dot(hn_s[...], win_ref[:, o_dt:o_dt + LANES])
    dtr[...] = _dot_nt(wdtT_ref[...], hn_s[...])

    p_c[...] = _dot(hn_s[...], win_ref[:, o_c:o_c + 2 * w_conf])
    for rb in range(tl // ROW_BLOCK):
        r0 = rb * ROW_BLOCK
        buf_c[CONF_HALO + r0:CONF_HALO + r0 + ROW_BLOCK, :] = (
            p_c[r0:r0 + ROW_BLOCK, 0:w_conf] * jax.nn.sigmoid(p_c[r0:r0 + ROW_BLOCK, w_conf:2 * w_conf]))
    n_sh = shc.shape[1]
    for r in range(1, SUBLANES):
        shc[r - 1] = buf_c[r:r + n_sh, :]
    for rb in range(tl // CONV_ROWS):
        acc = jnp.zeros((CONV_ROWS, w_conf), F32) + ccb_ref[...]
        for j in range(CONF_KERNEL):
            a, r = divmod(CONF_HALO - (CONF_KERNEL - 1) + j, SUBLANES)
            base = a * SUBLANES + rb * CONV_ROWS
            src = buf_c[base:base + CONV_ROWS, :] if r == 0 else shc[r - 1, base:base + CONV_ROWS, :]
            acc = acc + ccw_ref[j:j + 1, :] * src
        mu = jnp.mean(acc, axis=-1, keepdims=True)
        cen = acc - mu
        var = jnp.mean(cen * cen, axis=-1, keepdims=True)
        yc = cen * lax.rsqrt(var + EPS) * lng_ref[...] + lnb_ref[...]
        ycat[rb * CONV_ROWS:(rb + 1) * CONV_ROWS, w_a + w_ssd:w_a + w_ssd + w_conf] = _silu(yc).astype(BF16)

    p_a[...] = _dot(hn_s[...], win_ref[:, 0:3 * w_a])
    for rb in range(tl // ROW_BLOCK):
        r0 = rb * ROW_BLOCK
        b0 = hb + r0
        buf_a[b0:b0 + ROW_BLOCK, :] = p_a[r0:r0 + ROW_BLOCK, w_a:2 * w_a] * p_a[r0:r0 + ROW_BLOCK, 2 * w_a:3 * w_a]
        ya = jnp.zeros((ROW_BLOCK, w_a), F32)
        for j in range(A_CONV):
            s = b0 - (A_CONV - 1) + j
            ya = ya + caw_ref[j:j + 1, :] * buf_a[s:s + ROW_BLOCK, :]
        ycat[r0:r0 + ROW_BLOCK, 0:w_a] = (p_a[r0:r0 + ROW_BLOCK, 0:w_a] * ya).astype(BF16)

    a_col = -jnp.exp(alc_ref[...])
    a_row = -jnp.exp(alr_ref[...])
    ri = lax.broadcasted_iota(jnp.int32, (q, q), 0)
    ci = lax.broadcasted_iota(jnp.int32, (q, q), 1)
    causal = ri >= ci
    tril = jnp.where(causal, 1.0, 0.0).astype(BF16)
    triu = jnp.where(ri <= ci, 1.0, 0.0).astype(BF16)
    hd = SSD_HEAD_DIM

    for c in range(tl // q):
        rows = slice(c * q, (c + 1) * q)
        dt_c = _softplus(p_dt[rows, :] + dtbc_ref[...])
        dt_r = _softplus(dtr[:, rows] + dtbr_ref[...])
        acum_c = _dot_sel_left(tril, dt_c * a_col)
        acum_r = _dot_sel_right(dt_r * a_row, triu)
        to_end_r = dt_r * jnp.exp(acum_r[:, q - 1:q] - acum_r)
        for g in range(SSD_GROUPS):
            bg = xact[rows, w_ssd + g * SSD_STATE:w_ssd + (g + 1) * SSD_STATE]
            cg = xact[rows, w_ssd + n_bc + g * SSD_STATE:w_ssd + n_bc + (g + 1) * SSD_STATE].astype(BF16)
            cb = _dot_nt(cg, bg.astype(BF16))
            bgt = bg.T
            hg = hstate[g]
            yoff = _dot(cg, hg.astype(BF16))
            for r in range(hpg):
                h = g * hpg + r
                xh = xact[rows, h * hd:(h + 1) * hd].astype(BF16)
                acol = jnp.broadcast_to(acum_c[:, h:h + 1], (q, q))
                dec = jnp.where(causal, jnp.exp(acol - acum_r[h:h + 1, :]), 0.0)
                yd = _dot((cb * dec * dt_r[h:h + 1, :]).astype(BF16), xh)
                ysd[rows, h * hd:(h + 1) * hd] = yd + yoff[:, r * hd:(r + 1) * hd] * jnp.exp(acol[:, 0:hd])
                st = _dot((bgt * to_end_r[h:h + 1, :]).astype(BF16), xh)
                hstate[g, :, r * hd:(r + 1) * hd] = jnp.exp(acol[q - 1:q, 0:hd]) * hg[:, r * hd:(r + 1) * hd] + st
        xs = xact[rows, 0:w_ssd]
        yz = (ysd[rows, :] + dsk_ref[...] * xs) * _silu(p_z[rows, :])
        for g in range(SSD_GROUPS):
            yg = yz[:, g * gw:(g + 1) * gw]
            yn = yg * lax.rsqrt(jnp.mean(yg * yg, axis=-1, keepdims=True) + EPS) * snw_ref[:, g * gw:(g + 1) * gw]
            ycat[rows, w_a + g * gw:w_a + (g + 1) * gw] = yn.astype(BF16)

    out_ref[...] = x_ref[...] + _dot(ycat[...], wout_ref[...])


def _mixer_layer(x, norm_w, w_in, conv_a_w, conv_ssd_w, conv_ssd_b, dt_bias, a_log, d_skip, ssd_norm_w,
                 conv_conf_w, conv_conf_b, conf_ln_g, conf_ln_b, w_out, tl):
    bsz, seq, d = x.shape
    w_a = conv_a_w.shape[1]
    w_ssd = ssd_norm_w.shape[0]
    w_conf = conv_conf_w.shape[1]
    n_heads = dt_bias.shape[0]
    n_bc = SSD_GROUPS * SSD_STATE
    xbc = w_ssd + 2 * n_bc
    assert n_heads == SUBLANES and n_heads * SSD_HEAD_DIM == w_ssd and seq % tl == 0 and tl % SSD_CHUNK == 0
    assert tl % ROW_BLOCK == 0 and xbc % LANE_BLOCK == 0 and conv_a_w.shape[0] == A_CONV and conv_ssd_w.shape[0] == SSD_CONV
    assert conv_conf_w.shape[0] == CONF_KERNEL
    o_dt = 3 * w_a + w_ssd + xbc
    w_dt = w_in[:, o_dt:o_dt + n_heads]
    w_main = jnp.concatenate(
        [w_in[:, :o_dt], w_in[:, o_dt + n_heads:], jnp.pad(w_dt, ((0, 0), (0, LANES - n_heads)))], axis=1).astype(BF16)
    n_cols = w_main.shape[1]
    w_dt_t = w_dt.T.astype(BF16)
    row = lambda v: v.reshape(1, -1).astype(F32)
    pad_lanes = lambda v: jnp.pad(v.reshape(1, -1).astype(F32), ((0, 0), (0, LANES - v.shape[0])))
    col = lambda v: v.reshape(-1, 1).astype(F32)
    d_skip_x = jnp.repeat(d_skip.astype(F32), SSD_HEAD_DIM).reshape(1, w_ssd)

    const = lambda shape: pl.BlockSpec(shape, lambda s: (0,) * len(shape))
    n_l = seq // tl
    n_tiles = bsz * n_l
    dims = (tl, w_a, w_ssd, w_conf, n_heads, n_l)
    x2 = x.reshape(bsz * seq, d)
    out = pl.pallas_call(
        functools.partial(_mixer_kernel, dims),
        out_shape=jax.ShapeDtypeStruct((bsz * seq, d), F32),
        grid=(n_tiles,),
        in_specs=[
            pl.BlockSpec((tl, d), lambda s: (s, 0)),
            const((1, d)), const((d, n_cols)), const((n_heads, d)),
            const((A_CONV, w_a)), const((SSD_CONV, xbc)), const((1, xbc)),
            const((1, LANES)), const((n_heads, 1)), const((1, LANES)), const((n_heads, 1)),
            const((1, w_ssd)), const((1, w_ssd)),
            const((CONF_KERNEL, w_conf)), const((1, w_conf)), const((1, w_conf)), const((1, w_conf)),
            const((d, d)),
        ],
        out_specs=pl.BlockSpec((tl, d), lambda s: (s, 0)),
        scratch_shapes=[
            pltpu.VMEM((tl, d), BF16),
            pltpu.VMEM((tl, 3 * w_a), F32),
            pltpu.VMEM((tl, 2 * w_conf), F32),
            pltpu.VMEM((SMALL_HALO + tl, xbc), F32),
            pltpu.VMEM((tl, w_ssd), F32),
            pltpu.VMEM((tl, LANES), F32),
            pltpu.VMEM((n_heads, tl), F32),
            pltpu.VMEM((SMALL_HALO + tl, w_a), F32),
            pltpu.VMEM((CONF_HALO + tl, w_conf), F32),
            pltpu.VMEM((SUBLANES - 1, tl + CONF_HALO - SUBLANES, w_conf), F32),
            pltpu.VMEM((tl, xbc), F32),
            pltpu.VMEM((tl, w_ssd), F32),
            pltpu.VMEM((tl, d), BF16),
            pltpu.VMEM((SSD_GROUPS, SSD_STATE, w_ssd // SSD_GROUPS), F32),
        ],
        compiler_params=pltpu.CompilerParams(
            dimension_semantics=("arbitrary",), vmem_limit_bytes=VMEM_LIMIT_BYTES),
        name="mixer",
    )(x2, row(norm_w), w_main, w_dt_t, conv_a_w.astype(F32), conv_ssd_w.astype(F32), row(conv_ssd_b),
      pad_lanes(dt_bias), col(dt_bias), pad_lanes(a_log), col(a_log), d_skip_x, row(ssd_norm_w),
      conv_conf_w.astype(F32), row(conv_conf_b), row(conf_ln_g), row(conf_ln_b), w_out.astype(BF16))
    return out.reshape(bsz, seq, d)


def _dense_ffn_kernel(final_norm, n_cast, x_ref, nw_ref, wg_ref, wu_ref, wd_ref, fw_ref, *rest):
    cast_in, out_ref, cast_out, hn = rest[:n_cast], rest[n_cast], rest[n_cast + 1:2 * n_cast + 1], rest[-1]
    hn[...] = _rmsnorm(x_ref[...], nw_ref[...]).astype(BF16)
    a = (_silu(_dot(hn[...], wg_ref[...])) * _dot(hn[...], wu_ref[...])).astype(BF16)
    y = x_ref[...] + _dot(a, wd_ref[...])
    out_ref[...] = _rmsnorm(y, fw_ref[...]) if final_norm else y
    for src, dst in zip(cast_in, cast_out):
        dst[...] = src[...].astype(BF16)


def _dense_ffn(x2, norm_w, w_gate, w_up, w_down, final_w, tm, later=()):
    t, d = x2.shape
    ff = w_gate.shape[1]
    assert t % tm == 0
    n_steps = t // tm
    final_norm = final_w is not None
    fw = (final_w if final_norm else jnp.ones((d,), F32)).reshape(1, d).astype(F32)
    resident = lambda shape: pl.BlockSpec(shape, lambda i: (0, 0), pipeline_mode=pl.Buffered(1))
    bf16_rows = 2 * SUBLANES
    slabs = []
    for w in later:
        rows = w.size // w.shape[-1]
        assert rows % (n_steps * bf16_rows) == 0
        slabs.append((rows // n_steps, w.shape[-1]))
    slab_specs = [pl.BlockSpec(s, lambda i: (i, 0)) for s in slabs]
    res = pl.pallas_call(
        functools.partial(_dense_ffn_kernel, final_norm, len(later)),
        out_shape=[jax.ShapeDtypeStruct((t, d), F32)]
        + [jax.ShapeDtypeStruct((n_steps * r, c), BF16) for r, c in slabs],
        grid=(n_steps,),
        in_specs=[
            pl.BlockSpec((tm, d), lambda i: (i, 0)),
            resident((1, d)), resident((d, ff)), resident((d, ff)), resident((ff, d)), resident((1, d)),
        ] + slab_specs,
        out_specs=[pl.BlockSpec((tm, d), lambda i: (i, 0))] + slab_specs,
        scratch_shapes=[pltpu.VMEM((tm, d), BF16)],
        compiler_params=pltpu.CompilerParams(
            dimension_semantics=("arbitrary",), vmem_limit_bytes=VMEM_LIMIT_BYTES),
        name="dense_ffn",
    )(x2, norm_w.reshape(1, d).astype(F32), w_gate.astype(BF16), w_up.astype(BF16), w_down.astype(BF16), fw,
      *[w.reshape(n_steps * r, c) for w, (r, c) in zip(later, slabs)])
    return res[0], [b.reshape(w.shape) for b, w in zip(res[1:], later)]


def _rows_to_tiles(tile_ref, val):
    n, d = val.shape
    n_sub = d // LANES
    for c in range(n_sub):
        tile_ref[pl.ds(c, n, stride=n_sub), :] = val[:, c * LANES:(c + 1) * LANES]


def _tiles_to_rows(tile_ref, n_sub):
    n = tile_ref.shape[0] // n_sub
    return jnp.concatenate([tile_ref[pl.ds(c, n, stride=n_sub), :] for c in range(n_sub)], axis=1)


def _expert_ffn_kernel(blk_e_ref, nact_ref, xb_ref, wg_ref, wu_ref, wd_ref, out_ref, hb, acc):
    i = pl.program_id(0)
    j = pl.program_id(1)

    n_j = pl.num_programs(1)
    active = i < nact_ref[0]

    def partial_out():
        h = hb[...]
        a = (_silu(_dot(h, wg_ref[...])) * _dot(h, wu_ref[...])).astype(BF16)
        return _dot(a, wd_ref[...])

    @pl.when(active & (j == 0))
    def _():
        hb[...] = _tiles_to_rows(xb_ref, hb.shape[1] // LANES).astype(BF16)
        acc[...] = partial_out()

    @pl.when(active & (j > 0) & (j < n_j - 1))
    def _():
        acc[...] += partial_out()

    @pl.when(active & (j == n_j - 1))
    def _():
        _rows_to_tiles(out_ref, acc[...] + partial_out())

    @pl.when((i >= nact_ref[0]) & (j == 0))
    def _():
        out_ref[...] = jnp.zeros(out_ref.shape, F32)


def _expert_ffn(buf, blk_e, nact, w_gate, w_up, w_down, bm, tf):
    d, ff = w_gate.shape[1], w_gate.shape[2]
    n_sub = d // LANES
    cap = buf.shape[0] // n_sub
    n_blocks = cap // bm
    n_j = ff // tf
    assert cap % bm == 0 and ff % tf == 0 and n_j >= 2 and w_gate.dtype == BF16
    row_blk = lambda i, j, be, na: (jnp.minimum(i, na[0] - 1), 0)
    snake = lambda i, j: jnp.where(i % 2 == 0, j, n_j - 1 - j)
    col_j = lambda i, j, na: jnp.where(i < na[0], snake(i, j), snake(na[0] - 1, n_j - 1))
    return pl.pallas_call(
        _expert_ffn_kernel,
        out_shape=jax.ShapeDtypeStruct((cap * n_sub, LANES), F32),
        grid_spec=pltpu.PrefetchScalarGridSpec(
            num_scalar_prefetch=2,
            grid=(n_blocks, n_j),
            in_specs=[
                pl.BlockSpec((bm * n_sub, LANES), row_blk),
                pl.BlockSpec((None, d, tf), lambda i, j, be, na: (be[i], 0, col_j(i, j, na))),
                pl.BlockSpec((None, d, tf), lambda i, j, be, na: (be[i], 0, col_j(i, j, na))),
                pl.BlockSpec((None, tf, d), lambda i, j, be, na: (be[i], col_j(i, j, na), 0)),
            ],
            out_specs=pl.BlockSpec((bm * n_sub, LANES), lambda i, j, be, na: (i, 0)),
            scratch_shapes=[pltpu.VMEM((bm, d), BF16), pltpu.VMEM((bm, d), F32)],
        ),
        compiler_params=pltpu.CompilerParams(
            dimension_semantics=("arbitrary", "arbitrary"), vmem_limit_bytes=VMEM_LIMIT_BYTES),
        name="expert_ffn",
    )(blk_e, nact, buf, w_gate, w_up, w_down)


RANK_BLOCK = 256


def _router_kernel(x_ref, nw_ref, wr_ref, hn_ref, route_ref, route_t_ref, count_ref):
    h = _rmsnorm(x_ref[...], nw_ref[...])
    _rows_to_tiles(hn_ref, h)
    h_hi, h_lo, _ = _split3(h)
    w_hi, w_lo, _ = _split3(wr_ref[...])
    logits = _dot(h_hi, w_hi) + _dot(h_hi, w_lo) + _dot(h_lo, w_hi)
    lane = lax.broadcasted_iota(jnp.int32, logits.shape, 1)
    neg = jnp.float32(-jnp.inf)
    logits = jnp.where(lane < N_EXPERTS, logits, neg)
    m1 = jnp.max(logits, axis=-1, keepdims=True)
    i1 = jnp.min(jnp.where(logits == m1, lane, LANES), axis=-1, keepdims=True)
    rest = jnp.where(lane == i1, neg, logits)
    m2 = jnp.max(rest, axis=-1, keepdims=True)
    i2 = jnp.min(jnp.where(rest == m2, lane, LANES), axis=-1, keepdims=True)
    e2 = jnp.exp(m2 - m1)
    g1 = 1.0 / (1.0 + e2)
    g2 = e2 * g1
    pick1, pick2 = lane == i1, lane == i2
    onehot = jnp.where(pick1 | pick2, 1.0, 0.0)
    rb = RANK_BLOCK
    ri = lax.broadcasted_iota(jnp.int32, (rb, rb), 0)
    ci = lax.broadcasted_iota(jnp.int32, (rb, rb), 1)
    before = jnp.where(ri > ci, 1.0, 0.0).astype(BF16)
    total = jnp.zeros((1, LANES), F32)
    earlier = []
    for b in range(h.shape[0] // rb):
        oh_b = onehot[b * rb:(b + 1) * rb, :]
        earlier.append(_dot(before, oh_b.astype(BF16)) + total)
        total = total + jnp.sum(oh_b, axis=0, keepdims=True)
    earlier = jnp.concatenate(earlier, axis=0)
    r1 = jnp.sum(jnp.where(pick1, earlier, 0.0), axis=-1, keepdims=True)
    r2 = jnp.sum(jnp.where(pick2, earlier, 0.0), axis=-1, keepdims=True)
    cols = (i1.astype(F32), i2.astype(F32), g1, g2, r1, r2)
    route = jnp.zeros(logits.shape, F32)
    for c, v in enumerate(cols):
        route = jnp.where(lane == c, v, route)
    route_ref[...] = route
    route_t_ref[...] = route.T[0:SUBLANES, :]
    count_ref[...] = jnp.broadcast_to(total, count_ref.shape)


def _router(x2, norm_w, w_router, tm):
    t, d = x2.shape
    n_e = w_router.shape[1]
    assert n_e == N_EXPERTS and t % tm == 0 and d % LANES == 0 and tm % RANK_BLOCK == 0
    n_sub = d // LANES
    wr = jnp.pad(w_router.astype(F32), ((0, 0), (0, LANES - n_e)))
    return pl.pallas_call(
        _router_kernel,
        out_shape=(jax.ShapeDtypeStruct((t * n_sub, LANES), F32), jax.ShapeDtypeStruct((t, LANES), F32),
                   jax.ShapeDtypeStruct((SUBLANES, t), F32),
                   jax.ShapeDtypeStruct((t // tm * SUBLANES, LANES), F32)),
        grid=(t // tm,),
        in_specs=[
            pl.BlockSpec((tm, d), lambda i: (i, 0)),
            pl.BlockSpec((1, d), lambda i: (0, 0)),
            pl.BlockSpec((d, LANES), lambda i: (0, 0)),
        ],
        out_specs=(pl.BlockSpec((tm * n_sub, LANES), lambda i: (i, 0)),
                   pl.BlockSpec((tm, LANES), lambda i: (i, 0)),
                   pl.BlockSpec((SUBLANES, tm), lambda i: (0, i)),
                   pl.BlockSpec((SUBLANES, LANES), lambda i: (i, 0))),
        compiler_params=pltpu.CompilerParams(
            dimension_semantics=("arbitrary",), vmem_limit_bytes=VMEM_LIMIT_BYTES),
        name="router",
    )(x2, norm_w.reshape(1, d).astype(F32), wr)


def _row_copy(n_sub, src_ref, src_row, dst_ref, dst_row, sem):
    return pltpu.make_async_copy(src_ref.at[pl.ds(pl.multiple_of(src_row, n_sub), n_sub)],
                                 dst_ref.at[pl.ds(pl.multiple_of(dst_row, n_sub), n_sub)], sem)


def _tile_indices(dest, tile):
    k, t = dest.shape
    return dest.reshape(k, t // tile, tile).transpose(1, 0, 2).reshape(t // tile, 1, k * tile)


def _scatter_kernel(ts, n_sub, dest_ref, pad_ref, hn_ref, buf_ref, zero_tile, sem, pad_sem):
    @pl.when(pl.program_id(0) == 0)
    def _():
        zero_tile[...] = jnp.zeros(zero_tile.shape, F32)
        n_ranges = pad_ref.shape[0] // 2
        for e in range(n_ranges):
            first, count = pad_ref[e], pad_ref[n_ranges + e]

            def fill(p, carry):
                _row_copy(n_sub, zero_tile, 0, buf_ref, (first + p) * n_sub, pad_sem).start()
                return carry

            def fill_done(p, carry):
                _row_copy(n_sub, zero_tile, 0, buf_ref, 0, pad_sem).wait()
                return carry

            lax.fori_loop(0, count, fill, 0)
            lax.fori_loop(0, count, fill_done, 0)

    def issue(rb, carry):
        for u in range(DMA_UNROLL):
            r = rb * DMA_UNROLL + u
            for k in range(TOP_K):
                _row_copy(n_sub, hn_ref, r * n_sub, buf_ref, dest_ref[0, 0, k * ts + r], sem).start(priority=k % 2)
        return carry

    lax.fori_loop(0, ts // DMA_UNROLL, issue, 0)
    for k in range(TOP_K):
        pltpu.make_async_copy(hn_ref, buf_ref.at[pl.ds(0, ts * n_sub)], sem).wait()


def _scatter_rows(hn, dest, pad_slots, cap, ts, n_sub):
    t = hn.shape[0] // n_sub
    assert t % ts == 0 and ts % DMA_UNROLL == 0
    return pl.pallas_call(
        functools.partial(_scatter_kernel, ts, n_sub),
        out_shape=jax.ShapeDtypeStruct((cap * n_sub, LANES), F32),
        grid=(t // ts,),
        in_specs=[
            pl.BlockSpec((1, 1, TOP_K * ts), lambda i: (i, 0, 0), memory_space=pltpu.SMEM),
            pl.BlockSpec(memory_space=pltpu.SMEM),
            pl.BlockSpec((ts * n_sub, LANES), lambda i: (i, 0)),
        ],
        out_specs=pl.BlockSpec(memory_space=pl.ANY),
        scratch_shapes=[pltpu.VMEM((n_sub, LANES), F32), pltpu.SemaphoreType.DMA(()), pltpu.SemaphoreType.DMA(())],
        compiler_params=pltpu.CompilerParams(
            dimension_semantics=("arbitrary",), vmem_limit_bytes=VMEM_LIMIT_BYTES),
        name="scatter_rows",
    )(_tile_indices(dest, ts), pad_slots, hn)


def _combine_kernel(tc, final_norm, dest_ref, dest_next_ref, x_ref, route_ref, fw_ref, ybuf_ref, out_ref, rows, sems):
    n_sub = x_ref.shape[1] // LANES
    i = pl.program_id(0)
    slot = i % 2

    def gather(idx_ref, s):
        def issue(rb, carry):
            for u in range(DMA_UNROLL):
                r = rb * DMA_UNROLL + u
                for k in range(TOP_K):
                    _row_copy(n_sub, ybuf_ref, idx_ref[0, 0, k * tc + r], rows.at[s, k], r * n_sub,
                              sems.at[s]).start(priority=k % 2)
            return carry

        lax.fori_loop(0, tc // DMA_UNROLL, issue, 0)

    @pl.when(i == 0)
    def _():
        gather(dest_ref, 0)

    @pl.when(i + 1 < pl.num_programs(0))
    def _():
        gather(dest_next_ref, 1 - slot)

    for k in range(TOP_K):
        pltpu.make_async_copy(ybuf_ref.at[pl.ds(0, tc * n_sub)], rows.at[slot, k], sems.at[slot]).wait()

    route = route_ref[...]
    y = x_ref[...]
    for k in range(TOP_K):
        y = y + route[:, TOP_K + k:TOP_K + k + 1] * _tiles_to_rows(rows.at[slot, k], n_sub)
    out_ref[...] = _rmsnorm(y, fw_ref[...]) if final_norm else y


def _combine(x2, route, dest, ybuf, final_w, tc):
    t, d = x2.shape
    assert t % tc == 0 and tc % DMA_UNROLL == 0
    n_tiles = t // tc
    final_norm = final_w is not None
    fw = (final_w if final_norm else jnp.ones((d,), F32)).reshape(1, d).astype(F32)
    idx = _tile_indices(dest, tc)
    return pl.pallas_call(
        functools.partial(_combine_kernel, tc, final_norm),
        out_shape=jax.ShapeDtypeStruct((t, d), F32),
        grid=(n_tiles,),
        in_specs=[
            pl.BlockSpec((1, 1, TOP_K * tc), lambda i: (i, 0, 0), memory_space=pltpu.SMEM),
            pl.BlockSpec((1, 1, TOP_K * tc), lambda i: (jnp.minimum(i + 1, n_tiles - 1), 0, 0), memory_space=pltpu.SMEM),
            pl.BlockSpec((tc, d), lambda i: (i, 0)),
            pl.BlockSpec((tc, LANES), lambda i: (i, 0)),
            pl.BlockSpec((1, d), lambda i: (0, 0)),
            pl.BlockSpec(memory_space=pl.ANY),
        ],
        out_specs=pl.BlockSpec((tc, d), lambda i: (i, 0)),
        scratch_shapes=[pltpu.VMEM((2, TOP_K, tc * (d // LANES), LANES), F32), pltpu.SemaphoreType.DMA((2,))],
        compiler_params=pltpu.CompilerParams(
            dimension_semantics=("arbitrary",), vmem_limit_bytes=VMEM_LIMIT_BYTES),
        name="combine",
    )(idx, idx, x2, route, fw, ybuf)


def _moe_layer(x2, norm_w, w_router, w_gate, w_up, w_down, final_w, tiles):
    t, d = x2.shape
    bm = tiles["moe_bm"]
    tm = tiles["router_tm"]
    hn, route, route_t, counts = _router(x2, norm_w, w_router, tm)
    counts = counts[::SUBLANES, :N_EXPERTS].astype(jnp.int32)
    sizes = jnp.sum(counts, axis=0)
    padded = ((sizes + bm - 1) // bm) * bm
    ends = jnp.cumsum(padded)
    starts = ends - padded
    tile_base = starts[None, :] + jnp.cumsum(counts, axis=0) - counts
    e_idx = route_t[0:TOP_K].astype(jnp.int32).reshape(TOP_K, t // tm, tm)
    rank = route_t[2 * TOP_K:3 * TOP_K].astype(jnp.int32).reshape(TOP_K, t // tm, tm)
    base = jnp.zeros_like(rank)
    for e in range(N_EXPERTS):
        base = jnp.where(e_idx == e, tile_base[None, :, e:e + 1], base)
    dest = (base + rank).reshape(TOP_K, t)
    cap = TOP_K * t + N_EXPERTS * bm
    n_blocks = cap // bm
    first_slot = jnp.arange(n_blocks, dtype=jnp.int32) * bm
    blk_e = jnp.minimum(jnp.sum(first_slot[:, None] >= ends[None, :], axis=1), N_EXPERTS - 1).astype(jnp.int32)
    nact = (ends[-1:] // bm).astype(jnp.int32)

    n_sub = d // LANES
    dest_rows = dest * n_sub
    pad_slots = jnp.concatenate([starts + sizes, ends[-1:], padded - sizes, cap - ends[-1:]]).astype(jnp.int32)
    buf = _scatter_rows(hn, dest_rows, pad_slots, cap, tiles["scatter_ts"], n_sub)
    ybuf = _expert_ffn(buf, blk_e, nact, w_gate, w_up, w_down, bm, tiles["moe_tf"])
    return _combine(x2, route, dest_rows, ybuf, final_w, tiles["combine_tc"])


TILES = {
    "mixer_tl": 512,
    "ffn_tm": 512,
    "router_tm": 1024,
    "moe_bm": 512, "moe_tf": 1792,
    "scatter_ts": 1024,
    "combine_tc": 512,
}


def kernel(x, norm_mix, w_in, conv_a_w, conv_ssd_w, conv_ssd_b, dt_bias, a_log, d_skip, ssd_norm_w, conv_conf_w, conv_conf_b, conf_ln_g, conf_ln_b, w_out, norm_ffn, ffn_w_gate, ffn_w_up, ffn_w_down, moe_router, moe_w_gate, moe_w_up, moe_w_down, norm_final):
    bsz, seq, d = x.shape
    depth = norm_mix.shape[0]
    tiles = dict(TILES)
    tiles["mixer_tl"] = min(tiles["mixer_tl"], seq)
    expert_w = None
    for i in range(depth):
        x = _mixer_layer(x, norm_mix[i], w_in[i], conv_a_w[i], conv_ssd_w[i], conv_ssd_b[i], dt_bias[i], a_log[i],
                         d_skip[i], ssd_norm_w[i], conv_conf_w[i], conv_conf_b[i], conf_ln_g[i], conf_ln_b[i],
                         w_out[i], tiles["mixer_tl"])
        x2 = x.reshape(bsz * seq, d)
        final_w = norm_final if i == depth - 1 else None
        j = i // 2
        if i % 2 == 0:
            later = (moe_w_gate[j], moe_w_up[j], moe_w_down[j]) if i + 1 < depth else ()
            x2, expert_w = _dense_ffn(x2, norm_ffn[i], ffn_w_gate[j], ffn_w_up[j], ffn_w_down[j], final_w,
                                      tiles["ffn_tm"], later)
        else:
            if not expert_w:
                expert_w = [w.astype(BF16) for w in (moe_w_gate[j], moe_w_up[j], moe_w_down[j])]
            x2 = _moe_layer(x2, norm_ffn[i], moe_router[j], *expert_w, final_w, tiles)
            expert_w = None
        x = x2.reshape(bsz, seq, d)
    return x
```

```python
import functools

import jax
import jax.numpy as jnp
from jax import lax
from jax.experimental import pallas as pl
from jax.experimental.pallas import tpu as pltpu

F32 = jnp.float32
BF16 = jnp.bfloat16
EPS = 1e-5

LANES = 128
SUBLANES = 8
VMEM_LIMIT_BYTES = 56 * 1024 * 1024

GROUP_DIM = 64
A_CONV = 3
SSD_HEAD_DIM = 64
SSD_GROUPS = 2
SSD_STATE = 128
SSD_CONV = 4
CONF_KERNEL = 31
N_EXPERTS = 8
TOP_K = 2

SSD_CHUNK = 128
CONF_HALO = 32
SMALL_HALO = SUBLANES
CONV_ROWS = 32
ROW_BLOCK = 64
LANE_BLOCK = 256
DMA_UNROLL = 16


def _split3(v):
    hi = v.astype(BF16)
    r1 = v - hi.astype(F32)
    mid = r1.astype(BF16)
    lo = (r1 - mid.astype(F32)).astype(BF16)
    return hi, mid, lo


def _dot(a, b):
    return jnp.dot(a, b, preferred_element_type=F32)


def _dot_nt(a, b):
    return lax.dot_general(a, b, (((1,), (1,)), ((), ())), preferred_element_type=F32)


def _dot_sel_right(v, sel):
    hi, mid, lo = _split3(v)
    return _dot(hi, sel) + _dot(mid, sel) + _dot(lo, sel)


def _dot_sel_left(sel, v):
    hi, mid, lo = _split3(v)
    return _dot(sel, hi) + _dot(sel, mid) + _dot(sel, lo)


def _softplus(v):
    return jnp.maximum(v, 0.0) + jnp.log1p(jnp.exp(-jnp.abs(v)))


def _silu(v):
    return v * jax.nn.sigmoid(v)


def _tile_rows(w8, rows):
    return jnp.tile(w8, (rows // SUBLANES, 1))


def _rmsnorm(x, w):
    return x * lax.rsqrt(jnp.mean(x * x, axis=-1, keepdims=True) + EPS) * w


def _mixer_kernel(dims, x_ref, nw_ref, win_ref, wdtT_ref, caw_ref, csw_ref, csb_ref, dtbc_ref, dtbr_ref,
                  alc_ref, alr_ref, dsk_ref, snw_ref, ccw_ref, ccb_ref, lng_ref, lnb_ref, wout_ref,
                  out_ref,
                  hn_s, p_a, p_c, p_s, p_z, p_dt, dtr, buf_a, buf_c, shc, xact, ysd, ycat, hstate):
    tl, w_a, w_ssd, w_conf, n_heads, n_l = dims
    q = SSD_CHUNK
    n_bc = SSD_GROUPS * SSD_STATE
    xbc = w_ssd + 2 * n_bc
    hpg = n_heads // SSD_GROUPS
    gw = hpg * SSD_HEAD_DIM
    hb = SMALL_HALO
    o_z = 3 * w_a
    o_xbc = o_z + w_ssd
    o_c = o_xbc + xbc
    o_dt = o_c + 2 * w_conf
    lt = lax.rem(pl.program_id(0), n_l)

    @pl.when(lt == 0)
    def _():
        buf_a[0:hb, :] = jnp.zeros((hb, w_a), F32)
        p_s[0:hb, :] = jnp.zeros((hb, xbc), F32)
        buf_c[0:CONF_HALO, :] = jnp.zeros((CONF_HALO, w_conf), F32)
        hstate[...] = jnp.zeros(hstate.shape, F32)

    @pl.when(lt > 0)
    def _():
        buf_a[0:hb, :] = buf_a[tl:tl + hb, :]
        p_s[0:hb, :] = p_s[tl:tl + hb, :]
        buf_c[0:CONF_HALO, :] = buf_c[tl:tl + CONF_HALO, :]

    hn_s[...] = _rmsnorm(x_ref[...], nw_ref[...]).astype(BF16)

    p_s[hb:hb + tl, :] = _dot(hn_s[...], win_ref[:, o_xbc:o_xbc + xbc])
    for rb in range(tl // ROW_BLOCK):
        for lb in range(xbc // LANE_BLOCK):
            cols = slice(lb * LANE_BLOCK, (lb + 1) * LANE_BLOCK)
            xc = jnp.zeros((ROW_BLOCK, LANE_BLOCK), F32) + csb_ref[:, cols]
            for j in range(SSD_CONV):
                s = hb + rb * ROW_BLOCK - (SSD_CONV - 1) + j
                xc = xc + _tile_rows(csw_ref[j, :, cols], ROW_BLOCK) * p_s[s:s + ROW_BLOCK, cols]
            xact[rb * ROW_BLOCK:(rb + 1) * ROW_BLOCK, cols] = _silu(xc)
    p_z[...] = _dot(hn_s[...], win_ref[:, o_z:o_z + w_ssd])
    p_dt[...] = _dot(hn_s[...], win_ref[:, o_dt:o_dt + LANES])
    dtr[...] = _dot_nt(wdtT_ref[...], hn_s[...])

    p_c[...] = _dot(hn_s[...], win_ref[:, o_c:o_c + 2 * w_conf])
    for rb in range(tl // ROW_BLOCK):
        r0 = rb * ROW_BLOCK
        buf_c[CONF_HALO + r0:CONF_HALO + r0 + ROW_BLOCK, :] = (
            p_c[r0:r0 + ROW_BLOCK, 0:w_conf] * jax.nn.sigmoid(p_c[r0:r0 + ROW_BLOCK, w_conf:2 * w_conf]))
    n_sh = shc.shape[1]
    for r in range(1, SUBLANES):
        shc[r - 1] = buf_c[r:r + n_sh, :]
    for rb in range(tl // CONV_ROWS):
        acc = jnp.zeros((CONV_ROWS, w_conf), F32) + ccb_ref[...]
        for j in range(CONF_KERNEL):
            a, r = divmod(CONF_HALO - (CONF_KERNEL - 1) + j, SUBLANES)
            base = a * SUBLANES + rb * CONV_ROWS
            src = buf_c[base:base + CONV_ROWS, :] if r == 0 else shc[r - 1, base:base + CONV_ROWS, :]
            acc = acc + _tile_rows(ccw_ref[j], CONV_ROWS) * src
        mu = jnp.mean(acc, axis=-1, keepdims=True)
        cen = acc - mu
        var = jnp.mean(cen * cen, axis=-1, keepdims=True)
        yc = cen * lax.rsqrt(var + EPS) * lng_ref[...] + lnb_ref[...]
        ycat[rb * CONV_ROWS:(rb + 1) * CONV_ROWS, w_a + w_ssd:w_a + w_ssd + w_conf] = _silu(yc).astype(BF16)

    p_a[...] = _dot(hn_s[...], win_ref[:, 0:3 * w_a])
    for rb in range(tl // ROW_BLOCK):
        r0 = rb * ROW_BLOCK
        b0 = hb + r0
        buf_a[b0:b0 + ROW_BLOCK, :] = p_a[r0:r0 + ROW_BLOCK, w_a:2 * w_a] * p_a[r0:r0 + ROW_BLOCK, 2 * w_a:3 * w_a]
        ya = jnp.zeros((ROW_BLOCK, w_a), F32)
        for j in range(A_CONV):
            s = b0 - (A_CONV - 1) + j
            ya = ya + _tile_rows(caw_ref[j], ROW_BLOCK) * buf_a[s:s + ROW_BLOCK, :]
        ycat[r0:r0 + ROW_BLOCK, 0:w_a] = (p_a[r0:r0 + ROW_BLOCK, 0:w_a] * ya).astype(BF16)

    a_col = -jnp.exp(alc_ref[...])
    a_row = -jnp.exp(alr_ref[...])
    ri = lax.broadcasted_iota(jnp.int32, (q, q), 0)
    ci = lax.broadcasted_iota(jnp.int32, (q, q), 1)
    causal = ri >= ci
    tril = jnp.where(causal, 1.0, 0.0).astype(BF16)
    triu = jnp.where(ri <= ci, 1.0, 0.0).astype(BF16)
    hd = SSD_HEAD_DIM

    for c in range(tl // q):
        rows = slice(c * q, (c + 1) * q)
        dt_c = _softplus(p_dt[rows, :] + dtbc_ref[...])
        dt_r = _softplus(dtr[:, rows] + dtbr_ref[...])
        acum_c = _dot_sel_left(tril, dt_c * a_col)
        acum_r = _dot_sel_right(dt_r * a_row, triu)
        to_end_r = dt_r * jnp.exp(acum_r[:, q - 1:q] - acum_r)
        for g in range(SSD_GROUPS):
            bg = xact[rows, w_ssd + g * SSD_STATE:w_ssd + (g + 1) * SSD_STATE]
            cg = xact[rows, w_ssd + n_bc + g * SSD_STATE:w_ssd + n_bc + (g + 1) * SSD_STATE].astype(BF16)
            cb = _dot_nt(cg, bg.astype(BF16))
            bgt = bg.T
            hg = hstate[g]
            yoff = _dot(cg, hg.astype(BF16))
            for r in range(hpg):
                h = g * hpg + r
                xh = xact[rows, h * hd:(h + 1) * hd].astype(BF16)
                acol = jnp.broadcast_to(acum_c[:, h:h + 1], (q, q))
                dec = jnp.where(causal, jnp.exp(acol - acum_r[h:h + 1, :]), 0.0)
                yd = _dot((cb * dec * dt_r[h:h + 1, :]).astype(BF16), xh)
                ysd[rows, h * hd:(h + 1) * hd] = yd + yoff[:, r * hd:(r + 1) * hd] * jnp.exp(acol[:, 0:hd])
                st = _dot((bgt * to_end_r[h:h + 1, :]).astype(BF16), xh)
                hstate[g, :, r * hd:(r + 1) * hd] = jnp.exp(acol[q - 1:q, 0:hd]) * hg[:, r * hd:(r + 1) * hd] + st
        xs = xact[rows, 0:w_ssd]
        yz = (ysd[rows, :] + dsk_ref[...] * xs) * _silu(p_z[rows, :])
        for g in range(SSD_GROUPS):
            yg = yz[:, g * gw:(g + 1) * gw]
            yn = yg * lax.rsqrt(jnp.mean(yg * yg, axis=-1, keepdims=True) + EPS) * snw_ref[:, g * gw:(g + 1) * gw]
            ycat[rows, w_a + g * gw:w_a + (g + 1) * gw] = yn.astype(BF16)

    out_ref[...] = x_ref[...] + _dot(ycat[...], wout_ref[...])


def _mixer_layer(x, norm_w, w_in, conv_a_w, conv_ssd_w, conv_ssd_b, dt_bias, a_log, d_skip, ssd_norm_w,
                 conv_conf_w, conv_conf_b, conf_ln_g, conf_ln_b, w_out, tl):
    bsz, seq, d = x.shape
    w_a = conv_a_w.shape[1]
    w_ssd = ssd_norm_w.shape[0]
    w_conf = conv_conf_w.shape[1]
    n_heads = dt_bias.shape[0]
    n_bc = SSD_GROUPS * SSD_STATE
    xbc = w_ssd + 2 * n_bc
    assert n_heads == SUBLANES and n_heads * SSD_HEAD_DIM == w_ssd and seq % tl == 0 and tl % SSD_CHUNK == 0
    assert tl % ROW_BLOCK == 0 and xbc % LANE_BLOCK == 0 and conv_a_w.shape[0] == A_CONV and conv_ssd_w.shape[0] == SSD_CONV
    assert conv_conf_w.shape[0] == CONF_KERNEL
    o_dt = 3 * w_a + w_ssd + xbc
    w_dt = w_in[:, o_dt:o_dt + n_heads]
    w_main = jnp.concatenate(
        [w_in[:, :o_dt], w_in[:, o_dt + n_heads:], jnp.pad(w_dt, ((0, 0), (0, LANES - n_heads)))], axis=1).astype(BF16)
    n_cols = w_main.shape[1]
    w_dt_t = w_dt.T.astype(BF16)
    row = lambda v: v.reshape(1, -1).astype(F32)
    pad_lanes = lambda v: jnp.pad(v.reshape(1, -1).astype(F32), ((0, 0), (0, LANES - v.shape[0])))
    col = lambda v: v.reshape(-1, 1).astype(F32)
    taps = lambda w: jnp.broadcast_to(w.astype(F32)[:, None, :], (w.shape[0], SUBLANES, w.shape[1]))
    d_skip_x = jnp.repeat(d_skip.astype(F32), SSD_HEAD_DIM).reshape(1, w_ssd)

    const = lambda shape: pl.BlockSpec(shape, lambda s: (0,) * len(shape))
    n_l = seq // tl
    n_tiles = bsz * n_l
    dims = (tl, w_a, w_ssd, w_conf, n_heads, n_l)
    x2 = x.reshape(bsz * seq, d)
    out = pl.pallas_call(
        functools.partial(_mixer_kernel, dims),
        out_shape=jax.ShapeDtypeStruct((bsz * seq, d), F32),
        grid=(n_tiles,),
        in_specs=[
            pl.BlockSpec((tl, d), lambda s: (s, 0)),
            const((1, d)), const((d, n_cols)), const((n_heads, d)),
            const((A_CONV, SUBLANES, w_a)), const((SSD_CONV, SUBLANES, xbc)), const((1, xbc)),
            const((1, LANES)), const((n_heads, 1)), const((1, LANES)), const((n_heads, 1)),
            const((1, w_ssd)), const((1, w_ssd)),
            const((CONF_KERNEL, SUBLANES, w_conf)), const((1, w_conf)), const((1, w_conf)), const((1, w_conf)),
            const((d, d)),
        ],
        out_specs=pl.BlockSpec((tl, d), lambda s: (s, 0)),
        scratch_shapes=[
            pltpu.VMEM((tl, d), BF16),
            pltpu.VMEM((tl, 3 * w_a), F32),
            pltpu.VMEM((tl, 2 * w_conf), F32),
            pltpu.VMEM((SMALL_HALO + tl, xbc), F32),
            pltpu.VMEM((tl, w_ssd), F32),
            pltpu.VMEM((tl, LANES), F32),
            pltpu.VMEM((n_heads, tl), F32),
            pltpu.VMEM((SMALL_HALO + tl, w_a), F32),
            pltpu.VMEM((CONF_HALO + tl, w_conf), F32),
            pltpu.VMEM((SUBLANES - 1, tl + CONF_HALO - SUBLANES, w_conf), F32),
            pltpu.VMEM((tl, xbc), F32),
            pltpu.VMEM((tl, w_ssd), F32),
            pltpu.VMEM((tl, d), BF16),
            pltpu.VMEM((SSD_GROUPS, SSD_STATE, w_ssd // SSD_GROUPS), F32),
        ],
        compiler_params=pltpu.CompilerParams(
            dimension_semantics=("arbitrary",), vmem_limit_bytes=VMEM_LIMIT_BYTES),
        name="mixer",
    )(x2, row(norm_w), w_main, w_dt_t, taps(conv_a_w), taps(conv_ssd_w), row(conv_ssd_b),
      pad_lanes(dt_bias), col(dt_bias), pad_lanes(a_log), col(a_log), d_skip_x, row(ssd_norm_w),
      taps(conv_conf_w), row(conv_conf_b), row(conf_ln_g), row(conf_ln_b), w_out.astype(BF16))
    return out.reshape(bsz, seq, d)


def _dense_ffn_kernel(final_norm, n_cast, x_ref, nw_ref, wg_ref, wu_ref, wd_ref, fw_ref, *rest):
    cast_in, out_ref, cast_out, hn = rest[:n_cast], rest[n_cast], rest[n_cast + 1:2 * n_cast + 1], rest[-1]
    hn[...] = _rmsnorm(x_ref[...], nw_ref[...]).astype(BF16)
    a = (_silu(_dot(hn[...], wg_ref[...])) * _dot(hn[...], wu_ref[...])).astype(BF16)
    y = x_ref[...] + _dot(a, wd_ref[...])
    out_ref[...] = _rmsnorm(y, fw_ref[...]) if final_norm else y
    for src, dst in zip(cast_in, cast_out):
        dst[...] = src[...].astype(BF16)


def _dense_ffn(x2, norm_w, w_gate, w_up, w_down, final_w, tm, later=()):
    t, d = x2.shape
    ff = w_gate.shape[1]
    assert t % tm == 0
    n_steps = t // tm
    final_norm = final_w is not None
    fw = (final_w if final_norm else jnp.ones((d,), F32)).reshape(1, d).astype(F32)
    resident = lambda shape: pl.BlockSpec(shape, lambda i: (0, 0), pipeline_mode=pl.Buffered(1))
    bf16_rows = 2 * SUBLANES
    slabs = []
    for w in later:
        rows = w.size // w.shape[-1]
        assert rows % (n_steps * bf16_rows) == 0
        slabs.append((rows // n_steps, w.shape[-1]))
    slab_specs = [pl.BlockSpec(s, lambda i: (i, 0)) for s in slabs]
    res = pl.pallas_call(
        functools.partial(_dense_ffn_kernel, final_norm, len(later)),
        out_shape=[jax.ShapeDtypeStruct((t, d), F32)]
        + [jax.ShapeDtypeStruct((n_steps * r, c), BF16) for r, c in slabs],
        grid=(n_steps,),
        in_specs=[
            pl.BlockSpec((tm, d), lambda i: (i, 0)),
            resident((1, d)), resident((d, ff)), resident((d, ff)), resident((ff, d)), resident((1, d)),
        ] + slab_specs,
        out_specs=[pl.BlockSpec((tm, d), lambda i: (i, 0))] + slab_specs,
        scratch_shapes=[pltpu.VMEM((tm, d), BF16)],
        compiler_params=pltpu.CompilerParams(
            dimension_semantics=("arbitrary",), vmem_limit_bytes=VMEM_LIMIT_BYTES),
        name="dense_ffn",
    )(x2, norm_w.reshape(1, d).astype(F32), w_gate.astype(BF16), w_up.astype(BF16), w_down.astype(BF16), fw,
      *[w.reshape(n_steps * r, c) for w, (r, c) in zip(later, slabs)])
    return res[0], [b.reshape(w.shape) for b, w in zip(res[1:], later)]


def _rows_to_tiles(tile_ref, val):
    n, d = val.shape
    n_sub = d // LANES
    for c in range(n_sub):
        tile_ref[pl.ds(c, n, stride=n_sub), :] = val[:, c * LANES:(c + 1) * LANES]


def _tiles_to_rows(tile_ref, n_sub):
    n = tile_ref.shape[0] // n_sub
    return jnp.concatenate([tile_ref[pl.ds(c, n, stride=n_sub), :] for c in range(n_sub)], axis=1)


def _expert_ffn_kernel(blk_e_ref, nact_ref, xb_ref, wg_ref, wu_ref, wd_ref, out_ref, hb, acc):
    i = pl.program_id(0)
    j = pl.program_id(1)

    n_j = pl.num_programs(1)
    active = i < nact_ref[0]

    def partial_out():
        h = hb[...]
        a = (_silu(_dot(h, wg_ref[...])) * _dot(h, wu_ref[...])).astype(BF16)
        return _dot(a, wd_ref[...])

    @pl.when(active & (j == 0))
    def _():
        hb[...] = _tiles_to_rows(xb_ref, hb.shape[1] // LANES).astype(BF16)
        acc[...] = partial_out()

    @pl.when(active & (j > 0) & (j < n_j - 1))
    def _():
        acc[...] += partial_out()

    @pl.when(active & (j == n_j - 1))
    def _():
        _rows_to_tiles(out_ref, acc[...] + partial_out())

    @pl.when((i >= nact_ref[0]) & (j == 0))
    def _():
        out_ref[...] = jnp.zeros(out_ref.shape, F32)


def _expert_ffn(buf, blk_e, nact, w_gate, w_up, w_down, bm, tf):
    d, ff = w_gate.shape[1], w_gate.shape[2]
    n_sub = d // LANES
    cap = buf.shape[0] // n_sub
    n_blocks = cap // bm
    n_j = ff // tf
    assert cap % bm == 0 and ff % tf == 0 and n_j >= 2 and w_gate.dtype == BF16
    row_blk = lambda i, j, be, na: (jnp.minimum(i, na[0] - 1), 0)
    snake = lambda i, j: jnp.where(i % 2 == 0, j, n_j - 1 - j)
    col_j = lambda i, j, na: jnp.where(i < na[0], snake(i, j), snake(na[0] - 1, n_j - 1))
    return pl.pallas_call(
        _expert_ffn_kernel,
        out_shape=jax.ShapeDtypeStruct((cap * n_sub, LANES), F32),
        grid_spec=pltpu.PrefetchScalarGridSpec(
            num_scalar_prefetch=2,
            grid=(n_blocks, n_j),
            in_specs=[
                pl.BlockSpec((bm * n_sub, LANES), row_blk),
                pl.BlockSpec((None, d, tf), lambda i, j, be, na: (be[i], 0, col_j(i, j, na))),
                pl.BlockSpec((None, d, tf), lambda i, j, be, na: (be[i], 0, col_j(i, j, na))),
                pl.BlockSpec((None, tf, d), lambda i, j, be, na: (be[i], col_j(i, j, na), 0)),
            ],
            out_specs=pl.BlockSpec((bm * n_sub, LANES), lambda i, j, be, na: (i, 0)),
            scratch_shapes=[pltpu.VMEM((bm, d), BF16), pltpu.VMEM((bm, d), F32)],
        ),
        compiler_params=pltpu.CompilerParams(
            dimension_semantics=("arbitrary", "arbitrary"), vmem_limit_bytes=VMEM_LIMIT_BYTES),
        name="expert_ffn",
    )(blk_e, nact, buf, w_gate, w_up, w_down)


RANK_BLOCK = 256


def _router_kernel(x_ref, nw_ref, wr_ref, hn_ref, route_ref, route_t_ref, count_ref):
    h = _rmsnorm(x_ref[...], nw_ref[...])
    _rows_to_tiles(hn_ref, h)
    h_hi, h_lo, _ = _split3(h)
    w_hi, w_lo, _ = _split3(wr_ref[...])
    logits = _dot(h_hi, w_hi) + _dot(h_hi, w_lo) + _dot(h_lo, w_hi)
    lane = lax.broadcasted_iota(jnp.int32, logits.shape, 1)
    neg = jnp.float32(-jnp.inf)
    logits = jnp.where(lane < N_EXPERTS, logits, neg)
    m1 = jnp.max(logits, axis=-1, keepdims=True)
    i1 = jnp.min(jnp.where(logits == m1, lane, LANES), axis=-1, keepdims=True)
    rest = jnp.where(lane == i1, neg, logits)
    m2 = jnp.max(rest, axis=-1, keepdims=True)
    i2 = jnp.min(jnp.where(rest == m2, lane, LANES), axis=-1, keepdims=True)
    e2 = jnp.exp(m2 - m1)
    g1 = 1.0 / (1.0 + e2)
    g2 = e2 * g1
    pick1, pick2 = lane == i1, lane == i2
    onehot = jnp.where(pick1 | pick2, 1.0, 0.0)
    rb = RANK_BLOCK
    ri = lax.broadcasted_iota(jnp.int32, (rb, rb), 0)
    ci = lax.broadcasted_iota(jnp.int32, (rb, rb), 1)
    before = jnp.where(ri > ci, 1.0, 0.0).astype(BF16)
    total = jnp.zeros((1, LANES), F32)
    earlier = []
    for b in range(h.shape[0] // rb):
        oh_b = onehot[b * rb:(b + 1) * rb, :]
        earlier.append(_dot(before, oh_b.astype(BF16)) + total)
        total = total + jnp.sum(oh_b, axis=0, keepdims=True)
    earlier = jnp.concatenate(earlier, axis=0)
    r1 = jnp.sum(jnp.where(pick1, earlier, 0.0), axis=-1, keepdims=True)
    r2 = jnp.sum(jnp.where(pick2, earlier, 0.0), axis=-1, keepdims=True)
    cols = (i1.astype(F32), i2.astype(F32), g1, g2, r1, r2)
    route = jnp.zeros(logits.shape, F32)
    for c, v in enumerate(cols):
        route = jnp.where(lane == c, v, route)
    route_ref[...] = route
    route_t_ref[...] = route.T[0:SUBLANES, :]
    count_ref[...] = jnp.broadcast_to(total, count_ref.shape)


def _router(x2, norm_w, w_router, tm):
    t, d = x2.shape
    n_e = w_router.shape[1]
    assert n_e == N_EXPERTS and t % tm == 0 and d % LANES == 0 and tm % RANK_BLOCK == 0
    n_sub = d // LANES
    wr = jnp.pad(w_router.astype(F32), ((0, 0), (0, LANES - n_e)))
    return pl.pallas_call(
        _router_kernel,
        out_shape=(jax.ShapeDtypeStruct((t * n_sub, LANES), F32), jax.ShapeDtypeStruct((t, LANES), F32),
                   jax.ShapeDtypeStruct((SUBLANES, t), F32),
                   jax.ShapeDtypeStruct((t // tm * SUBLANES, LANES), F32)),
        grid=(t // tm,),
        in_specs=[
            pl.BlockSpec((tm, d), lambda i: (i, 0)),
            pl.BlockSpec((1, d), lambda i: (0, 0)),
            pl.BlockSpec((d, LANES), lambda i: (0, 0)),
        ],
        out_specs=(pl.BlockSpec((tm * n_sub, LANES), lambda i: (i, 0)),
                   pl.BlockSpec((tm, LANES), lambda i: (i, 0)),
                   pl.BlockSpec((SUBLANES, tm), lambda i: (0, i)),
                   pl.BlockSpec((SUBLANES, LANES), lambda i: (i, 0))),
        compiler_params=pltpu.CompilerParams(
            dimension_semantics=("arbitrary",), vmem_limit_bytes=VMEM_LIMIT_BYTES),
        name="router",
    )(x2, norm_w.reshape(1, d).astype(F32), wr)


def _row_copy(n_sub, src_ref, src_row, dst_ref, dst_row, sem):
    return pltpu.make_async_copy(src_ref.at[pl.ds(pl.multiple_of(src_row, n_sub), n_sub)],
                                 dst_ref.at[pl.ds(pl.multiple_of(dst_row, n_sub), n_sub)], sem)


def _tile_indices(dest, tile):
    k, t = dest.shape
    return dest.reshape(k, t // tile, tile).transpose(1, 0, 2).reshape(t // tile, 1, k * tile)


def _scatter_kernel(ts, n_sub, dest_ref, pad_ref, hn_ref, buf_ref, zero_tile, sem, pad_sem):
    @pl.when(pl.program_id(0) == 0)
    def _():
        zero_tile[...] = jnp.zeros(zero_tile.shape, F32)
        n_ranges = pad_ref.shape[0] // 2
        for e in range(n_ranges):
            first, count = pad_ref[e], pad_ref[n_ranges + e]

            def fill(p, carry):
                _row_copy(n_sub, zero_tile, 0, buf_ref, (first + p) * n_sub, pad_sem).start()
                return carry

            def fill_done(p, carry):
                _row_copy(n_sub, zero_tile, 0, buf_ref, 0, pad_sem).wait()
                return carry

            lax.fori_loop(0, count, fill, 0)
            lax.fori_loop(0, count, fill_done, 0)

    def issue(rb, carry):
        for u in range(DMA_UNROLL):
            r = rb * DMA_UNROLL + u
            for k in range(TOP_K):
                _row_copy(n_sub, hn_ref, r * n_sub, buf_ref, dest_ref[0, 0, k * ts + r], sem).start(priority=k % 2)
        return carry

    lax.fori_loop(0, ts // DMA_UNROLL, issue, 0)
    for k in range(TOP_K):
        pltpu.make_async_copy(hn_ref, buf_ref.at[pl.ds(0, ts * n_sub)], sem).wait()


def _scatter_rows(hn, dest, pad_slots, cap, ts, n_sub):
    t = hn.shape[0] // n_sub
    assert t % ts == 0 and ts % DMA_UNROLL == 0
    return pl.pallas_call(
        functools.partial(_scatter_kernel, ts, n_sub),
        out_shape=jax.ShapeDtypeStruct((cap * n_sub, LANES), F32),
        grid=(t // ts,),
        in_specs=[
            pl.BlockSpec((1, 1, TOP_K * ts), lambda i: (i, 0, 0), memory_space=pltpu.SMEM),
            pl.BlockSpec(memory_space=pltpu.SMEM),
            pl.BlockSpec((ts * n_sub, LANES), lambda i: (i, 0)),
        ],
        out_specs=pl.BlockSpec(memory_space=pl.ANY),
        scratch_shapes=[pltpu.VMEM((n_sub, LANES), F32), pltpu.SemaphoreType.DMA(()), pltpu.SemaphoreType.DMA(())],
        compiler_params=pltpu.CompilerParams(
            dimension_semantics=("arbitrary",), vmem_limit_bytes=VMEM_LIMIT_BYTES),
        name="scatter_rows",
    )(_tile_indices(dest, ts), pad_slots, hn)


def _combine_kernel(tc, final_norm, dest_ref, dest_next_ref, x_ref, route_ref, fw_ref, ybuf_ref, out_ref, rows, sems):
    n_sub = x_ref.shape[1] // LANES
    i = pl.program_id(0)
    slot = i % 2

    def gather(idx_ref, s):
        def issue(rb, carry):
            for u in range(DMA_UNROLL):
                r = rb * DMA_UNROLL + u
                for k in range(TOP_K):
                    _row_copy(n_sub, ybuf_ref, idx_ref[0, 0, k * tc + r], rows.at[s, k], r * n_sub,
                              sems.at[s]).start(priority=k % 2)
            return carry

        lax.fori_loop(0, tc // DMA_UNROLL, issue, 0)

    @pl.when(i == 0)
    def _():
        gather(dest_ref, 0)

    @pl.when(i + 1 < pl.num_programs(0))
    def _():
        gather(dest_next_ref, 1 - slot)

    for k in range(TOP_K):
        pltpu.make_async_copy(ybuf_ref.at[pl.ds(0, tc * n_sub)], rows.at[slot, k], sems.at[slot]).wait()

    route = route_ref[...]
    y = x_ref[...]
    for k in range(TOP_K):
        y = y + route[:, TOP_K + k:TOP_K + k + 1] * _tiles_to_rows(rows.at[slot, k], n_sub)
    out_ref[...] = _rmsnorm(y, fw_ref[...]) if final_norm else y


def _combine(x2, route, dest, ybuf, final_w, tc):
    t, d = x2.shape
    assert t % tc == 0 and tc % DMA_UNROLL == 0
    n_tiles = t // tc
    final_norm = final_w is not None
    fw = (final_w if final_norm else jnp.ones((d,), F32)).reshape(1, d).astype(F32)
    idx = _tile_indices(dest, tc)
    return pl.pallas_call(
        functools.partial(_combine_kernel, tc, final_norm),
        out_shape=jax.ShapeDtypeStruct((t, d), F32),
        grid=(n_tiles,),
        in_specs=[
            pl.BlockSpec((1, 1, TOP_K * tc), lambda i: (i, 0, 0), memory_space=pltpu.SMEM),
            pl.BlockSpec((1, 1, TOP_K * tc), lambda i: (jnp.minimum(i + 1, n_tiles - 1), 0, 0), memory_space=pltpu.SMEM),
            pl.BlockSpec((tc, d), lambda i: (i, 0)),
            pl.BlockSpec((tc, LANES), lambda i: (i, 0)),
            pl.BlockSpec((1, d), lambda i: (0, 0)),
            pl.BlockSpec(memory_space=pl.ANY),
        ],
        out_specs=pl.BlockSpec((tc, d), lambda i: (i, 0)),
        scratch_shapes=[pltpu.VMEM((2, TOP_K, tc * (d // LANES), LANES), F32), pltpu.SemaphoreType.DMA((2,))],
        compiler_params=pltpu.CompilerParams(
            dimension_semantics=("arbitrary",), vmem_limit_bytes=VMEM_LIMIT_BYTES),
        name="combine",
    )(idx, idx, x2, route, fw, ybuf)


def _moe_layer(x2, norm_w, w_router, w_gate, w_up, w_down, final_w, tiles):
    t, d = x2.shape
    bm = tiles["moe_bm"]
    tm = tiles["router_tm"]
    hn, route, route_t, counts = _router(x2, norm_w, w_router, tm)
    counts = counts[::SUBLANES, :N_EXPERTS].astype(jnp.int32)
    sizes = jnp.sum(counts, axis=0)
    padded = ((sizes + bm - 1) // bm) * bm
    ends = jnp.cumsum(padded)
    starts = ends - padded
    tile_base = starts[None, :] + jnp.cumsum(counts, axis=0) - counts
    e_idx = route_t[0:TOP_K].astype(jnp.int32).reshape(TOP_K, t // tm, tm)
    rank = route_t[2 * TOP_K:3 * TOP_K].astype(jnp.int32).reshape(TOP_K, t // tm, tm)
    base = jnp.zeros_like(rank)
    for e in range(N_EXPERTS):
        base = jnp.where(e_idx == e, tile_base[None, :, e:e + 1], base)
    dest = (base + rank).reshape(TOP_K, t)
    cap = TOP_K * t + N_EXPERTS * bm
    n_blocks = cap // bm
    first_slot = jnp.arange(n_blocks, dtype=jnp.int32) * bm
    blk_e = jnp.minimum(jnp.sum(first_slot[:, None] >= ends[None, :], axis=1), N_EXPERTS - 1).astype(jnp.int32)
    nact = (ends[-1:] // bm).astype(jnp.int32)

    n_sub = d // LANES
    dest_rows = dest * n_sub
    pad_slots = jnp.concatenate([starts + sizes, ends[-1:], padded - sizes, cap - ends[-1:]]).astype(jnp.int32)
    buf = _scatter_rows(hn, dest_rows, pad_slots, cap, tiles["scatter_ts"], n_sub)
    ybuf = _expert_ffn(buf, blk_e, nact, w_gate, w_up, w_down, bm, tiles["moe_tf"])
    return _combine(x2, route, dest_rows, ybuf, final_w, tiles["combine_tc"])


TILES = {
    "mixer_tl": 512,
    "ffn_tm": 512,
    "router_tm": 1024,
    "moe_bm": 512, "moe_tf": 1792,
    "scatter_ts": 1024,
    "combine_tc": 512,
}


def kernel(x, norm_mix, w_in, conv_a_w, conv_ssd_w, conv_ssd_b, dt_bias, a_log, d_skip, ssd_norm_w, conv_conf_w, conv_conf_b, conf_ln_g, conf_ln_b, w_out, norm_ffn, ffn_w_gate, ffn_w_up, ffn_w_down, moe_router, moe_w_gate, moe_w_up, moe_w_down, norm_final):
    bsz, seq, d = x.shape
    depth = norm_mix.shape[0]
    tiles = dict(TILES)
    tiles["mixer_tl"] = min(tiles["mixer_tl"], seq)
    expert_w = None
    for i in range(depth):
        x = _mixer_layer(x, norm_mix[i], w_in[i], conv_a_w[i], conv_ssd_w[i], conv_ssd_b[i], dt_bias[i], a_log[i],
                         d_skip[i], ssd_norm_w[i], conv_conf_w[i], conv_conf_b[i], conf_ln_g[i], conf_ln_b[i],
                         w_out[i], tiles["mixer_tl"])
        x2 = x.reshape(bsz * seq, d)
        final_w = norm_final if i == depth - 1 else None
        j = i // 2
        if i % 2 == 0:
            later = (moe_w_gate[j], moe_w_up[j], moe_w_down[j]) if i + 1 < depth else ()
            x2, expert_w = _dense_ffn(x2, norm_ffn[i], ffn_w_gate[j], ffn_w_up[j], ffn_w_down[j], final_w,
                                      tiles["ffn_tm"], later)
        else:
            if not expert_w:
                expert_w = [w.astype(BF16) for w in (moe_w_gate[j], moe_w_up[j], moe_w_down[j])]
            x2 = _moe_layer(x2, norm_ffn[i], moe_router[j], *expert_w, final_w, tiles)
            expert_w = None
        x = x2.reshape(bsz, seq, d)
    return x
```

```python
import functools

import jax
import jax.numpy as jnp
from jax import lax
from jax.experimental import pallas as pl
from jax.experimental.pallas import tpu as pltpu

F32 = jnp.float32
BF16 = jnp.bfloat16
EPS = 1e-5

LANES = 128
SUBLANES = 8
VMEM_LIMIT_BYTES = 56 * 1024 * 1024

GROUP_DIM = 64
A_CONV = 3
SSD_HEAD_DIM = 64
SSD_GROUPS = 2
SSD_STATE = 128
SSD_CONV = 4
CONF_KERNEL = 31
N_EXPERTS = 8
TOP_K = 2

SSD_CHUNK = 128
CONF_HALO = 32
SMALL_HALO = SUBLANES
CONV_ROWS = 32
ROW_BLOCK = 64
LANE_BLOCK = 256
DMA_UNROLL = 16


def _split3(v):
    hi = v.astype(BF16)
    r1 = v - hi.astype(F32)
    mid = r1.astype(BF16)
    lo = (r1 - mid.astype(F32)).astype(BF16)
    return hi, mid, lo


def _dot(a, b):
    return jnp.dot(a, b, preferred_element_type=F32)


def _dot_nt(a, b):
    return lax.dot_general(a, b, (((1,), (1,)), ((), ())), preferred_element_type=F32)


def _dot_sel_right(v, sel):
    hi, mid, lo = _split3(v)
    return _dot(hi, sel) + _dot(mid, sel) + _dot(lo, sel)


def _dot_sel_left(sel, v):
    hi, mid, lo = _split3(v)
    return _dot(sel, hi) + _dot(sel, mid) + _dot(sel, lo)


def _softplus(v):
    return jnp.maximum(v, 0.0) + jnp.log1p(jnp.exp(-jnp.abs(v)))


def _silu(v):
    return v * jax.nn.sigmoid(v)


def _tile_rows(w8, rows):
    return jnp.tile(w8, (rows // SUBLANES, 1))


def _rmsnorm(x, w):
    return x * lax.rsqrt(jnp.mean(x * x, axis=-1, keepdims=True) + EPS) * w


def _mixer_kernel(dims, x_ref, nw_ref, win_ref, wdtT_ref, caw_ref, csw_ref, csb_ref, dtbr_ref,
                  alr_ref, dsk_ref, snw_ref, ccw_ref, ccb_ref, lng_ref, lnb_ref, wout_ref,
                  out_ref,
                  hn_s, p_a, p_c, p_s, p_z, dtr, buf_a, buf_c, shc, xact, ysd, ycat, hstate):
    tl, w_a, w_ssd, w_conf, n_heads, n_l = dims
    q = SSD_CHUNK
    n_bc = SSD_GROUPS * SSD_STATE
    xbc = w_ssd + 2 * n_bc
    hpg = n_heads // SSD_GROUPS
    gw = hpg * SSD_HEAD_DIM
    hb = SMALL_HALO
    o_z = 3 * w_a
    o_xbc = o_z + w_ssd
    o_c = o_xbc + xbc
    lt = lax.rem(pl.program_id(0), n_l)

    @pl.when(lt == 0)
    def _():
        buf_a[0:hb, :] = jnp.zeros((hb, w_a), F32)
        p_s[0:hb, :] = jnp.zeros((hb, xbc), F32)
        buf_c[0:CONF_HALO, :] = jnp.zeros((CONF_HALO, w_conf), F32)
        hstate[...] = jnp.zeros(hstate.shape, F32)

    @pl.when(lt > 0)
    def _():
        buf_a[0:hb, :] = buf_a[tl:tl + hb, :]
        p_s[0:hb, :] = p_s[tl:tl + hb, :]
        buf_c[0:CONF_HALO, :] = buf_c[tl:tl + CONF_HALO, :]

    hn_s[...] = _rmsnorm(x_ref[...], nw_ref[...]).astype(BF16)

    p_s[hb:hb + tl, :] = _dot(hn_s[...], win_ref[:, o_xbc:o_xbc + xbc])
    for rb in range(tl // ROW_BLOCK):
        for lb in range(xbc // LANE_BLOCK):
            cols = slice(lb * LANE_BLOCK, (lb + 1) * LANE_BLOCK)
            xc = _tile_rows(csb_ref[:, cols], ROW_BLOCK)
            for j in range(SSD_CONV):
                s = hb + rb * ROW_BLOCK - (SSD_CONV - 1) + j
                xc = xc + _tile_rows(csw_ref[j, :, cols], ROW_BLOCK) * p_s[s:s + ROW_BLOCK, cols]
            xact[rb * ROW_BLOCK:(rb + 1) * ROW_BLOCK, cols] = _silu(xc)
    p_z[...] = _dot(hn_s[...], win_ref[:, o_z:o_z + w_ssd])
    dtr[...] = _dot_nt(wdtT_ref[...], hn_s[...])

    p_c[...] = _dot(hn_s[...], win_ref[:, o_c:o_c + 2 * w_conf])
    for rb in range(tl // ROW_BLOCK):
        r0 = rb * ROW_BLOCK
        buf_c[CONF_HALO + r0:CONF_HALO + r0 + ROW_BLOCK, :] = (
            p_c[r0:r0 + ROW_BLOCK, 0:w_conf] * jax.nn.sigmoid(p_c[r0:r0 + ROW_BLOCK, w_conf:2 * w_conf]))
    n_sh = shc.shape[1]
    for r in range(1, SUBLANES):
        shc[r - 1] = buf_c[r:r + n_sh, :]
    for rb in range(tl // CONV_ROWS):
        acc = _tile_rows(ccb_ref[...], CONV_ROWS)
        for j in range(CONF_KERNEL):
            a, r = divmod(CONF_HALO - (CONF_KERNEL - 1) + j, SUBLANES)
            base = a * SUBLANES + rb * CONV_ROWS
            src = buf_c[base:base + CONV_ROWS, :] if r == 0 else shc[r - 1, base:base + CONV_ROWS, :]
            acc = acc + _tile_rows(ccw_ref[j], CONV_ROWS) * src
        mu = jnp.mean(acc, axis=-1, keepdims=True)
        cen = acc - mu
        var = jnp.mean(cen * cen, axis=-1, keepdims=True)
        yc = cen * lax.rsqrt(var + EPS) * lng_ref[...] + lnb_ref[...]
        ycat[rb * CONV_ROWS:(rb + 1) * CONV_ROWS, w_a + w_ssd:w_a + w_ssd + w_conf] = _silu(yc).astype(BF16)

    p_a[...] = _dot(hn_s[...], win_ref[:, 0:3 * w_a])
    for rb in range(tl // ROW_BLOCK):
        r0 = rb * ROW_BLOCK
        b0 = hb + r0
        buf_a[b0:b0 + ROW_BLOCK, :] = p_a[r0:r0 + ROW_BLOCK, w_a:2 * w_a] * p_a[r0:r0 + ROW_BLOCK, 2 * w_a:3 * w_a]
        ya = jnp.zeros((ROW_BLOCK, w_a), F32)
        for j in range(A_CONV):
            s = b0 - (A_CONV - 1) + j
            ya = ya + _tile_rows(caw_ref[j], ROW_BLOCK) * buf_a[s:s + ROW_BLOCK, :]
        ycat[r0:r0 + ROW_BLOCK, 0:w_a] = (p_a[r0:r0 + ROW_BLOCK, 0:w_a] * ya).astype(BF16)

    a_row = -jnp.exp(alr_ref[...])
    ri = lax.broadcasted_iota(jnp.int32, (q, q), 0)
    ci = lax.broadcasted_iota(jnp.int32, (q, q), 1)
    causal = ri >= ci
    triu = jnp.where(ri <= ci, 1.0, 0.0).astype(BF16)
    hd = SSD_HEAD_DIM

    for c in range(tl // q):
        rows = slice(c * q, (c + 1) * q)
        dt_r = _softplus(dtr[:, rows] + dtbr_ref[...])
        acum_r = _dot_sel_right(dt_r * a_row, triu)
        to_end_r = dt_r * jnp.exp(acum_r[:, q - 1:q] - acum_r)
        acum_c = jnp.concatenate([acum_r, jnp.zeros((LANES - n_heads, q), F32)], axis=0).T
        for g in range(SSD_GROUPS):
            bg = xact[rows, w_ssd + g * SSD_STATE:w_ssd + (g + 1) * SSD_STATE]
            cg = xact[rows, w_ssd + n_bc + g * SSD_STATE:w_ssd + n_bc + (g + 1) * SSD_STATE].astype(BF16)
            cb = _dot_nt(cg, bg.astype(BF16))
            bgt = bg.T
            hg = hstate[g]
            yoff = _dot(cg, hg.astype(BF16))
            for r in range(hpg):
                h = g * hpg + r
                xh = xact[rows, h * hd:(h + 1) * hd].astype(BF16)
                acol = jnp.broadcast_to(acum_c[:, h:h + 1], (q, q))
                dec = jnp.where(causal, jnp.exp(acol - acum_r[h:h + 1, :]), 0.0)
                yd = _dot((cb * dec * dt_r[h:h + 1, :]).astype(BF16), xh)
                ysd[rows, h * hd:(h + 1) * hd] = yd + yoff[:, r * hd:(r + 1) * hd] * jnp.exp(acol[:, 0:hd])
                st = _dot((bgt * to_end_r[h:h + 1, :]).astype(BF16), xh)
                hstate[g, :, r * hd:(r + 1) * hd] = jnp.exp(acol[q - 1:q, 0:hd]) * hg[:, r * hd:(r + 1) * hd] + st
        xs = xact[rows, 0:w_ssd]
        yz = (ysd[rows, :] + dsk_ref[...] * xs) * _silu(p_z[rows, :])
        for g in range(SSD_GROUPS):
            yg = yz[:, g * gw:(g + 1) * gw]
            yn = yg * lax.rsqrt(jnp.mean(yg * yg, axis=-1, keepdims=True) + EPS) * snw_ref[:, g * gw:(g + 1) * gw]
            ycat[rows, w_a + g * gw:w_a + (g + 1) * gw] = yn.astype(BF16)

    out_ref[...] = x_ref[...] + _dot(ycat[...], wout_ref[...])


def _mixer_layer(x, norm_w, w_in, conv_a_w, conv_ssd_w, conv_ssd_b, dt_bias, a_log, d_skip, ssd_norm_w,
                 conv_conf_w, conv_conf_b, conf_ln_g, conf_ln_b, w_out, tl):
    bsz, seq, d = x.shape
    w_a = conv_a_w.shape[1]
    w_ssd = ssd_norm_w.shape[0]
    w_conf = conv_conf_w.shape[1]
    n_heads = dt_bias.shape[0]
    n_bc = SSD_GROUPS * SSD_STATE
    xbc = w_ssd + 2 * n_bc
    assert n_heads == SUBLANES and n_heads * SSD_HEAD_DIM == w_ssd and seq % tl == 0 and tl % SSD_CHUNK == 0
    assert tl % ROW_BLOCK == 0 and xbc % LANE_BLOCK == 0 and conv_a_w.shape[0] == A_CONV and conv_ssd_w.shape[0] == SSD_CONV
    assert conv_conf_w.shape[0] == CONF_KERNEL
    o_dt = 3 * w_a + w_ssd + xbc
    w_dt = w_in[:, o_dt:o_dt + n_heads]
    w_main = jnp.concatenate([w_in[:, :o_dt], w_in[:, o_dt + n_heads:]], axis=1).astype(BF16)
    n_cols = w_main.shape[1]
    w_dt_t = w_dt.T.astype(BF16)
    row = lambda v: v.reshape(1, -1).astype(F32)
    col = lambda v: v.reshape(-1, 1).astype(F32)
    taps = lambda w: jnp.broadcast_to(w.astype(F32)[:, None, :], (w.shape[0], SUBLANES, w.shape[1]))
    d_skip_x = jnp.repeat(d_skip.astype(F32), SSD_HEAD_DIM).reshape(1, w_ssd)

    const = lambda shape: pl.BlockSpec(shape, lambda s: (0,) * len(shape))
    n_l = seq // tl
    n_tiles = bsz * n_l
    dims = (tl, w_a, w_ssd, w_conf, n_heads, n_l)
    x2 = x.reshape(bsz * seq, d)
    out = pl.pallas_call(
        functools.partial(_mixer_kernel, dims),
        out_shape=jax.ShapeDtypeStruct((bsz * seq, d), F32),
        grid=(n_tiles,),
        in_specs=[
            pl.BlockSpec((tl, d), lambda s: (s, 0)),
            const((1, d)), const((d, n_cols)), const((n_heads, d)),
            const((A_CONV, SUBLANES, w_a)), const((SSD_CONV, SUBLANES, xbc)), const((SUBLANES, xbc)),
            const((n_heads, 1)), const((n_heads, 1)),
            const((1, w_ssd)), const((1, w_ssd)),
            const((CONF_KERNEL, SUBLANES, w_conf)), const((SUBLANES, w_conf)), const((1, w_conf)), const((1, w_conf)),
            const((d, d)),
        ],
        out_specs=pl.BlockSpec((tl, d), lambda s: (s, 0)),
        scratch_shapes=[
            pltpu.VMEM((tl, d), BF16),
            pltpu.VMEM((tl, 3 * w_a), F32),
            pltpu.VMEM((tl, 2 * w_conf), F32),
            pltpu.VMEM((SMALL_HALO + tl, xbc), F32),
            pltpu.VMEM((tl, w_ssd), F32),
            pltpu.VMEM((n_heads, tl), F32),
            pltpu.VMEM((SMALL_HALO + tl, w_a), F32),
            pltpu.VMEM((CONF_HALO + tl, w_conf), F32),
            pltpu.VMEM((SUBLANES - 1, tl + CONF_HALO - SUBLANES, w_conf), F32),
            pltpu.VMEM((tl, xbc), F32),
            pltpu.VMEM((tl, w_ssd), F32),
            pltpu.VMEM((tl, d), BF16),
            pltpu.VMEM((SSD_GROUPS, SSD_STATE, w_ssd // SSD_GROUPS), F32),
        ],
        compiler_params=pltpu.CompilerParams(
            dimension_semantics=("arbitrary",), vmem_limit_bytes=VMEM_LIMIT_BYTES),
        name="mixer",
    )(x2, row(norm_w), w_main, w_dt_t, taps(conv_a_w), taps(conv_ssd_w), taps(conv_ssd_b[None])[0],
      col(dt_bias), col(a_log), d_skip_x, row(ssd_norm_w),
      taps(conv_conf_w), taps(conv_conf_b[None])[0], row(conf_ln_g), row(conf_ln_b), w_out.astype(BF16))
    return out.reshape(bsz, seq, d)


def _dense_ffn_kernel(final_norm, n_cast, x_ref, nw_ref, wg_ref, wu_ref, wd_ref, fw_ref, *rest):
    cast_in, out_ref, cast_out, hn = rest[:n_cast], rest[n_cast], rest[n_cast + 1:2 * n_cast + 1], rest[-1]
    hn[...] = _rmsnorm(x_ref[...], nw_ref[...]).astype(BF16)
    a = (_silu(_dot(hn[...], wg_ref[...])) * _dot(hn[...], wu_ref[...])).astype(BF16)
    y = x_ref[...] + _dot(a, wd_ref[...])
    out_ref[...] = _rmsnorm(y, fw_ref[...]) if final_norm else y
    for src, dst in zip(cast_in, cast_out):
        dst[...] = src[...].astype(BF16)


def _dense_ffn(x2, norm_w, w_gate, w_up, w_down, final_w, tm, later=()):
    t, d = x2.shape
    ff = w_gate.shape[1]
    assert t % tm == 0
    n_steps = t // tm
    final_norm = final_w is not None
    fw = (final_w if final_norm else jnp.ones((d,), F32)).reshape(1, d).astype(F32)
    resident = lambda shape: pl.BlockSpec(shape, lambda i: (0, 0), pipeline_mode=pl.Buffered(1))
    bf16_rows = 2 * SUBLANES
    slabs = []
    for w in later:
        rows = w.size // w.shape[-1]
        assert rows % (n_steps * bf16_rows) == 0
        slabs.append((rows // n_steps, w.shape[-1]))
    slab_specs = [pl.BlockSpec(s, lambda i: (i, 0)) for s in slabs]
    res = pl.pallas_call(
        functools.partial(_dense_ffn_kernel, final_norm, len(later)),
        out_shape=[jax.ShapeDtypeStruct((t, d), F32)]
        + [jax.ShapeDtypeStruct((n_steps * r, c), BF16) for r, c in slabs],
        grid=(n_steps,),
        in_specs=[
            pl.BlockSpec((tm, d), lambda i: (i, 0)),
            resident((1, d)), resident((d, ff)), resident((d, ff)), resident((ff, d)), resident((1, d)),
        ] + slab_specs,
        out_specs=[pl.BlockSpec((tm, d), lambda i: (i, 0))] + slab_specs,
        scratch_shapes=[pltpu.VMEM((tm, d), BF16)],
        compiler_params=pltpu.CompilerParams(
            dimension_semantics=("arbitrary",), vmem_limit_bytes=VMEM_LIMIT_BYTES),
        name="dense_ffn",
    )(x2, norm_w.reshape(1, d).astype(F32), w_gate.astype(BF16), w_up.astype(BF16), w_down.astype(BF16), fw,
      *[w.reshape(n_steps * r, c) for w, (r, c) in zip(later, slabs)])
    return res[0], [b.reshape(w.shape) for b, w in zip(res[1:], later)]


def _rows_to_tiles(tile_ref, val):
    n, d = val.shape
    n_sub = d // LANES
    for c in range(n_sub):
        tile_ref[pl.ds(c, n, stride=n_sub), :] = val[:, c * LANES:(c + 1) * LANES]


def _tiles_to_rows(tile_ref, n_sub):
    n = tile_ref.shape[0] // n_sub
    return jnp.concatenate([tile_ref[pl.ds(c, n, stride=n_sub), :] for c in range(n_sub)], axis=1)


def _expert_ffn_kernel(blk_e_ref, nact_ref, xb_ref, wg_ref, wu_ref, wd_ref, out_ref, hb, acc):
    i = pl.program_id(0)
    j = pl.program_id(1)

    n_j = pl.num_programs(1)
    active = i < nact_ref[0]

    def partial_out():
        h = hb[...]
        a = (_silu(_dot(h, wg_ref[...])) * _dot(h, wu_ref[...])).astype(BF16)
        return _dot(a, wd_ref[...])

    @pl.when(active & (j == 0))
    def _():
        hb[...] = _tiles_to_rows(xb_ref, hb.shape[1] // LANES).astype(BF16)
        acc[...] = partial_out()

    @pl.when(active & (j > 0) & (j < n_j - 1))
    def _():
        acc[...] += partial_out()

    @pl.when(active & (j == n_j - 1))
    def _():
        _rows_to_tiles(out_ref, acc[...] + partial_out())

    @pl.when((i >= nact_ref[0]) & (j == 0))
    def _():
        out_ref[...] = jnp.zeros(out_ref.shape, F32)


def _expert_ffn(buf, blk_e, nact, w_gate, w_up, w_down, bm, tf):
    d, ff = w_gate.shape[1], w_gate.shape[2]
    n_sub = d // LANES
    cap = buf.shape[0] // n_sub
    n_blocks = cap // bm
    n_j = ff // tf
    assert cap % bm == 0 and ff % tf == 0 and n_j >= 2 and w_gate.dtype == BF16
    row_blk = lambda i, j, be, na: (jnp.minimum(i, na[0] - 1), 0)
    snake = lambda i, j: jnp.where(i % 2 == 0, j, n_j - 1 - j)
    col_j = lambda i, j, na: jnp.where(i < na[0], snake(i, j), snake(na[0] - 1, n_j - 1))
    return pl.pallas_call(
        _expert_ffn_kernel,
        out_shape=jax.ShapeDtypeStruct((cap * n_sub, LANES), F32),
        grid_spec=pltpu.PrefetchScalarGridSpec(
            num_scalar_prefetch=2,
            grid=(n_blocks, n_j),
            in_specs=[
                pl.BlockSpec((bm * n_sub, LANES), row_blk),
                pl.BlockSpec((None, d, tf), lambda i, j, be, na: (be[i], 0, col_j(i, j, na))),
                pl.BlockSpec((None, d, tf), lambda i, j, be, na: (be[i], 0, col_j(i, j, na))),
                pl.BlockSpec((None, tf, d), lambda i, j, be, na: (be[i], col_j(i, j, na), 0)),
            ],
            out_specs=pl.BlockSpec((bm * n_sub, LANES), lambda i, j, be, na: (i, 0)),
            scratch_shapes=[pltpu.VMEM((bm, d), BF16), pltpu.VMEM((bm, d), F32)],
        ),
        compiler_params=pltpu.CompilerParams(
            dimension_semantics=("arbitrary", "arbitrary"), vmem_limit_bytes=VMEM_LIMIT_BYTES),
        name="expert_ffn",
    )(blk_e, nact, buf, w_gate, w_up, w_down)


RANK_BLOCK = 256


def _router_kernel(x_ref, nw_ref, wr_ref, hn_ref, route_ref, route_t_ref, count_ref):
    h = _rmsnorm(x_ref[...], nw_ref[...])
    _rows_to_tiles(hn_ref, h)
    h_hi, h_lo, _ = _split3(h)
    w_hi, w_lo, _ = _split3(wr_ref[...])
    logits = _dot(h_hi, w_hi) + _dot(h_hi, w_lo) + _dot(h_lo, w_hi)
    lane = lax.broadcasted_iota(jnp.int32, logits.shape, 1)
    neg = jnp.float32(-jnp.inf)
    logits = jnp.where(lane < N_EXPERTS, logits, neg)
    m1 = jnp.max(logits, axis=-1, keepdims=True)
    i1 = jnp.min(jnp.where(logits == m1, lane, LANES), axis=-1, keepdims=True)
    rest = jnp.where(lane == i1, neg, logits)
    m2 = jnp.max(rest, axis=-1, keepdims=True)
    i2 = jnp.min(jnp.where(rest == m2, lane, LANES), axis=-1, keepdims=True)
    e2 = jnp.exp(m2 - m1)
    g1 = 1.0 / (1.0 + e2)
    g2 = e2 * g1
    pick1, pick2 = lane == i1, lane == i2
    onehot = jnp.where(pick1 | pick2, 1.0, 0.0)
    rb = RANK_BLOCK
    ri = lax.broadcasted_iota(jnp.int32, (rb, rb), 0)
    ci = lax.broadcasted_iota(jnp.int32, (rb, rb), 1)
    before = jnp.where(ri > ci, 1.0, 0.0).astype(BF16)
    total = jnp.zeros((1, LANES), F32)
    earlier = []
    for b in range(h.shape[0] // rb):
        oh_b = onehot[b * rb:(b + 1) * rb, :]
        earlier.append(_dot(before, oh_b.astype(BF16)) + total)
        total = total + jnp.sum(oh_b, axis=0, keepdims=True)
    earlier = jnp.concatenate(earlier, axis=0)
    r1 = jnp.sum(jnp.where(pick1, earlier, 0.0), axis=-1, keepdims=True)
    r2 = jnp.sum(jnp.where(pick2, earlier, 0.0), axis=-1, keepdims=True)
    cols = (i1.astype(F32), i2.astype(F32), g1, g2, r1, r2)
    route = jnp.zeros(logits.shape, F32)
    for c, v in enumerate(cols):
        route = jnp.where(lane == c, v, route)
    route_ref[...] = route
    route_t_ref[...] = route.T[0:SUBLANES, :]
    count_ref[...] = jnp.broadcast_to(total, count_ref.shape)


def _router(x2, norm_w, w_router, tm):
    t, d = x2.shape
    n_e = w_router.shape[1]
    assert n_e == N_EXPERTS and t % tm == 0 and d % LANES == 0 and tm % RANK_BLOCK == 0
    n_sub = d // LANES
    wr = jnp.pad(w_router.astype(F32), ((0, 0), (0, LANES - n_e)))
    return pl.pallas_call(
        _router_kernel,
        out_shape=(jax.ShapeDtypeStruct((t * n_sub, LANES), F32), jax.ShapeDtypeStruct((t, LANES), F32),
                   jax.ShapeDtypeStruct((SUBLANES, t), F32),
                   jax.ShapeDtypeStruct((t // tm * SUBLANES, LANES), F32)),
        grid=(t // tm,),
        in_specs=[
            pl.BlockSpec((tm, d), lambda i: (i, 0)),
            pl.BlockSpec((1, d), lambda i: (0, 0)),
            pl.BlockSpec((d, LANES), lambda i: (0, 0)),
        ],
        out_specs=(pl.BlockSpec((tm * n_sub, LANES), lambda i: (i, 0)),
                   pl.BlockSpec((tm, LANES), lambda i: (i, 0)),
                   pl.BlockSpec((SUBLANES, tm), lambda i: (0, i)),
                   pl.BlockSpec((SUBLANES, LANES), lambda i: (i, 0))),
        compiler_params=pltpu.CompilerParams(
            dimension_semantics=("arbitrary",), vmem_limit_bytes=VMEM_LIMIT_BYTES),
        name="router",
    )(x2, norm_w.reshape(1, d).astype(F32), wr)


def _row_copy(n_sub, src_ref, src_row, dst_ref, dst_row, sem):
    return pltpu.make_async_copy(src_ref.at[pl.ds(pl.multiple_of(src_row, n_sub), n_sub)],
                                 dst_ref.at[pl.ds(pl.multiple_of(dst_row, n_sub), n_sub)], sem)


def _tile_indices(dest, tile):
    k, t = dest.shape
    return dest.reshape(k, t // tile, tile).transpose(1, 0, 2).reshape(t // tile, 1, k * tile)


def _scatter_kernel(ts, n_sub, dest_ref, pad_ref, hn_ref, buf_ref, zero_tile, sem, pad_sem):
    @pl.when(pl.program_id(0) == 0)
    def _():
        zero_tile[...] = jnp.zeros(zero_tile.shape, F32)
        n_ranges = pad_ref.shape[0] // 2
        for e in range(n_ranges):
            first, count = pad_ref[e], pad_ref[n_ranges + e]

            def fill(p, carry):
                _row_copy(n_sub, zero_tile, 0, buf_ref, (first + p) * n_sub, pad_sem).start()
                return carry

            def fill_done(p, carry):
                _row_copy(n_sub, zero_tile, 0, buf_ref, 0, pad_sem).wait()
                return carry

            lax.fori_loop(0, count, fill, 0)
            lax.fori_loop(0, count, fill_done, 0)

    def issue(rb, carry):
        for u in range(DMA_UNROLL):
            r = rb * DMA_UNROLL + u
            for k in range(TOP_K):
                _row_copy(n_sub, hn_ref, r * n_sub, buf_ref, dest_ref[0, 0, k * ts + r], sem).start(priority=k % 2)
        return carry

    lax.fori_loop(0, ts // DMA_UNROLL, issue, 0)
    for k in range(TOP_K):
        pltpu.make_async_copy(hn_ref, buf_ref.at[pl.ds(0, ts * n_sub)], sem).wait()


def _scatter_rows(hn, dest, pad_slots, cap, ts, n_sub):
    t = hn.shape[0] // n_sub
    assert t % ts == 0 and ts % DMA_UNROLL == 0
    return pl.pallas_call(
        functools.partial(_scatter_kernel, ts, n_sub),
        out_shape=jax.ShapeDtypeStruct((cap * n_sub, LANES), F32),
        grid=(t // ts,),
        in_specs=[
            pl.BlockSpec((1, 1, TOP_K * ts), lambda i: (i, 0, 0), memory_space=pltpu.SMEM),
            pl.BlockSpec(memory_space=pltpu.SMEM),
            pl.BlockSpec((ts * n_sub, LANES), lambda i: (i, 0)),
        ],
        out_specs=pl.BlockSpec(memory_space=pl.ANY),
        scratch_shapes=[pltpu.VMEM((n_sub, LANES), F32), pltpu.SemaphoreType.DMA(()), pltpu.SemaphoreType.DMA(())],
        compiler_params=pltpu.CompilerParams(
            dimension_semantics=("arbitrary",), vmem_limit_bytes=VMEM_LIMIT_BYTES),
        name="scatter_rows",
    )(_tile_indices(dest, ts), pad_slots, hn)


def _combine_kernel(tc, final_norm, dest_ref, dest_next_ref, x_ref, route_ref, fw_ref, ybuf_ref, out_ref, rows, sems):
    n_sub = x_ref.shape[1] // LANES
    i = pl.program_id(0)
    slot = i % 2

    def gather(idx_ref, s):
        def issue(rb, carry):
            for u in range(DMA_UNROLL):
                r = rb * DMA_UNROLL + u
                for k in range(TOP_K):
                    _row_copy(n_sub, ybuf_ref, idx_ref[0, 0, k * tc + r], rows.at[s, k], r * n_sub,
                              sems.at[s]).start(priority=k % 2)
            return carry

        lax.fori_loop(0, tc // DMA_UNROLL, issue, 0)

    @pl.when(i == 0)
    def _():
        gather(dest_ref, 0)

    @pl.when(i + 1 < pl.num_programs(0))
    def _():
        gather(dest_next_ref, 1 - slot)

    for k in range(TOP_K):
        pltpu.make_async_copy(ybuf_ref.at[pl.ds(0, tc * n_sub)], rows.at[slot, k], sems.at[slot]).wait()

    route = route_ref[...]
    y = x_ref[...]
    for k in range(TOP_K):
        y = y + route[:, TOP_K + k:TOP_K + k + 1] * _tiles_to_rows(rows.at[slot, k], n_sub)
    out_ref[...] = _rmsnorm(y, fw_ref[...]) if final_norm else y


def _combine(x2, route, dest, ybuf, final_w, tc):
    t, d = x2.shape
    assert t % tc == 0 and tc % DMA_UNROLL == 0
    n_tiles = t // tc
    final_norm = final_w is not None
    fw = (final_w if final_norm else jnp.ones((d,), F32)).reshape(1, d).astype(F32)
    idx = _tile_indices(dest, tc)
    return pl.pallas_call(
        functools.partial(_combine_kernel, tc, final_norm),
        out_shape=jax.ShapeDtypeStruct((t, d), F32),
        grid=(n_tiles,),
        in_specs=[
            pl.BlockSpec((1, 1, TOP_K * tc), lambda i: (i, 0, 0), memory_space=pltpu.SMEM),
            pl.BlockSpec((1, 1, TOP_K * tc), lambda i: (jnp.minimum(i + 1, n_tiles - 1), 0, 0), memory_space=pltpu.SMEM),
            pl.BlockSpec((tc, d), lambda i: (i, 0)),
            pl.BlockSpec((tc, LANES), lambda i: (i, 0)),
            pl.BlockSpec((1, d), lambda i: (0, 0)),
            pl.BlockSpec(memory_space=pl.ANY),
        ],
        out_specs=pl.BlockSpec((tc, d), lambda i: (i, 0)),
        scratch_shapes=[pltpu.VMEM((2, TOP_K, tc * (d // LANES), LANES), F32), pltpu.SemaphoreType.DMA((2,))],
        compiler_params=pltpu.CompilerParams(
            dimension_semantics=("arbitrary",), vmem_limit_bytes=VMEM_LIMIT_BYTES),
        name="combine",
    )(idx, idx, x2, route, fw, ybuf)


def _moe_layer(x2, norm_w, w_router, w_gate, w_up, w_down, final_w, tiles):
    t, d = x2.shape
    bm = tiles["moe_bm"]
    tm = tiles["router_tm"]
    hn, route, route_t, counts = _router(x2, norm_w, w_router, tm)
    counts = counts[::SUBLANES, :N_EXPERTS].astype(jnp.int32)
    sizes = jnp.sum(counts, axis=0)
    padded = ((sizes + bm - 1) // bm) * bm
    ends = jnp.cumsum(padded)
    starts = ends - padded
    tile_base = starts[None, :] + jnp.cumsum(counts, axis=0) - counts
    e_idx = route_t[0:TOP_K].astype(jnp.int32).reshape(TOP_K, t // tm, tm)
    rank = route_t[2 * TOP_K:3 * TOP_K].astype(jnp.int32).reshape(TOP_K, t // tm, tm)
    base = jnp.zeros_like(rank)
    for e in range(N_EXPERTS):
        base = jnp.where(e_idx == e, tile_base[None, :, e:e + 1], base)
    dest = (base + rank).reshape(TOP_K, t)
    cap = TOP_K * t + N_EXPERTS * bm
    n_blocks = cap // bm
    first_slot = jnp.arange(n_blocks, dtype=jnp.int32) * bm
    blk_e = jnp.minimum(jnp.sum(first_slot[:, None] >= ends[None, :], axis=1), N_EXPERTS - 1).astype(jnp.int32)
    nact = (ends[-1:] // bm).astype(jnp.int32)

    n_sub = d // LANES
    dest_rows = dest * n_sub
    pad_slots = jnp.concatenate([starts + sizes, ends[-1:], padded - sizes, cap - ends[-1:]]).astype(jnp.int32)
    buf = _scatter_rows(hn, dest_rows, pad_slots, cap, tiles["scatter_ts"], n_sub)
    ybuf = _expert_ffn(buf, blk_e, nact, w_gate, w_up, w_down, bm, tiles["moe_tf"])
    return _combine(x2, route, dest_rows, ybuf, final_w, tiles["combine_tc"])


TILES = {
    "mixer_tl": 512,
    "ffn_tm": 512,
    "router_tm": 1024,
    "moe_bm": 512, "moe_tf": 1792,
    "scatter_ts": 1024,
    "combine_tc": 512,
}


def kernel(x, norm_mix, w_in, conv_a_w, conv_ssd_w, conv_ssd_b, dt_bias, a_log, d_skip, ssd_norm_w, conv_conf_w, conv_conf_b, conf_ln_g, conf_ln_b, w_out, norm_ffn, ffn_w_gate, ffn_w_up, ffn_w_down, moe_router, moe_w_gate, moe_w_up, moe_w_down, norm_final):
    bsz, seq, d = x.shape
    depth = norm_mix.shape[0]
    tiles = dict(TILES)
    tiles["mixer_tl"] = min(tiles["mixer_tl"], seq)
    expert_w = None
    for i in range(depth):
        x = _mixer_layer(x, norm_mix[i], w_in[i], conv_a_w[i], conv_ssd_w[i], conv_ssd_b[i], dt_bias[i], a_log[i],
                         d_skip[i], ssd_norm_w[i], conv_conf_w[i], conv_conf_b[i], conf_ln_g[i], conf_ln_b[i],
                         w_out[i], tiles["mixer_tl"])
        x2 = x.reshape(bsz * seq, d)
        final_w = norm_final if i == depth - 1 else None
        j = i // 2
        if i % 2 == 0:
            later = (moe_w_gate[j], moe_w_up[j], moe_w_down[j]) if i + 1 < depth else ()
            x2, expert_w = _dense_ffn(x2, norm_ffn[i], ffn_w_gate[j], ffn_w_up[j], ffn_w_down[j], final_w,
                                      tiles["ffn_tm"], later)
        else:
            if not expert_w:
                expert_w = [w.astype(BF16) for w in (moe_w_gate[j], moe_w_up[j], moe_w_down[j])]
            x2 = _moe_layer(x2, norm_ffn[i], moe_router[j], *expert_w, final_w, tiles)
            expert_w = None
        x = x2.reshape(bsz, seq, d)
    return x
```

```python
import functools

import jax
import jax.numpy as jnp
from jax import lax
from jax.experimental import pallas as pl
from jax.experimental.pallas import tpu as pltpu

F32 = jnp.float32
BF16 = jnp.bfloat16
EPS = 1e-5

LANES = 128
SUBLANES = 8
VMEM_LIMIT_BYTES = 56 * 1024 * 1024

A_CONV = 3
SSD_HEAD_DIM = 64
SSD_GROUPS = 2
SSD_STATE = 128
SSD_CONV = 4
CONF_KERNEL = 31
N_EXPERTS = 8
TOP_K = 2

SSD_CHUNK = 128
CONF_HALO = 32
SMALL_HALO = SUBLANES
CONV_ROWS = 32
ROW_BLOCK = 64
LANE_BLOCK = 256
DMA_UNROLL = 16


def _split3(v):
    hi = v.astype(BF16)
    r1 = v - hi.astype(F32)
    mid = r1.astype(BF16)
    lo = (r1 - mid.astype(F32)).astype(BF16)
    return hi, mid, lo


def _dot(a, b):
    return jnp.dot(a, b, preferred_element_type=F32)


def _dot_nt(a, b):
    return lax.dot_general(a, b, (((1,), (1,)), ((), ())), preferred_element_type=F32)


def _dot_sel_right(v, sel):
    hi, mid, lo = _split3(v)
    return _dot(hi, sel) + _dot(mid, sel) + _dot(lo, sel)


def _softplus(v):
    return jnp.maximum(v, 0.0) + jnp.log1p(jnp.exp(-jnp.abs(v)))


def _silu(v):
    return v * jax.nn.sigmoid(v)


def _tile_rows(w8, rows):
    return jnp.tile(w8, (rows // SUBLANES, 1))


def _rmsnorm(x, w):
    return x * lax.rsqrt(jnp.mean(x * x, axis=-1, keepdims=True) + EPS) * w


def _mixer_kernel(dims, x_ref, nw_ref, win_ref, wdtT_ref, caw_ref, csw_ref, csb_ref, dtbr_ref,
                  alr_ref, dsk_ref, snw_ref, ccw_ref, ccb_ref, lng_ref, lnb_ref, wout_ref,
                  out_ref,
                  hn_s, p_a, p_c, p_s, p_z, dtr, buf_a, buf_c, shc, xact, ysd, ycat, hstate):
    tl, w_a, w_ssd, w_conf, n_heads, n_l = dims
    q = SSD_CHUNK
    n_bc = SSD_GROUPS * SSD_STATE
    xbc = w_ssd + 2 * n_bc
    hpg = n_heads // SSD_GROUPS
    gw = hpg * SSD_HEAD_DIM
    hb = SMALL_HALO
    o_z = 3 * w_a
    o_xbc = o_z + w_ssd
    o_c = o_xbc + xbc
    lt = lax.rem(pl.program_id(0), n_l)

    @pl.when(lt == 0)
    def _():
        buf_a[0:hb, :] = jnp.zeros((hb, w_a), F32)
        p_s[0:hb, :] = jnp.zeros((hb, xbc), F32)
        buf_c[0:CONF_HALO, :] = jnp.zeros((CONF_HALO, w_conf), F32)
        hstate[...] = jnp.zeros(hstate.shape, F32)

    @pl.when(lt > 0)
    def _():
        buf_a[0:hb, :] = buf_a[tl:tl + hb, :]
        p_s[0:hb, :] = p_s[tl:tl + hb, :]
        buf_c[0:CONF_HALO, :] = buf_c[tl:tl + CONF_HALO, :]

    hn_s[...] = _rmsnorm(x_ref[...], nw_ref[...]).astype(BF16)

    p_s[hb:hb + tl, :] = _dot(hn_s[...], win_ref[:, o_xbc:o_xbc + xbc])
    for rb in range(tl // ROW_BLOCK):
        for lb in range(xbc // LANE_BLOCK):
            cols = slice(lb * LANE_BLOCK, (lb + 1) * LANE_BLOCK)
            xc = _tile_rows(csb_ref[:, cols], ROW_BLOCK)
            for j in range(SSD_CONV):
                s = hb + rb * ROW_BLOCK - (SSD_CONV - 1) + j
                xc = xc + _tile_rows(csw_ref[j, :, cols], ROW_BLOCK) * p_s[s:s + ROW_BLOCK, cols]
            xact[rb * ROW_BLOCK:(rb + 1) * ROW_BLOCK, cols] = _silu(xc)
    p_z[...] = _dot(hn_s[...], win_ref[:, o_z:o_z + w_ssd])
    dtr[...] = _dot_nt(wdtT_ref[...], hn_s[...])

    p_c[...] = _dot(hn_s[...], win_ref[:, o_c:o_c + 2 * w_conf])
    for rb in range(tl // ROW_BLOCK):
        r0 = rb * ROW_BLOCK
        buf_c[CONF_HALO + r0:CONF_HALO + r0 + ROW_BLOCK, :] = (
            p_c[r0:r0 + ROW_BLOCK, 0:w_conf] * jax.nn.sigmoid(p_c[r0:r0 + ROW_BLOCK, w_conf:2 * w_conf]))
    n_sh = shc.shape[1]
    for r in range(1, SUBLANES):
        shc[r - 1] = buf_c[r:r + n_sh, :]
    for rb in range(tl // CONV_ROWS):
        acc = _tile_rows(ccb_ref[...], CONV_ROWS)
        for j in range(CONF_KERNEL):
            a, r = divmod(CONF_HALO - (CONF_KERNEL - 1) + j, SUBLANES)
            base = a * SUBLANES + rb * CONV_ROWS
            src = buf_c[base:base + CONV_ROWS, :] if r == 0 else shc[r - 1, base:base + CONV_ROWS, :]
            acc = acc + _tile_rows(ccw_ref[j], CONV_ROWS) * src
        mu = jnp.mean(acc, axis=-1, keepdims=True)
        cen = acc - mu
        var = jnp.mean(cen * cen, axis=-1, keepdims=True)
        yc = cen * lax.rsqrt(var + EPS) * lng_ref[...] + lnb_ref[...]
        ycat[rb * CONV_ROWS:(rb + 1) * CONV_ROWS, w_a + w_ssd:w_a + w_ssd + w_conf] = _silu(yc).astype(BF16)

    p_a[...] = _dot(hn_s[...], win_ref[:, 0:3 * w_a])
    for rb in range(tl // ROW_BLOCK):
        r0 = rb * ROW_BLOCK
        b0 = hb + r0
        buf_a[b0:b0 + ROW_BLOCK, :] = p_a[r0:r0 + ROW_BLOCK, w_a:2 * w_a] * p_a[r0:r0 + ROW_BLOCK, 2 * w_a:3 * w_a]
        ya = jnp.zeros((ROW_BLOCK, w_a), F32)
        for j in range(A_CONV):
            s = b0 - (A_CONV - 1) + j
            ya = ya + _tile_rows(caw_ref[j], ROW_BLOCK) * buf_a[s:s + ROW_BLOCK, :]
        ycat[r0:r0 + ROW_BLOCK, 0:w_a] = (p_a[r0:r0 + ROW_BLOCK, 0:w_a] * ya).astype(BF16)

    a_row = -jnp.exp(alr_ref[...])
    ri = lax.broadcasted_iota(jnp.int32, (q, q), 0)
    ci = lax.broadcasted_iota(jnp.int32, (q, q), 1)
    causal = ri >= ci
    triu = jnp.where(ri <= ci, 1.0, 0.0).astype(BF16)
    hd = SSD_HEAD_DIM

    for c in range(tl // q):
        rows = slice(c * q, (c + 1) * q)
        dt_r = _softplus(dtr[:, rows] + dtbr_ref[...])
        acum_r = _dot_sel_right(dt_r * a_row, triu)
        to_end_r = dt_r * jnp.exp(acum_r[:, q - 1:q] - acum_r)
        acum_c = jnp.concatenate([acum_r, jnp.zeros((LANES - n_heads, q), F32)], axis=0).T
        for g in range(SSD_GROUPS):
            bg = xact[rows, w_ssd + g * SSD_STATE:w_ssd + (g + 1) * SSD_STATE]
            cg = xact[rows, w_ssd + n_bc + g * SSD_STATE:w_ssd + n_bc + (g + 1) * SSD_STATE].astype(BF16)
            cb = _dot_nt(cg, bg.astype(BF16))
            bgt = bg.T
            hg = hstate[g]
            yoff = _dot(cg, hg.astype(BF16))
            for r in range(hpg):
                h = g * hpg + r
                xh = xact[rows, h * hd:(h + 1) * hd].astype(BF16)
                acol = jnp.broadcast_to(acum_c[:, h:h + 1], (q, q))
                dec = jnp.where(causal, jnp.exp(acol - acum_r[h:h + 1, :]), 0.0)
                yd = _dot((cb * dec * dt_r[h:h + 1, :]).astype(BF16), xh)
                ysd[rows, h * hd:(h + 1) * hd] = yd + yoff[:, r * hd:(r + 1) * hd] * jnp.exp(acol[:, 0:hd])
                st = _dot((bgt * to_end_r[h:h + 1, :]).astype(BF16), xh)
                hstate[g, :, r * hd:(r + 1) * hd] = jnp.exp(acol[q - 1:q, 0:hd]) * hg[:, r * hd:(r + 1) * hd] + st
        xs = xact[rows, 0:w_ssd]
        yz = (ysd[rows, :] + dsk_ref[...] * xs) * _silu(p_z[rows, :])
        for g in range(SSD_GROUPS):
            yg = yz[:, g * gw:(g + 1) * gw]
            yn = yg * lax.rsqrt(jnp.mean(yg * yg, axis=-1, keepdims=True) + EPS) * snw_ref[:, g * gw:(g + 1) * gw]
            ycat[rows, w_a + g * gw:w_a + (g + 1) * gw] = yn.astype(BF16)

    out_ref[...] = x_ref[...] + _dot(ycat[...], wout_ref[...])


def _mixer_layer(x, norm_w, w_in, conv_a_w, conv_ssd_w, conv_ssd_b, dt_bias, a_log, d_skip, ssd_norm_w,
                 conv_conf_w, conv_conf_b, conf_ln_g, conf_ln_b, w_out, tl):
    bsz, seq, d = x.shape
    w_a = conv_a_w.shape[1]
    w_ssd = ssd_norm_w.shape[0]
    w_conf = conv_conf_w.shape[1]
    n_heads = dt_bias.shape[0]
    n_bc = SSD_GROUPS * SSD_STATE
    xbc = w_ssd + 2 * n_bc
    assert n_heads == SUBLANES and n_heads * SSD_HEAD_DIM == w_ssd and seq % tl == 0 and tl % SSD_CHUNK == 0
    assert tl % ROW_BLOCK == 0 and xbc % LANE_BLOCK == 0 and conv_a_w.shape[0] == A_CONV and conv_ssd_w.shape[0] == SSD_CONV
    assert conv_conf_w.shape[0] == CONF_KERNEL
    o_dt = 3 * w_a + w_ssd + xbc
    w_dt = w_in[:, o_dt:o_dt + n_heads]
    w_main = jnp.concatenate([w_in[:, :o_dt], w_in[:, o_dt + n_heads:]], axis=1).astype(BF16)
    n_cols = w_main.shape[1]
    w_dt_t = w_dt.T.astype(BF16)
    row = lambda v: v.reshape(1, -1).astype(F32)
    col = lambda v: v.reshape(-1, 1).astype(F32)
    taps = lambda w: jnp.broadcast_to(w.astype(F32)[:, None, :], (w.shape[0], SUBLANES, w.shape[1]))
    d_skip_x = jnp.repeat(d_skip.astype(F32), SSD_HEAD_DIM).reshape(1, w_ssd)

    const = lambda shape: pl.BlockSpec(shape, lambda s: (0,) * len(shape))
    n_l = seq // tl
    n_tiles = bsz * n_l
    dims = (tl, w_a, w_ssd, w_conf, n_heads, n_l)
    x2 = x.reshape(bsz * seq, d)
    out = pl.pallas_call(
        functools.partial(_mixer_kernel, dims),
        out_shape=jax.ShapeDtypeStruct((bsz * seq, d), F32),
        grid=(n_tiles,),
        in_specs=[
            pl.BlockSpec((tl, d), lambda s: (s, 0)),
            const((1, d)), const((d, n_cols)), const((n_heads, d)),
            const((A_CONV, SUBLANES, w_a)), const((SSD_CONV, SUBLANES, xbc)), const((SUBLANES, xbc)),
            const((n_heads, 1)), const((n_heads, 1)),
            const((1, w_ssd)), const((1, w_ssd)),
            const((CONF_KERNEL, SUBLANES, w_conf)), const((SUBLANES, w_conf)), const((1, w_conf)), const((1, w_conf)),
            const((d, d)),
        ],
        out_specs=pl.BlockSpec((tl, d), lambda s: (s, 0)),
        scratch_shapes=[
            pltpu.VMEM((tl, d), BF16),
            pltpu.VMEM((tl, 3 * w_a), F32),
            pltpu.VMEM((tl, 2 * w_conf), F32),
            pltpu.VMEM((SMALL_HALO + tl, xbc), F32),
            pltpu.VMEM((tl, w_ssd), F32),
            pltpu.VMEM((n_heads, tl), F32),
            pltpu.VMEM((SMALL_HALO + tl, w_a), F32),
            pltpu.VMEM((CONF_HALO + tl, w_conf), F32),
            pltpu.VMEM((SUBLANES - 1, tl + CONF_HALO - SUBLANES, w_conf), F32),
            pltpu.VMEM((tl, xbc), F32),
            pltpu.VMEM((tl, w_ssd), F32),
            pltpu.VMEM((tl, d), BF16),
            pltpu.VMEM((SSD_GROUPS, SSD_STATE, w_ssd // SSD_GROUPS), F32),
        ],
        compiler_params=pltpu.CompilerParams(
            dimension_semantics=("arbitrary",), vmem_limit_bytes=VMEM_LIMIT_BYTES),
        name="mixer",
    )(x2, row(norm_w), w_main, w_dt_t, taps(conv_a_w), taps(conv_ssd_w), taps(conv_ssd_b[None])[0],
      col(dt_bias), col(a_log), d_skip_x, row(ssd_norm_w),
      taps(conv_conf_w), taps(conv_conf_b[None])[0], row(conf_ln_g), row(conf_ln_b), w_out.astype(BF16))
    return out.reshape(bsz, seq, d)


def _dense_ffn_kernel(final_norm, n_cast, x_ref, nw_ref, wg_ref, wu_ref, wd_ref, fw_ref, *rest):
    cast_in, out_ref, cast_out, hn = rest[:n_cast], rest[n_cast], rest[n_cast + 1:2 * n_cast + 1], rest[-1]
    hn[...] = _rmsnorm(x_ref[...], nw_ref[...]).astype(BF16)
    a = (_silu(_dot(hn[...], wg_ref[...])) * _dot(hn[...], wu_ref[...])).astype(BF16)
    y = x_ref[...] + _dot(a, wd_ref[...])
    out_ref[...] = _rmsnorm(y, fw_ref[...]) if final_norm else y
    for src, dst in zip(cast_in, cast_out):
        dst[...] = src[...].astype(BF16)


def _dense_ffn(x2, norm_w, w_gate, w_up, w_down, final_w, tm, later=()):
    t, d = x2.shape
    ff = w_gate.shape[1]
    assert t % tm == 0
    n_steps = t // tm
    final_norm = final_w is not None
    fw = (final_w if final_norm else jnp.ones((d,), F32)).reshape(1, d).astype(F32)
    resident = lambda shape: pl.BlockSpec(shape, lambda i: (0, 0), pipeline_mode=pl.Buffered(1))
    bf16_rows = 2 * SUBLANES
    slabs = []
    for w in later:
        rows = w.size // w.shape[-1]
        assert rows % (n_steps * bf16_rows) == 0
        slabs.append((rows // n_steps, w.shape[-1]))
    slab_specs = [pl.BlockSpec(s, lambda i: (i, 0)) for s in slabs]
    res = pl.pallas_call(
        functools.partial(_dense_ffn_kernel, final_norm, len(later)),
        out_shape=[jax.ShapeDtypeStruct((t, d), F32)]
        + [jax.ShapeDtypeStruct((n_steps * r, c), BF16) for r, c in slabs],
        grid=(n_steps,),
        in_specs=[
            pl.BlockSpec((tm, d), lambda i: (i, 0)),
            resident((1, d)), resident((d, ff)), resident((d, ff)), resident((ff, d)), resident((1, d)),
        ] + slab_specs,
        out_specs=[pl.BlockSpec((tm, d), lambda i: (i, 0))] + slab_specs,
        scratch_shapes=[pltpu.VMEM((tm, d), BF16)],
        compiler_params=pltpu.CompilerParams(
            dimension_semantics=("arbitrary",), vmem_limit_bytes=VMEM_LIMIT_BYTES),
        name="dense_ffn",
    )(x2, norm_w.reshape(1, d).astype(F32), w_gate.astype(BF16), w_up.astype(BF16), w_down.astype(BF16), fw,
      *[w.reshape(n_steps * r, c) for w, (r, c) in zip(later, slabs)])
    return res[0], [b.reshape(w.shape) for b, w in zip(res[1:], later)]


def _rows_to_tiles(tile_ref, val):
    n, d = val.shape
    n_sub = d // LANES
    for c in range(n_sub):
        tile_ref[pl.ds(c, n, stride=n_sub), :] = val[:, c * LANES:(c + 1) * LANES]


def _tiles_to_rows(tile_ref, n_sub):
    n = tile_ref.shape[0] // n_sub
    return jnp.concatenate([tile_ref[pl.ds(c, n, stride=n_sub), :] for c in range(n_sub)], axis=1)


def _expert_ffn_kernel(blk_e_ref, nact_ref, xb_ref, wg_ref, wu_ref, wd_ref, out_ref, hb, acc):
    i = pl.program_id(0)
    j = pl.program_id(1)

    n_j = pl.num_programs(1)
    active = i < nact_ref[0]

    def partial_out():
        h = hb[...]
        a = (_silu(_dot(h, wg_ref[...])) * _dot(h, wu_ref[...])).astype(BF16)
        return _dot(a, wd_ref[...])

    @pl.when(active & (j == 0))
    def _():
        hb[...] = _tiles_to_rows(xb_ref, hb.shape[1] // LANES).astype(BF16)
        acc[...] = partial_out()

    @pl.when(active & (j > 0) & (j < n_j - 1))
    def _():
        acc[...] += partial_out()

    @pl.when(active & (j == n_j - 1))
    def _():
        _rows_to_tiles(out_ref, acc[...] + partial_out())

    @pl.when((i >= nact_ref[0]) & (j == 0))
    def _():
        out_ref[...] = jnp.zeros(out_ref.shape, F32)


def _expert_ffn(buf, blk_e, nact, w_gate, w_up, w_down, bm, tf):
    d, ff = w_gate.shape[1], w_gate.shape[2]
    n_sub = d // LANES
    cap = buf.shape[0] // n_sub
    n_blocks = cap // bm
    n_j = ff // tf
    assert cap % bm == 0 and ff % tf == 0 and n_j >= 2 and w_gate.dtype == BF16
    row_blk = lambda i, j, be, na: (jnp.minimum(i, na[0] - 1), 0)
    snake = lambda i, j: jnp.where(i % 2 == 0, j, n_j - 1 - j)
    col_j = lambda i, j, na: jnp.where(i < na[0], snake(i, j), snake(na[0] - 1, n_j - 1))
    return pl.pallas_call(
        _expert_ffn_kernel,
        out_shape=jax.ShapeDtypeStruct((cap * n_sub, LANES), F32),
        grid_spec=pltpu.PrefetchScalarGridSpec(
            num_scalar_prefetch=2,
            grid=(n_blocks, n_j),
            in_specs=[
                pl.BlockSpec((bm * n_sub, LANES), row_blk),
                pl.BlockSpec((None, d, tf), lambda i, j, be, na: (be[i], 0, col_j(i, j, na))),
                pl.BlockSpec((None, d, tf), lambda i, j, be, na: (be[i], 0, col_j(i, j, na))),
                pl.BlockSpec((None, tf, d), lambda i, j, be, na: (be[i], col_j(i, j, na), 0)),
            ],
            out_specs=pl.BlockSpec((bm * n_sub, LANES), lambda i, j, be, na: (i, 0)),
            scratch_shapes=[pltpu.VMEM((bm, d), BF16), pltpu.VMEM((bm, d), F32)],
        ),
        compiler_params=pltpu.CompilerParams(
            dimension_semantics=("arbitrary", "arbitrary"), vmem_limit_bytes=VMEM_LIMIT_BYTES),
        name="expert_ffn",
    )(blk_e, nact, buf, w_gate, w_up, w_down)


RANK_BLOCK = 256


def _router_kernel(x_ref, nw_ref, wr_ref, hn_ref, route_ref, route_t_ref, count_ref):
    h = _rmsnorm(x_ref[...], nw_ref[...])
    _rows_to_tiles(hn_ref, h)
    h_hi, h_lo, _ = _split3(h)
    w_hi, w_lo, _ = _split3(wr_ref[...])
    logits = _dot(h_hi, w_hi) + _dot(h_hi, w_lo) + _dot(h_lo, w_hi)
    lane = lax.broadcasted_iota(jnp.int32, logits.shape, 1)
    neg = jnp.float32(-jnp.inf)
    logits = jnp.where(lane < N_EXPERTS, logits, neg)
    m1 = jnp.max(logits, axis=-1, keepdims=True)
    i1 = jnp.min(jnp.where(logits == m1, lane, LANES), axis=-1, keepdims=True)
    rest = jnp.where(lane == i1, neg, logits)
    m2 = jnp.max(rest, axis=-1, keepdims=True)
    i2 = jnp.min(jnp.where(rest == m2, lane, LANES), axis=-1, keepdims=True)
    e2 = jnp.exp(m2 - m1)
    g1 = 1.0 / (1.0 + e2)
    g2 = e2 * g1
    pick1, pick2 = lane == i1, lane == i2
    onehot = jnp.where(pick1 | pick2, 1.0, 0.0)
    rb = RANK_BLOCK
    ri = lax.broadcasted_iota(jnp.int32, (rb, rb), 0)
    ci = lax.broadcasted_iota(jnp.int32, (rb, rb), 1)
    before = jnp.where(ri > ci, 1.0, 0.0).astype(BF16)
    total = jnp.zeros((1, LANES), F32)
    earlier = []
    for b in range(h.shape[0] // rb):
        oh_b = onehot[b * rb:(b + 1) * rb, :]
        earlier.append(_dot(before, oh_b.astype(BF16)) + total)
        total = total + jnp.sum(oh_b, axis=0, keepdims=True)
    earlier = jnp.concatenate(earlier, axis=0)
    r1 = jnp.sum(jnp.where(pick1, earlier, 0.0), axis=-1, keepdims=True)
    r2 = jnp.sum(jnp.where(pick2, earlier, 0.0), axis=-1, keepdims=True)
    cols = (i1.astype(F32), i2.astype(F32), g1, g2, r1, r2)
    route = jnp.zeros(logits.shape, F32)
    for c, v in enumerate(cols):
        route = jnp.where(lane == c, v, route)
    route_ref[...] = route
    route_t_ref[...] = route.T[0:SUBLANES, :]
    count_ref[...] = jnp.broadcast_to(total, count_ref.shape)


def _router(x2, norm_w, w_router, tm):
    t, d = x2.shape
    n_e = w_router.shape[1]
    assert n_e == N_EXPERTS and t % tm == 0 and d % LANES == 0 and tm % RANK_BLOCK == 0
    n_sub = d // LANES
    wr = jnp.pad(w_router.astype(F32), ((0, 0), (0, LANES - n_e)))
    return pl.pallas_call(
        _router_kernel,
        out_shape=(jax.ShapeDtypeStruct((t * n_sub, LANES), F32), jax.ShapeDtypeStruct((t, LANES), F32),
                   jax.ShapeDtypeStruct((SUBLANES, t), F32),
                   jax.ShapeDtypeStruct((t // tm * SUBLANES, LANES), F32)),
        grid=(t // tm,),
        in_specs=[
            pl.BlockSpec((tm, d), lambda i: (i, 0)),
            pl.BlockSpec((1, d), lambda i: (0, 0)),
            pl.BlockSpec((d, LANES), lambda i: (0, 0)),
        ],
        out_specs=(pl.BlockSpec((tm * n_sub, LANES), lambda i: (i, 0)),
                   pl.BlockSpec((tm, LANES), lambda i: (i, 0)),
                   pl.BlockSpec((SUBLANES, tm), lambda i: (0, i)),
                   pl.BlockSpec((SUBLANES, LANES), lambda i: (i, 0))),
        compiler_params=pltpu.CompilerParams(
            dimension_semantics=("arbitrary",), vmem_limit_bytes=VMEM_LIMIT_BYTES),
        name="router",
    )(x2, norm_w.reshape(1, d).astype(F32), wr)


def _row_copy(n_sub, src_ref, src_row, dst_ref, dst_row, sem):
    return pltpu.make_async_copy(src_ref.at[pl.ds(pl.multiple_of(src_row, n_sub), n_sub)],
                                 dst_ref.at[pl.ds(pl.multiple_of(dst_row, n_sub), n_sub)], sem)


def _tile_indices(dest, tile):
    k, t = dest.shape
    return dest.reshape(k, t // tile, tile).transpose(1, 0, 2).reshape(t // tile, 1, k * tile)


def _scatter_kernel(ts, n_sub, dest_ref, pad_ref, hn_ref, buf_ref, zero_tile, sem, pad_sem):
    @pl.when(pl.program_id(0) == 0)
    def _():
        zero_tile[...] = jnp.zeros(zero_tile.shape, F32)
        n_ranges = pad_ref.shape[0] // 2
        for e in range(n_ranges):
            first, count = pad_ref[e], pad_ref[n_ranges + e]

            def fill(p, carry):
                _row_copy(n_sub, zero_tile, 0, buf_ref, (first + p) * n_sub, pad_sem).start()
                return carry

            def fill_done(p, carry):
                _row_copy(n_sub, zero_tile, 0, buf_ref, 0, pad_sem).wait()
                return carry

            lax.fori_loop(0, count, fill, 0)
            lax.fori_loop(0, count, fill_done, 0)

    def issue(rb, carry):
        for u in range(DMA_UNROLL):
            r = rb * DMA_UNROLL + u
            for k in range(TOP_K):
                _row_copy(n_sub, hn_ref, r * n_sub, buf_ref, dest_ref[0, 0, k * ts + r], sem).start(priority=k % 2)
        return carry

    lax.fori_loop(0, ts // DMA_UNROLL, issue, 0)
    for k in range(TOP_K):
        pltpu.make_async_copy(hn_ref, buf_ref.at[pl.ds(0, ts * n_sub)], sem).wait()


def _scatter_rows(hn, dest, pad_slots, cap, ts, n_sub):
    t = hn.shape[0] // n_sub
    assert t % ts == 0 and ts % DMA_UNROLL == 0
    return pl.pallas_call(
        functools.partial(_scatter_kernel, ts, n_sub),
        out_shape=jax.ShapeDtypeStruct((cap * n_sub, LANES), F32),
        grid=(t // ts,),
        in_specs=[
            pl.BlockSpec((1, 1, TOP_K * ts), lambda i: (i, 0, 0), memory_space=pltpu.SMEM),
            pl.BlockSpec(memory_space=pltpu.SMEM),
            pl.BlockSpec((ts * n_sub, LANES), lambda i: (i, 0)),
        ],
        out_specs=pl.BlockSpec(memory_space=pl.ANY),
        scratch_shapes=[pltpu.VMEM((n_sub, LANES), F32), pltpu.SemaphoreType.DMA(()), pltpu.SemaphoreType.DMA(())],
        compiler_params=pltpu.CompilerParams(
            dimension_semantics=("arbitrary",), vmem_limit_bytes=VMEM_LIMIT_BYTES),
        name="scatter_rows",
    )(_tile_indices(dest, ts), pad_slots, hn)


def _combine_kernel(tc, final_norm, dest_ref, dest_next_ref, x_ref, route_ref, fw_ref, ybuf_ref, out_ref, rows, sems):
    n_sub = x_ref.shape[1] // LANES
    i = pl.program_id(0)
    slot = i % 2

    def gather(idx_ref, s):
        def issue(rb, carry):
            for u in range(DMA_UNROLL):
                r = rb * DMA_UNROLL + u
                for k in range(TOP_K):
                    _row_copy(n_sub, ybuf_ref, idx_ref[0, 0, k * tc + r], rows.at[s, k], r * n_sub,
                              sems.at[s]).start(priority=k % 2)
            return carry

        lax.fori_loop(0, tc // DMA_UNROLL, issue, 0)

    @pl.when(i == 0)
    def _():
        gather(dest_ref, 0)

    @pl.when(i + 1 < pl.num_programs(0))
    def _():
        gather(dest_next_ref, 1 - slot)

    for k in range(TOP_K):
        pltpu.make_async_copy(ybuf_ref.at[pl.ds(0, tc * n_sub)], rows.at[slot, k], sems.at[slot]).wait()

    route = route_ref[...]
    y = x_ref[...]
    for k in range(TOP_K):
        y = y + route[:, TOP_K + k:TOP_K + k + 1] * _tiles_to_rows(rows.at[slot, k], n_sub)
    out_ref[...] = _rmsnorm(y, fw_ref[...]) if final_norm else y


def _combine(x2, route, dest, ybuf, final_w, tc):
    t, d = x2.shape
    assert t % tc == 0 and tc % DMA_UNROLL == 0
    n_tiles = t // tc
    final_norm = final_w is not None
    fw = (final_w if final_norm else jnp.ones((d,), F32)).reshape(1, d).astype(F32)
    idx = _tile_indices(dest, tc)
    return pl.pallas_call(
        functools.partial(_combine_kernel, tc, final_norm),
        out_shape=jax.ShapeDtypeStruct((t, d), F32),
        grid=(n_tiles,),
        in_specs=[
            pl.BlockSpec((1, 1, TOP_K * tc), lambda i: (i, 0, 0), memory_space=pltpu.SMEM),
            pl.BlockSpec((1, 1, TOP_K * tc), lambda i: (jnp.minimum(i + 1, n_tiles - 1), 0, 0), memory_space=pltpu.SMEM),
            pl.BlockSpec((tc, d), lambda i: (i, 0)),
            pl.BlockSpec((tc, LANES), lambda i: (i, 0)),
            pl.BlockSpec((1, d), lambda i: (0, 0)),
            pl.BlockSpec(memory_space=pl.ANY),
        ],
        out_specs=pl.BlockSpec((tc, d), lambda i: (i, 0)),
        scratch_shapes=[pltpu.VMEM((2, TOP_K, tc * (d // LANES), LANES), F32), pltpu.SemaphoreType.DMA((2,))],
        compiler_params=pltpu.CompilerParams(
            dimension_semantics=("arbitrary",), vmem_limit_bytes=VMEM_LIMIT_BYTES),
        name="combine",
    )(idx, idx, x2, route, fw, ybuf)


def _moe_layer(x2, norm_w, w_router, w_gate, w_up, w_down, final_w, tiles):
    t, d = x2.shape
    bm = tiles["moe_bm"]
    tm = tiles["router_tm"]
    hn, route, route_t, counts = _router(x2, norm_w, w_router, tm)
    counts = counts[::SUBLANES, :N_EXPERTS].astype(jnp.int32)
    sizes = jnp.sum(counts, axis=0)
    padded = ((sizes + bm - 1) // bm) * bm
    ends = jnp.cumsum(padded)
    starts = ends - padded
    tile_base = starts[None, :] + jnp.cumsum(counts, axis=0) - counts
    e_idx = route_t[0:TOP_K].astype(jnp.int32).reshape(TOP_K, t // tm, tm)
    rank = route_t[2 * TOP_K:3 * TOP_K].astype(jnp.int32).reshape(TOP_K, t // tm, tm)
    base = jnp.zeros_like(rank)
    for e in range(N_EXPERTS):
        base = jnp.where(e_idx == e, tile_base[None, :, e:e + 1], base)
    dest = (base + rank).reshape(TOP_K, t)
    cap = TOP_K * t + N_EXPERTS * bm
    n_blocks = cap // bm
    first_slot = jnp.arange(n_blocks, dtype=jnp.int32) * bm
    blk_e = jnp.minimum(jnp.sum(first_slot[:, None] >= ends[None, :], axis=1), N_EXPERTS - 1).astype(jnp.int32)
    nact = (ends[-1:] // bm).astype(jnp.int32)

    n_sub = d // LANES
    dest_rows = dest * n_sub
    pad_slots = jnp.concatenate([starts + sizes, ends[-1:], padded - sizes, cap - ends[-1:]]).astype(jnp.int32)
    buf = _scatter_rows(hn, dest_rows, pad_slots, cap, tiles["scatter_ts"], n_sub)
    ybuf = _expert_ffn(buf, blk_e, nact, w_gate, w_up, w_down, bm, tiles["moe_tf"])
    return _combine(x2, route, dest_rows, ybuf, final_w, tiles["combine_tc"])


TILES = {
    "mixer_tl": 512,
    "ffn_tm": 512,
    "router_tm": 1024,
    "moe_bm": 512, "moe_tf": 1792,
    "scatter_ts": 1024,
    "combine_tc": 512,
}


def kernel(x, norm_mix, w_in, conv_a_w, conv_ssd_w, conv_ssd_b, dt_bias, a_log, d_skip, ssd_norm_w, conv_conf_w, conv_conf_b, conf_ln_g, conf_ln_b, w_out, norm_ffn, ffn_w_gate, ffn_w_up, ffn_w_down, moe_router, moe_w_gate, moe_w_up, moe_w_down, norm_final):
    bsz, seq, d = x.shape
    depth = norm_mix.shape[0]
    tiles = dict(TILES)
    tiles["mixer_tl"] = min(tiles["mixer_tl"], seq)
    expert_w = None
    for i in range(depth):
        x = _mixer_layer(x, norm_mix[i], w_in[i], conv_a_w[i], conv_ssd_w[i], conv_ssd_b[i], dt_bias[i], a_log[i],
                         d_skip[i], ssd_norm_w[i], conv_conf_w[i], conv_conf_b[i], conf_ln_g[i], conf_ln_b[i],
                         w_out[i], tiles["mixer_tl"])
        x2 = x.reshape(bsz * seq, d)
        final_w = norm_final if i == depth - 1 else None
        j = i // 2
        if i % 2 == 0:
            later = (moe_w_gate[j], moe_w_up[j], moe_w_down[j]) if i + 1 < depth else ()
            x2, expert_w = _dense_ffn(x2, norm_ffn[i], ffn_w_gate[j], ffn_w_up[j], ffn_w_down[j], final_w,
                                      tiles["ffn_tm"], later)
        else:
            if not expert_w:
                expert_w = [w.astype(BF16) for w in (moe_w_gate[j], moe_w_up[j], moe_w_down[j])]
            x2 = _moe_layer(x2, norm_ffn[i], moe_router[j], *expert_w, final_w, tiles)
            expert_w = None
        x = x2.reshape(bsz, seq, d)
    return x
```

```python
import functools

import jax
import jax.numpy as jnp
from jax import lax
from jax.experimental import pallas as pl
from jax.experimental.pallas import tpu as pltpu

F32 = jnp.float32
BF16 = jnp.bfloat16
EPS = 1e-5

LANES = 128
SUBLANES = 8
VMEM_LIMIT_BYTES = 56 * 1024 * 1024

A_CONV = 3
SSD_HEAD_DIM = 64
SSD_GROUPS = 2
SSD_STATE = 128
SSD_CONV = 4
CONF_KERNEL = 31
N_EXPERTS = 8
TOP_K = 2

SSD_CHUNK = 128
CONF_HALO = 32
SMALL_HALO = SUBLANES
CONV_ROWS = 32
ROW_BLOCK = 64
LANE_BLOCK = 256
DMA_UNROLL = 16


def _split3(v):
    hi = v.astype(BF16)
    r1 = v - hi.astype(F32)
    mid = r1.astype(BF16)
    lo = (r1 - mid.astype(F32)).astype(BF16)
    return hi, mid, lo


def _dot(a, b):
    return jnp.dot(a, b, preferred_element_type=F32)


def _dot_nt(a, b):
    return lax.dot_general(a, b, (((1,), (1,)), ((), ())), preferred_element_type=F32)


def _dot_sel_right(v, sel):
    hi, mid, lo = _split3(v)
    return _dot(hi, sel) + _dot(mid, sel) + _dot(lo, sel)


def _softplus(v):
    return jnp.maximum(v, 0.0) + jnp.log1p(jnp.exp(-jnp.abs(v)))


def _silu(v):
    return v * jax.nn.sigmoid(v)


def _tile_rows(w8, rows):
    return jnp.tile(w8, (rows // SUBLANES, 1))


def _rmsnorm(x, w):
    return x * lax.rsqrt(jnp.mean(x * x, axis=-1, keepdims=True) + EPS) * w


def _mixer_kernel(dims, x_ref, nw_ref, win_ref, wdtT_ref, caw_ref, csw_ref, csb_ref, dtbr_ref,
                  alr_ref, dsk_ref, snw_ref, ccw_ref, ccb_ref, lng_ref, lnb_ref, wout_ref,
                  out_ref,
                  hn_s, p_a, p_c, p_s, p_z, dtr, buf_a, buf_c, shc, xact, ysd, ycat, hstate):
    tl, w_a, w_ssd, w_conf, n_heads, n_l = dims
    q = SSD_CHUNK
    n_bc = SSD_GROUPS * SSD_STATE
    xbc = w_ssd + 2 * n_bc
    hpg = n_heads // SSD_GROUPS
    gw = hpg * SSD_HEAD_DIM
    hb = SMALL_HALO
    o_z = 3 * w_a
    o_xbc = o_z + w_ssd
    o_c = o_xbc + xbc
    lt = lax.rem(pl.program_id(0), n_l)

    @pl.when(lt == 0)
    def _():
        buf_a[0:hb, :] = jnp.zeros((hb, w_a), F32)
        p_s[0:hb, :] = jnp.zeros((hb, xbc), F32)
        buf_c[0:CONF_HALO, :] = jnp.zeros((CONF_HALO, w_conf), F32)
        hstate[...] = jnp.zeros(hstate.shape, F32)

    @pl.when(lt > 0)
    def _():
        buf_a[0:hb, :] = buf_a[tl:tl + hb, :]
        p_s[0:hb, :] = p_s[tl:tl + hb, :]
        buf_c[0:CONF_HALO, :] = buf_c[tl:tl + CONF_HALO, :]

    hn_s[...] = _rmsnorm(x_ref[...], nw_ref[...]).astype(BF16)

    p_s[hb:hb + tl, :] = _dot(hn_s[...], win_ref[:, o_xbc:o_xbc + xbc])
    for rb in range(tl // ROW_BLOCK):
        for lb in range(xbc // LANE_BLOCK):
            cols = slice(lb * LANE_BLOCK, (lb + 1) * LANE_BLOCK)
            xc = _tile_rows(csb_ref[:, cols], ROW_BLOCK)
            for j in range(SSD_CONV):
                s = hb + rb * ROW_BLOCK - (SSD_CONV - 1) + j
                xc = xc + _tile_rows(csw_ref[j, :, cols], ROW_BLOCK) * p_s[s:s + ROW_BLOCK, cols]
            xact[rb * ROW_BLOCK:(rb + 1) * ROW_BLOCK, cols] = _silu(xc)
    p_z[...] = _dot(hn_s[...], win_ref[:, o_z:o_z + w_ssd])
    dtr[...] = _dot_nt(wdtT_ref[...], hn_s[...])

    p_c[...] = _dot(hn_s[...], win_ref[:, o_c:o_c + 2 * w_conf])
    for rb in range(tl // ROW_BLOCK):
        r0 = rb * ROW_BLOCK
        buf_c[CONF_HALO + r0:CONF_HALO + r0 + ROW_BLOCK, :] = (
            p_c[r0:r0 + ROW_BLOCK, 0:w_conf] * jax.nn.sigmoid(p_c[r0:r0 + ROW_BLOCK, w_conf:2 * w_conf]))
    n_sh = shc.shape[1]
    for r in range(1, SUBLANES):
        shc[r - 1] = buf_c[r:r + n_sh, :]
    for rb in range(tl // CONV_ROWS):
        acc = _tile_rows(ccb_ref[...], CONV_ROWS)
        for j in range(CONF_KERNEL):
            a, r = divmod(CONF_HALO - (CONF_KERNEL - 1) + j, SUBLANES)
            base = a * SUBLANES + rb * CONV_ROWS
            src = buf_c[base:base + CONV_ROWS, :] if r == 0 else shc[r - 1, base:base + CONV_ROWS, :]
            acc = acc + _tile_rows(ccw_ref[j], CONV_ROWS) * src
        mu = jnp.mean(acc, axis=-1, keepdims=True)
        cen = acc - mu
        var = jnp.mean(cen * cen, axis=-1, keepdims=True)
        yc = cen * lax.rsqrt(var + EPS) * lng_ref[...] + lnb_ref[...]
        ycat[rb * CONV_ROWS:(rb + 1) * CONV_ROWS, w_a + w_ssd:w_a + w_ssd + w_conf] = _silu(yc).astype(BF16)

    p_a[...] = _dot(hn_s[...], win_ref[:, 0:3 * w_a])
    for rb in range(tl // ROW_BLOCK):
        r0 = rb * ROW_BLOCK
        b0 = hb + r0
        buf_a[b0:b0 + ROW_BLOCK, :] = p_a[r0:r0 + ROW_BLOCK, w_a:2 * w_a] * p_a[r0:r0 + ROW_BLOCK, 2 * w_a:3 * w_a]
        ya = jnp.zeros((ROW_BLOCK, w_a), F32)
        for j in range(A_CONV):
            s = b0 - (A_CONV - 1) + j
            ya = ya + _tile_rows(caw_ref[j], ROW_BLOCK) * buf_a[s:s + ROW_BLOCK, :]
        ycat[r0:r0 + ROW_BLOCK, 0:w_a] = (p_a[r0:r0 + ROW_BLOCK, 0:w_a] * ya).astype(BF16)

    a_row = -jnp.exp(alr_ref[...])
    ri = lax.broadcasted_iota(jnp.int32, (q, q), 0)
    ci = lax.broadcasted_iota(jnp.int32, (q, q), 1)
    causal = ri >= ci
    triu = jnp.where(ri <= ci, 1.0, 0.0).astype(BF16)
    hd = SSD_HEAD_DIM

    for c in range(tl // q):
        rows = slice(c * q, (c + 1) * q)
        dt_r = _softplus(dtr[:, rows] + dtbr_ref[...])
        acum_r = _dot_sel_right(dt_r * a_row, triu)
        to_end_r = dt_r * jnp.exp(acum_r[:, q - 1:q] - acum_r)
        acum_c = jnp.concatenate([acum_r, jnp.zeros((LANES - n_heads, q), F32)], axis=0).T
        for g in range(SSD_GROUPS):
            bg = xact[rows, w_ssd + g * SSD_STATE:w_ssd + (g + 1) * SSD_STATE]
            cg = xact[rows, w_ssd + n_bc + g * SSD_STATE:w_ssd + n_bc + (g + 1) * SSD_STATE].astype(BF16)
            cb = _dot_nt(cg, bg.astype(BF16))
            bgt = bg.T
            hg = hstate[g]
            yoff = _dot(cg, hg.astype(BF16))
            for r in range(hpg):
                h = g * hpg + r
                xh = xact[rows, h * hd:(h + 1) * hd].astype(BF16)
                acol = jnp.broadcast_to(acum_c[:, h:h + 1], (q, q))
                dec = jnp.where(causal, jnp.exp(acol - acum_r[h:h + 1, :]), 0.0)
                yd = _dot((cb * dec * dt_r[h:h + 1, :]).astype(BF16), xh)
                ysd[rows, h * hd:(h + 1) * hd] = yd + yoff[:, r * hd:(r + 1) * hd] * jnp.exp(acol[:, 0:hd])
                st = _dot((bgt * to_end_r[h:h + 1, :]).astype(BF16), xh)
                hstate[g, :, r * hd:(r + 1) * hd] = jnp.exp(acol[q - 1:q, 0:hd]) * hg[:, r * hd:(r + 1) * hd] + st
        xs = xact[rows, 0:w_ssd]
        yz = (ysd[rows, :] + dsk_ref[...] * xs) * _silu(p_z[rows, :])
        for g in range(SSD_GROUPS):
            yg = yz[:, g * gw:(g + 1) * gw]
            yn = yg * lax.rsqrt(jnp.mean(yg * yg, axis=-1, keepdims=True) + EPS) * snw_ref[:, g * gw:(g + 1) * gw]
            ycat[rows, w_a + g * gw:w_a + (g + 1) * gw] = yn.astype(BF16)

    out_ref[...] = x_ref[...] + _dot(ycat[...], wout_ref[...])


def _mixer_layer(x, norm_w, w_in, conv_a_w, conv_ssd_w, conv_ssd_b, dt_bias, a_log, d_skip, ssd_norm_w,
                 conv_conf_w, conv_conf_b, conf_ln_g, conf_ln_b, w_out, tl):
    bsz, seq, d = x.shape
    w_a = conv_a_w.shape[1]
    w_ssd = ssd_norm_w.shape[0]
    w_conf = conv_conf_w.shape[1]
    n_heads = dt_bias.shape[0]
    n_bc = SSD_GROUPS * SSD_STATE
    xbc = w_ssd + 2 * n_bc
    assert n_heads == SUBLANES and n_heads * SSD_HEAD_DIM == w_ssd and seq % tl == 0 and tl % SSD_CHUNK == 0
    assert tl % ROW_BLOCK == 0 and xbc % LANE_BLOCK == 0 and conv_a_w.shape[0] == A_CONV and conv_ssd_w.shape[0] == SSD_CONV
    assert conv_conf_w.shape[0] == CONF_KERNEL
    o_dt = 3 * w_a + w_ssd + xbc
    w_dt = w_in[:, o_dt:o_dt + n_heads]
    w_main = jnp.concatenate([w_in[:, :o_dt], w_in[:, o_dt + n_heads:]], axis=1).astype(BF16)
    n_cols = w_main.shape[1]
    w_dt_t = w_dt.T.astype(BF16)
    row = lambda v: v.reshape(1, -1).astype(F32)
    col = lambda v: v.reshape(-1, 1).astype(F32)
    taps = lambda w: jnp.broadcast_to(w.astype(F32)[:, None, :], (w.shape[0], SUBLANES, w.shape[1]))
    d_skip_x = jnp.repeat(d_skip.astype(F32), SSD_HEAD_DIM).reshape(1, w_ssd)

    const = lambda shape: pl.BlockSpec(shape, lambda s: (0,) * len(shape))
    n_l = seq // tl
    n_tiles = bsz * n_l
    dims = (tl, w_a, w_ssd, w_conf, n_heads, n_l)
    x2 = x.reshape(bsz * seq, d)
    out = pl.pallas_call(
        functools.partial(_mixer_kernel, dims),
        out_shape=jax.ShapeDtypeStruct((bsz * seq, d), F32),
        grid=(n_tiles,),
        in_specs=[
            pl.BlockSpec((tl, d), lambda s: (s, 0)),
            const((1, d)), const((d, n_cols)), const((n_heads, d)),
            const((A_CONV, SUBLANES, w_a)), const((SSD_CONV, SUBLANES, xbc)), const((SUBLANES, xbc)),
            const((n_heads, 1)), const((n_heads, 1)),
            const((1, w_ssd)), const((1, w_ssd)),
            const((CONF_KERNEL, SUBLANES, w_conf)), const((SUBLANES, w_conf)), const((1, w_conf)), const((1, w_conf)),
            const((d, d)),
        ],
        out_specs=pl.BlockSpec((tl, d), lambda s: (s, 0)),
        scratch_shapes=[
            pltpu.VMEM((tl, d), BF16),
            pltpu.VMEM((tl, 3 * w_a), F32),
            pltpu.VMEM((tl, 2 * w_conf), F32),
            pltpu.VMEM((SMALL_HALO + tl, xbc), F32),
            pltpu.VMEM((tl, w_ssd), F32),
            pltpu.VMEM((n_heads, tl), F32),
            pltpu.VMEM((SMALL_HALO + tl, w_a), F32),
            pltpu.VMEM((CONF_HALO + tl, w_conf), F32),
            pltpu.VMEM((SUBLANES - 1, tl + CONF_HALO - SUBLANES, w_conf), F32),
            pltpu.VMEM((tl, xbc), F32),
            pltpu.VMEM((tl, w_ssd), F32),
            pltpu.VMEM((tl, d), BF16),
            pltpu.VMEM((SSD_GROUPS, SSD_STATE, w_ssd // SSD_GROUPS), F32),
        ],
        compiler_params=pltpu.CompilerParams(
            dimension_semantics=("arbitrary",), vmem_limit_bytes=VMEM_LIMIT_BYTES),
        name="mixer",
    )(x2, row(norm_w), w_main, w_dt_t, taps(conv_a_w), taps(conv_ssd_w), taps(conv_ssd_b[None])[0],
      col(dt_bias), col(a_log), d_skip_x, row(ssd_norm_w),
      taps(conv_conf_w), taps(conv_conf_b[None])[0], row(conf_ln_g), row(conf_ln_b), w_out.astype(BF16))
    return out.reshape(bsz, seq, d)


def _dense_ffn_kernel(final_norm, n_cast, x_ref, nw_ref, wg_ref, wu_ref, wd_ref, fw_ref, *rest):
    cast_in, out_ref, cast_out, hn = rest[:n_cast], rest[n_cast], rest[n_cast + 1:2 * n_cast + 1], rest[-1]
    hn[...] = _rmsnorm(x_ref[...], nw_ref[...]).astype(BF16)
    a = (_silu(_dot(hn[...], wg_ref[...])) * _dot(hn[...], wu_ref[...])).astype(BF16)
    y = x_ref[...] + _dot(a, wd_ref[...])
    out_ref[...] = _rmsnorm(y, fw_ref[...]) if final_norm else y
    for src, dst in zip(cast_in, cast_out):
        dst[...] = src[...].astype(BF16)


def _dense_ffn(x2, norm_w, w_gate, w_up, w_down, final_w, tm, later=()):
    t, d = x2.shape
    ff = w_gate.shape[1]
    assert t % tm == 0
    n_steps = t // tm
    final_norm = final_w is not None
    fw = (final_w if final_norm else jnp.ones((d,), F32)).reshape(1, d).astype(F32)
    resident = lambda shape: pl.BlockSpec(shape, lambda i: (0, 0), pipeline_mode=pl.Buffered(1))
    bf16_rows = 2 * SUBLANES
    slabs = []
    for w in later:
        rows = w.size // w.shape[-1]
        assert rows % (n_steps * bf16_rows) == 0
        slabs.append((rows // n_steps, w.shape[-1]))
    slab_specs = [pl.BlockSpec(s, lambda i: (i, 0)) for s in slabs]
    res = pl.pallas_call(
        functools.partial(_dense_ffn_kernel, final_norm, len(later)),
        out_shape=[jax.ShapeDtypeStruct((t, d), F32)]
        + [jax.ShapeDtypeStruct((n_steps * r, c), BF16) for r, c in slabs],
        grid=(n_steps,),
        in_specs=[
            pl.BlockSpec((tm, d), lambda i: (i, 0)),
            resident((1, d)), resident((d, ff)), resident((d, ff)), resident((ff, d)), resident((1, d)),
        ] + slab_specs,
        out_specs=[pl.BlockSpec((tm, d), lambda i: (i, 0))] + slab_specs,
        scratch_shapes=[pltpu.VMEM((tm, d), BF16)],
        compiler_params=pltpu.CompilerParams(
            dimension_semantics=("arbitrary",), vmem_limit_bytes=VMEM_LIMIT_BYTES),
        name="dense_ffn",
    )(x2, norm_w.reshape(1, d).astype(F32), w_gate.astype(BF16), w_up.astype(BF16), w_down.astype(BF16), fw,
      *[w.reshape(n_steps * r, c) for w, (r, c) in zip(later, slabs)])
    return res[0], [b.reshape(w.shape) for b, w in zip(res[1:], later)]


def _rows_to_tiles(tile_ref, val):
    n, d = val.shape
    n_sub = d // LANES
    for c in range(n_sub):
        tile_ref[pl.ds(c, n, stride=n_sub), :] = val[:, c * LANES:(c + 1) * LANES]


def _tiles_to_rows(tile_ref, n_sub):
    n = tile_ref.shape[0] // n_sub
    return jnp.concatenate([tile_ref[pl.ds(c, n, stride=n_sub), :] for c in range(n_sub)], axis=1)


def _expert_ffn_kernel(blk_e_ref, nact_ref, xb_ref, wg_ref, wu_ref, wd_ref, out_ref, hb, acc):
    i = pl.program_id(0)
    j = pl.program_id(1)

    n_j = pl.num_programs(1)
    active = i < nact_ref[0]

    def partial_out():
        h = hb[...]
        a = (_silu(_dot(h, wg_ref[...])) * _dot(h, wu_ref[...])).astype(BF16)
        return _dot(a, wd_ref[...])

    @pl.when(active & (j == 0))
    def _():
        hb[...] = _tiles_to_rows(xb_ref, hb.shape[1] // LANES).astype(BF16)
        acc[...] = partial_out()

    @pl.when(active & (j > 0) & (j < n_j - 1))
    def _():
        acc[...] += partial_out()

    @pl.when(active & (j == n_j - 1))
    def _():
        _rows_to_tiles(out_ref, acc[...] + partial_out())

    @pl.when((i >= nact_ref[0]) & (j == 0))
    def _():
        out_ref[...] = jnp.zeros(out_ref.shape, F32)


def _expert_ffn(buf, blk_e, nact, w_gate, w_up, w_down, bm, tf):
    d, ff = w_gate.shape[1], w_gate.shape[2]
    n_sub = d // LANES
    cap = buf.shape[0] // n_sub
    n_blocks = cap // bm
    n_j = ff // tf
    assert cap % bm == 0 and ff % tf == 0 and n_j >= 2 and w_gate.dtype == BF16
    row_blk = lambda i, j, be, na: (jnp.minimum(i, na[0] - 1), 0)
    snake = lambda i, j: jnp.where(i % 2 == 0, j, n_j - 1 - j)
    col_j = lambda i, j, na: jnp.where(i < na[0], snake(i, j), snake(na[0] - 1, n_j - 1))
    return pl.pallas_call(
        _expert_ffn_kernel,
        out_shape=jax.ShapeDtypeStruct((cap * n_sub, LANES), F32),
        grid_spec=pltpu.PrefetchScalarGridSpec(
            num_scalar_prefetch=2,
            grid=(n_blocks, n_j),
            in_specs=[
                pl.BlockSpec((bm * n_sub, LANES), row_blk),
                pl.BlockSpec((None, d, tf), lambda i, j, be, na: (be[i], 0, col_j(i, j, na))),
                pl.BlockSpec((None, d, tf), lambda i, j, be, na: (be[i], 0, col_j(i, j, na))),
                pl.BlockSpec((None, tf, d), lambda i, j, be, na: (be[i], col_j(i, j, na), 0)),
            ],
            out_specs=pl.BlockSpec((bm * n_sub, LANES), lambda i, j, be, na: (i, 0)),
            scratch_shapes=[pltpu.VMEM((bm, d), BF16), pltpu.VMEM((bm, d), F32)],
        ),
        compiler_params=pltpu.CompilerParams(
            dimension_semantics=("arbitrary", "arbitrary"), vmem_limit_bytes=VMEM_LIMIT_BYTES),
        name="expert_ffn",
    )(blk_e, nact, buf, w_gate, w_up, w_down)


RANK_BLOCK = 256


def _router_kernel(x_ref, nw_ref, wr_ref, hn_ref, route_ref, route_t_ref, count_ref):
    h = _rmsnorm(x_ref[...], nw_ref[...])
    _rows_to_tiles(hn_ref, h)
    h_hi, h_lo, _ = _split3(h)
    w_hi, w_lo, _ = _split3(wr_ref[...])
    logits = _dot(h_hi, w_hi) + _dot(h_hi, w_lo) + _dot(h_lo, w_hi)
    lane = lax.broadcasted_iota(jnp.int32, logits.shape, 1)
    neg = jnp.float32(-jnp.inf)
    logits = jnp.where(lane < N_EXPERTS, logits, neg)
    m1 = jnp.max(logits, axis=-1, keepdims=True)
    i1 = jnp.min(jnp.where(logits == m1, lane, LANES), axis=-1, keepdims=True)
    rest = jnp.where(lane == i1, neg, logits)
    m2 = jnp.max(rest, axis=-1, keepdims=True)
    i2 = jnp.min(jnp.where(rest == m2, lane, LANES), axis=-1, keepdims=True)
    e2 = jnp.exp(m2 - m1)
    g1 = 1.0 / (1.0 + e2)
    g2 = e2 * g1
    pick1, pick2 = lane == i1, lane == i2
    onehot = jnp.where(pick1 | pick2, 1.0, 0.0)
    rb = RANK_BLOCK
    ri = lax.broadcasted_iota(jnp.int32, (rb, rb), 0)
    ci = lax.broadcasted_iota(jnp.int32, (rb, rb), 1)
    before = jnp.where(ri > ci, 1.0, 0.0).astype(BF16)
    total = jnp.zeros((1, LANES), F32)
    earlier = []
    for b in range(h.shape[0] // rb):
        oh_b = onehot[b * rb:(b + 1) * rb, :]
        earlier.append(_dot(before, oh_b.astype(BF16)) + total)
        total = total + jnp.sum(oh_b, axis=0, keepdims=True)
    earlier = jnp.concatenate(earlier, axis=0)
    r1 = jnp.sum(jnp.where(pick1, earlier, 0.0), axis=-1, keepdims=True)
    r2 = jnp.sum(jnp.where(pick2, earlier, 0.0), axis=-1, keepdims=True)
    cols = (i1.astype(F32), i2.astype(F32), g1, g2, r1, r2)
    route = jnp.zeros(logits.shape, F32)
    for c, v in enumerate(cols):
        route = jnp.where(lane == c, v, route)
    route_ref[...] = route
    route_t_ref[...] = route.T[0:SUBLANES, :]
    count_ref[...] = jnp.broadcast_to(total, count_ref.shape)


def _router(x2, norm_w, w_router, tm):
    t, d = x2.shape
    n_e = w_router.shape[1]
    assert n_e == N_EXPERTS and t % tm == 0 and d % LANES == 0 and tm % RANK_BLOCK == 0
    n_sub = d // LANES
    wr = jnp.pad(w_router.astype(F32), ((0, 0), (0, LANES - n_e)))
    return pl.pallas_call(
        _router_kernel,
        out_shape=(jax.ShapeDtypeStruct((t * n_sub, LANES), F32), jax.ShapeDtypeStruct((t, LANES), F32),
                   jax.ShapeDtypeStruct((SUBLANES, t), F32),
                   jax.ShapeDtypeStruct((t // tm * SUBLANES, LANES), F32)),
        grid=(t // tm,),
        in_specs=[
            pl.BlockSpec((tm, d), lambda i: (i, 0)),
            pl.BlockSpec((1, d), lambda i: (0, 0)),
            pl.BlockSpec((d, LANES), lambda i: (0, 0)),
        ],
        out_specs=(pl.BlockSpec((tm * n_sub, LANES), lambda i: (i, 0)),
                   pl.BlockSpec((tm, LANES), lambda i: (i, 0)),
                   pl.BlockSpec((SUBLANES, tm), lambda i: (0, i)),
                   pl.BlockSpec((SUBLANES, LANES), lambda i: (i, 0))),
        compiler_params=pltpu.CompilerParams(
            dimension_semantics=("arbitrary",), vmem_limit_bytes=VMEM_LIMIT_BYTES),
        name="router",
    )(x2, norm_w.reshape(1, d).astype(F32), wr)


def _row_copy(n_sub, src_ref, src_row, dst_ref, dst_row, sem):
    return pltpu.make_async_copy(src_ref.at[pl.ds(pl.multiple_of(src_row, n_sub), n_sub)],
                                 dst_ref.at[pl.ds(pl.multiple_of(dst_row, n_sub), n_sub)], sem)


def _tile_indices(dest, tile):
    k, t = dest.shape
    return dest.reshape(k, t // tile, tile).transpose(1, 0, 2).reshape(t // tile, 1, k * tile)


def _scatter_kernel(ts, n_sub, dest_ref, pad_ref, hn_ref, hn_hbm_ref, buf_ref, zero_tile, sem, hbm_sem, pad_sem):
    @pl.when(pl.program_id(0) == 0)
    def _():
        zero_tile[...] = jnp.zeros(zero_tile.shape, F32)
        n_ranges = pad_ref.shape[0] // 2
        for e in range(n_ranges):
            first, count = pad_ref[e], pad_ref[n_ranges + e]

            def fill(p, carry):
                _row_copy(n_sub, zero_tile, 0, buf_ref, (first + p) * n_sub, pad_sem).start()
                return carry

            def fill_done(p, carry):
                _row_copy(n_sub, zero_tile, 0, buf_ref, 0, pad_sem).wait()
                return carry

            lax.fori_loop(0, count, fill, 0)
            lax.fori_loop(0, count, fill_done, 0)

    tile_row0 = pl.program_id(0) * ts * n_sub

    def issue(rb, carry):
        for u in range(DMA_UNROLL):
            r = rb * DMA_UNROLL + u
            _row_copy(n_sub, hn_ref, r * n_sub, buf_ref, dest_ref[0, 0, r], sem).start()
            _row_copy(n_sub, hn_hbm_ref, tile_row0 + r * n_sub, buf_ref, dest_ref[0, 0, ts + r], hbm_sem).start()
        return carry

    lax.fori_loop(0, ts // DMA_UNROLL, issue, 0)
    pltpu.make_async_copy(hn_ref, buf_ref.at[pl.ds(0, ts * n_sub)], sem).wait()
    pltpu.make_async_copy(hn_hbm_ref.at[pl.ds(0, ts * n_sub)], buf_ref.at[pl.ds(0, ts * n_sub)], hbm_sem).wait()


def _scatter_rows(hn, dest, pad_slots, cap, ts, n_sub):
    t = hn.shape[0] // n_sub
    assert t % ts == 0 and ts % DMA_UNROLL == 0
    return pl.pallas_call(
        functools.partial(_scatter_kernel, ts, n_sub),
        out_shape=jax.ShapeDtypeStruct((cap * n_sub, LANES), F32),
        grid=(t // ts,),
        in_specs=[
            pl.BlockSpec((1, 1, TOP_K * ts), lambda i: (i, 0, 0), memory_space=pltpu.SMEM),
            pl.BlockSpec(memory_space=pltpu.SMEM),
            pl.BlockSpec((ts * n_sub, LANES), lambda i: (i, 0)),
            pl.BlockSpec(memory_space=pl.ANY),
        ],
        out_specs=pl.BlockSpec(memory_space=pl.ANY),
        scratch_shapes=[pltpu.VMEM((n_sub, LANES), F32)] + [pltpu.SemaphoreType.DMA(())] * 3,
        compiler_params=pltpu.CompilerParams(
            dimension_semantics=("arbitrary",), vmem_limit_bytes=VMEM_LIMIT_BYTES),
        name="scatter_rows",
    )(_tile_indices(dest, ts), pad_slots, hn, hn)


def _combine_kernel(tc, final_norm, dest_ref, dest_next_ref, x_ref, route_ref, fw_ref, ybuf_ref, out_ref, rows, sems):
    n_sub = x_ref.shape[1] // LANES
    i = pl.program_id(0)
    slot = i % 2

    def gather(idx_ref, s):
        def issue(rb, carry):
            for u in range(DMA_UNROLL):
                r = rb * DMA_UNROLL + u
                for k in range(TOP_K):
                    _row_copy(n_sub, ybuf_ref, idx_ref[0, 0, k * tc + r], rows.at[s, k], r * n_sub,
                              sems.at[s]).start(priority=k % 2)
            return carry

        lax.fori_loop(0, tc // DMA_UNROLL, issue, 0)

    @pl.when(i == 0)
    def _():
        gather(dest_ref, 0)

    @pl.when(i + 1 < pl.num_programs(0))
    def _():
        gather(dest_next_ref, 1 - slot)

    for k in range(TOP_K):
        pltpu.make_async_copy(ybuf_ref.at[pl.ds(0, tc * n_sub)], rows.at[slot, k], sems.at[slot]).wait()

    route = route_ref[...]
    y = x_ref[...]
    for k in range(TOP_K):
        y = y + route[:, TOP_K + k:TOP_K + k + 1] * _tiles_to_rows(rows.at[slot, k], n_sub)
    out_ref[...] = _rmsnorm(y, fw_ref[...]) if final_norm else y


def _combine(x2, route, dest, ybuf, final_w, tc):
    t, d = x2.shape
    assert t % tc == 0 and tc % DMA_UNROLL == 0
    n_tiles = t // tc
    final_norm = final_w is not None
    fw = (final_w if final_norm else jnp.ones((d,), F32)).reshape(1, d).astype(F32)
    idx = _tile_indices(dest, tc)
    return pl.pallas_call(
        functools.partial(_combine_kernel, tc, final_norm),
        out_shape=jax.ShapeDtypeStruct((t, d), F32),
        grid=(n_tiles,),
        in_specs=[
            pl.BlockSpec((1, 1, TOP_K * tc), lambda i: (i, 0, 0), memory_space=pltpu.SMEM),
            pl.BlockSpec((1, 1, TOP_K * tc), lambda i: (jnp.minimum(i + 1, n_tiles - 1), 0, 0), memory_space=pltpu.SMEM),
            pl.BlockSpec((tc, d), lambda i: (i, 0)),
            pl.BlockSpec((tc, LANES), lambda i: (i, 0)),
            pl.BlockSpec((1, d), lambda i: (0, 0)),
            pl.BlockSpec(memory_space=pl.ANY),
        ],
        out_specs=pl.BlockSpec((tc, d), lambda i: (i, 0)),
        scratch_shapes=[pltpu.VMEM((2, TOP_K, tc * (d // LANES), LANES), F32), pltpu.SemaphoreType.DMA((2,))],
        compiler_params=pltpu.CompilerParams(
            dimension_semantics=("arbitrary",), vmem_limit_bytes=VMEM_LIMIT_BYTES),
        name="combine",
    )(idx, idx, x2, route, fw, ybuf)


def _moe_layer(x2, norm_w, w_router, w_gate, w_up, w_down, final_w, tiles):
    t, d = x2.shape
    bm = tiles["moe_bm"]
    tm = tiles["router_tm"]
    hn, route, route_t, counts = _router(x2, norm_w, w_router, tm)
    counts = counts[::SUBLANES, :N_EXPERTS].astype(jnp.int32)
    sizes = jnp.sum(counts, axis=0)
    padded = ((sizes + bm - 1) // bm) * bm
    ends = jnp.cumsum(padded)
    starts = ends - padded
    tile_base = starts[None, :] + jnp.cumsum(counts, axis=0) - counts
    e_idx = route_t[0:TOP_K].astype(jnp.int32).reshape(TOP_K, t // tm, tm)
    rank = route_t[2 * TOP_K:3 * TOP_K].astype(jnp.int32).reshape(TOP_K, t // tm, tm)
    base = jnp.zeros_like(rank)
    for e in range(N_EXPERTS):
        base = jnp.where(e_idx == e, tile_base[None, :, e:e + 1], base)
    dest = (base + rank).reshape(TOP_K, t)
    cap = TOP_K * t + N_EXPERTS * bm
    n_blocks = cap // bm
    first_slot = jnp.arange(n_blocks, dtype=jnp.int32) * bm
    blk_e = jnp.minimum(jnp.sum(first_slot[:, None] >= ends[None, :], axis=1), N_EXPERTS - 1).astype(jnp.int32)
    nact = (ends[-1:] // bm).astype(jnp.int32)

    n_sub = d // LANES
    dest_rows = dest * n_sub
    pad_slots = jnp.concatenate([starts + sizes, ends[-1:], padded - sizes, cap - ends[-1:]]).astype(jnp.int32)
    buf = _scatter_rows(hn, dest_rows, pad_slots, cap, tiles["scatter_ts"], n_sub)
    ybuf = _expert_ffn(buf, blk_e, nact, w_gate, w_up, w_down, bm, tiles["moe_tf"])
    return _combine(x2, route, dest_rows, ybuf, final_w, tiles["combine_tc"])


TILES = {
    "mixer_tl": 512,
    "ffn_tm": 512,
    "router_tm": 1024,
    "moe_bm": 512, "moe_tf": 1792,
    "scatter_ts": 1024,
    "combine_tc": 512,
}


def kernel(x, norm_mix, w_in, conv_a_w, conv_ssd_w, conv_ssd_b, dt_bias, a_log, d_skip, ssd_norm_w, conv_conf_w, conv_conf_b, conf_ln_g, conf_ln_b, w_out, norm_ffn, ffn_w_gate, ffn_w_up, ffn_w_down, moe_router, moe_w_gate, moe_w_up, moe_w_down, norm_final):
    bsz, seq, d = x.shape
    depth = norm_mix.shape[0]
    tiles = dict(TILES)
    tiles["mixer_tl"] = min(tiles["mixer_tl"], seq)
    expert_w = None
    for i in range(depth):
        x = _mixer_layer(x, norm_mix[i], w_in[i], conv_a_w[i], conv_ssd_w[i], conv_ssd_b[i], dt_bias[i], a_log[i],
                         d_skip[i], ssd_norm_w[i], conv_conf_w[i], conv_conf_b[i], conf_ln_g[i], conf_ln_b[i],
                         w_out[i], tiles["mixer_tl"])
        x2 = x.reshape(bsz * seq, d)
        final_w = norm_final if i == depth - 1 else None
        j = i // 2
        if i % 2 == 0:
            later = (moe_w_gate[j], moe_w_up[j], moe_w_down[j]) if i + 1 < depth else ()
            x2, expert_w = _dense_ffn(x2, norm_ffn[i], ffn_w_gate[j], ffn_w_up[j], ffn_w_down[j], final_w,
                                      tiles["ffn_tm"], later)
        else:
            if not expert_w:
                expert_w = [w.astype(BF16) for w in (moe_w_gate[j], moe_w_up[j], moe_w_down[j])]
            x2 = _moe_layer(x2, norm_ffn[i], moe_router[j], *expert_w, final_w, tiles)
            expert_w = None
        x = x2.reshape(bsz, seq, d)
    return x
```

```python
import functools

import jax
import jax.numpy as jnp
from jax import lax
from jax.experimental import pallas as pl
from jax.experimental.pallas import tpu as pltpu

F32 = jnp.float32
BF16 = jnp.bfloat16
EPS = 1e-5

LANES = 128
SUBLANES = 8
VMEM_LIMIT_BYTES = 56 * 1024 * 1024

A_CONV = 3
SSD_HEAD_DIM = 64
SSD_GROUPS = 2
SSD_STATE = 128
SSD_CONV = 4
CONF_KERNEL = 31
N_EXPERTS = 8
TOP_K = 2

SSD_CHUNK = 128
CONF_HALO = 32
SMALL_HALO = SUBLANES
CONV_ROWS = 32
ROW_BLOCK = 64
LANE_BLOCK = 256
DMA_UNROLL = 16


def _split3(v):
    hi = v.astype(BF16)
    r1 = v - hi.astype(F32)
    mid = r1.astype(BF16)
    lo = (r1 - mid.astype(F32)).astype(BF16)
    return hi, mid, lo


def _dot(a, b):
    return jnp.dot(a, b, preferred_element_type=F32)


def _dot_nt(a, b):
    return lax.dot_general(a, b, (((1,), (1,)), ((), ())), preferred_element_type=F32)


def _dot_sel_right(v, sel):
    hi, mid, lo = _split3(v)
    return _dot(hi, sel) + _dot(mid, sel) + _dot(lo, sel)


def _softplus(v):
    return jnp.maximum(v, 0.0) + jnp.log1p(jnp.exp(-jnp.abs(v)))


def _silu(v):
    return v * jax.nn.sigmoid(v)


def _tile_rows(w8, rows):
    return jnp.tile(w8, (rows // SUBLANES, 1))


def _rmsnorm(x, w):
    return x * lax.rsqrt(jnp.mean(x * x, axis=-1, keepdims=True) + EPS) * w


def _mixer_kernel(dims, x_ref, nw_ref, win_ref, wdtT_ref, caw_ref, csw_ref, csb_ref, dtbr_ref,
                  alr_ref, dsk_ref, snw_ref, ccw_ref, ccb_ref, lng_ref, lnb_ref, wout_ref,
                  out_ref,
                  hn_s, p_a, p_c, p_s, p_z, dtr, buf_a, buf_c, shc, xact, ysd, ycat, hstate):
    tl, w_a, w_ssd, w_conf, n_heads, n_l = dims
    q = SSD_CHUNK
    n_bc = SSD_GROUPS * SSD_STATE
    xbc = w_ssd + 2 * n_bc
    hpg = n_heads // SSD_GROUPS
    gw = hpg * SSD_HEAD_DIM
    hb = SMALL_HALO
    o_z = 3 * w_a
    o_xbc = o_z + w_ssd
    o_c = o_xbc + xbc
    lt = lax.rem(pl.program_id(0), n_l)

    @pl.when(lt == 0)
    def _():
        buf_a[0:hb, :] = jnp.zeros((hb, w_a), F32)
        p_s[0:hb, :] = jnp.zeros((hb, xbc), F32)
        buf_c[0:CONF_HALO, :] = jnp.zeros((CONF_HALO, w_conf), F32)
        hstate[...] = jnp.zeros(hstate.shape, F32)

    @pl.when(lt > 0)
    def _():
        buf_a[0:hb, :] = buf_a[tl:tl + hb, :]
        p_s[0:hb, :] = p_s[tl:tl + hb, :]
        buf_c[0:CONF_HALO, :] = buf_c[tl:tl + CONF_HALO, :]

    hn_s[...] = _rmsnorm(x_ref[...], nw_ref[...]).astype(BF16)

    p_s[hb:hb + tl, :] = _dot(hn_s[...], win_ref[:, o_xbc:o_xbc + xbc])
    for rb in range(tl // ROW_BLOCK):
        for lb in range(xbc // LANE_BLOCK):
            cols = slice(lb * LANE_BLOCK, (lb + 1) * LANE_BLOCK)
            xc = _tile_rows(csb_ref[:, cols], ROW_BLOCK)
            for j in range(SSD_CONV):
                s = hb + rb * ROW_BLOCK - (SSD_CONV - 1) + j
                xc = xc + _tile_rows(csw_ref[j, :, cols], ROW_BLOCK) * p_s[s:s + ROW_BLOCK, cols]
            xact[rb * ROW_BLOCK:(rb + 1) * ROW_BLOCK, cols] = _silu(xc)
    p_z[...] = _dot(hn_s[...], win_ref[:, o_z:o_z + w_ssd])
    dtr[...] = _dot_nt(wdtT_ref[...], hn_s[...])

    p_c[...] = _dot(hn_s[...], win_ref[:, o_c:o_c + 2 * w_conf])
    for rb in range(tl // ROW_BLOCK):
        r0 = rb * ROW_BLOCK
        buf_c[CONF_HALO + r0:CONF_HALO + r0 + ROW_BLOCK, :] = (
            p_c[r0:r0 + ROW_BLOCK, 0:w_conf] * jax.nn.sigmoid(p_c[r0:r0 + ROW_BLOCK, w_conf:2 * w_conf]))
    n_sh = shc.shape[1]
    for r in range(1, SUBLANES):
        shc[r - 1] = buf_c[r:r + n_sh, :]
    for rb in range(tl // CONV_ROWS):
        acc = _tile_rows(ccb_ref[...], CONV_ROWS)
        for j in range(CONF_KERNEL):
            a, r = divmod(CONF_HALO - (CONF_KERNEL - 1) + j, SUBLANES)
            base = a * SUBLANES + rb * CONV_ROWS
            src = buf_c[base:base + CONV_ROWS, :] if r == 0 else shc[r - 1, base:base + CONV_ROWS, :]
            acc = acc + _tile_rows(ccw_ref[j], CONV_ROWS) * src
        mu = jnp.mean(acc, axis=-1, keepdims=True)
        cen = acc - mu
        var = jnp.mean(cen * cen, axis=-1, keepdims=True)
        yc = cen * lax.rsqrt(var + EPS) * lng_ref[...] + lnb_ref[...]
        ycat[rb * CONV_ROWS:(rb + 1) * CONV_ROWS, w_a + w_ssd:w_a + w_ssd + w_conf] = _silu(yc).astype(BF16)

    p_a[...] = _dot(hn_s[...], win_ref[:, 0:3 * w_a])
    for rb in range(tl // ROW_BLOCK):
        r0 = rb * ROW_BLOCK
        b0 = hb + r0
        buf_a[b0:b0 + ROW_BLOCK, :] = p_a[r0:r0 + ROW_BLOCK, w_a:2 * w_a] * p_a[r0:r0 + ROW_BLOCK, 2 * w_a:3 * w_a]
        ya = jnp.zeros((ROW_BLOCK, w_a), F32)
        for j in range(A_CONV):
            s = b0 - (A_CONV - 1) + j
            ya = ya + _tile_rows(caw_ref[j], ROW_BLOCK) * buf_a[s:s + ROW_BLOCK, :]
        ycat[r0:r0 + ROW_BLOCK, 0:w_a] = (p_a[r0:r0 + ROW_BLOCK, 0:w_a] * ya).astype(BF16)

    a_row = -jnp.exp(alr_ref[...])
    ri = lax.broadcasted_iota(jnp.int32, (q, q), 0)
    ci = lax.broadcasted_iota(jnp.int32, (q, q), 1)
    causal = ri >= ci
    triu = jnp.where(ri <= ci, 1.0, 0.0).astype(BF16)
    hd = SSD_HEAD_DIM

    for c in range(tl // q):
        rows = slice(c * q, (c + 1) * q)
        dt_r = _softplus(dtr[:, rows] + dtbr_ref[...])
        acum_r = _dot_sel_right(dt_r * a_row, triu)
        to_end_r = dt_r * jnp.exp(acum_r[:, q - 1:q] - acum_r)
        acum_c = jnp.concatenate([acum_r, jnp.zeros((LANES - n_heads, q), F32)], axis=0).T
        for g in range(SSD_GROUPS):
            bg = xact[rows, w_ssd + g * SSD_STATE:w_ssd + (g + 1) * SSD_STATE]
            cg = xact[rows, w_ssd + n_bc + g * SSD_STATE:w_ssd + n_bc + (g + 1) * SSD_STATE].astype(BF16)
            cb = _dot_nt(cg, bg.astype(BF16))
            bgt = bg.T
            hg = hstate[g]
            yoff = _dot(cg, hg.astype(BF16))
            for r in range(hpg):
                h = g * hpg + r
                xh = xact[rows, h * hd:(h + 1) * hd].astype(BF16)
                acol = jnp.broadcast_to(acum_c[:, h:h + 1], (q, q))
                dec = jnp.where(causal, jnp.exp(acol - acum_r[h:h + 1, :]), 0.0)
                yd = _dot((cb * dec * dt_r[h:h + 1, :]).astype(BF16), xh)
                ysd[rows, h * hd:(h + 1) * hd] = yd + yoff[:, r * hd:(r + 1) * hd] * jnp.exp(acol[:, 0:hd])
                st = _dot((bgt * to_end_r[h:h + 1, :]).astype(BF16), xh)
                hstate[g, :, r * hd:(r + 1) * hd] = jnp.exp(acol[q - 1:q, 0:hd]) * hg[:, r * hd:(r + 1) * hd] + st
        xs = xact[rows, 0:w_ssd]
        yz = (ysd[rows, :] + dsk_ref[...] * xs) * _silu(p_z[rows, :])
        for g in range(SSD_GROUPS):
            yg = yz[:, g * gw:(g + 1) * gw]
            yn = yg * lax.rsqrt(jnp.mean(yg * yg, axis=-1, keepdims=True) + EPS) * snw_ref[:, g * gw:(g + 1) * gw]
            ycat[rows, w_a + g * gw:w_a + (g + 1) * gw] = yn.astype(BF16)

    k_c = w_a + w_ssd
    out_ref[...] = (x_ref[...] + _dot(ycat[:, 0:k_c], wout_ref[0:k_c, :])) + _dot(ycat[:, k_c:], wout_ref[k_c:, :])


def _mixer_layer(x, norm_w, w_in, conv_a_w, conv_ssd_w, conv_ssd_b, dt_bias, a_log, d_skip, ssd_norm_w,
                 conv_conf_w, conv_conf_b, conf_ln_g, conf_ln_b, w_out, tl):
    bsz, seq, d = x.shape
    w_a = conv_a_w.shape[1]
    w_ssd = ssd_norm_w.shape[0]
    w_conf = conv_conf_w.shape[1]
    n_heads = dt_bias.shape[0]
    n_bc = SSD_GROUPS * SSD_STATE
    xbc = w_ssd + 2 * n_bc
    assert n_heads == SUBLANES and n_heads * SSD_HEAD_DIM == w_ssd and seq % tl == 0 and tl % SSD_CHUNK == 0
    assert tl % ROW_BLOCK == 0 and xbc % LANE_BLOCK == 0 and conv_a_w.shape[0] == A_CONV and conv_ssd_w.shape[0] == SSD_CONV
    assert conv_conf_w.shape[0] == CONF_KERNEL
    o_dt = 3 * w_a + w_ssd + xbc
    w_dt = w_in[:, o_dt:o_dt + n_heads]
    w_main = jnp.concatenate([w_in[:, :o_dt], w_in[:, o_dt + n_heads:]], axis=1).astype(BF16)
    n_cols = w_main.shape[1]
    w_dt_t = w_dt.T.astype(BF16)
    row = lambda v: v.reshape(1, -1).astype(F32)
    col = lambda v: v.reshape(-1, 1).astype(F32)
    taps = lambda w: jnp.broadcast_to(w.astype(F32)[:, None, :], (w.shape[0], SUBLANES, w.shape[1]))
    d_skip_x = jnp.repeat(d_skip.astype(F32), SSD_HEAD_DIM).reshape(1, w_ssd)

    const = lambda shape: pl.BlockSpec(shape, lambda s: (0,) * len(shape))
    n_l = seq // tl
    n_tiles = bsz * n_l
    dims = (tl, w_a, w_ssd, w_conf, n_heads, n_l)
    x2 = x.reshape(bsz * seq, d)
    out = pl.pallas_call(
        functools.partial(_mixer_kernel, dims),
        out_shape=jax.ShapeDtypeStruct((bsz * seq, d), F32),
        grid=(n_tiles,),
        in_specs=[
            pl.BlockSpec((tl, d), lambda s: (s, 0)),
            const((1, d)), const((d, n_cols)), const((n_heads, d)),
            const((A_CONV, SUBLANES, w_a)), const((SSD_CONV, SUBLANES, xbc)), const((SUBLANES, xbc)),
            const((n_heads, 1)), const((n_heads, 1)),
            const((1, w_ssd)), const((1, w_ssd)),
            const((CONF_KERNEL, SUBLANES, w_conf)), const((SUBLANES, w_conf)), const((1, w_conf)), const((1, w_conf)),
            const((d, d)),
        ],
        out_specs=pl.BlockSpec((tl, d), lambda s: (s, 0)),
        scratch_shapes=[
            pltpu.VMEM((tl, d), BF16),
            pltpu.VMEM((tl, 3 * w_a), F32),
            pltpu.VMEM((tl, 2 * w_conf), F32),
            pltpu.VMEM((SMALL_HALO + tl, xbc), F32),
            pltpu.VMEM((tl, w_ssd), F32),
            pltpu.VMEM((n_heads, tl), F32),
            pltpu.VMEM((SMALL_HALO + tl, w_a), F32),
            pltpu.VMEM((CONF_HALO + tl, w_conf), F32),
            pltpu.VMEM((SUBLANES - 1, tl + CONF_HALO - SUBLANES, w_conf), F32),
            pltpu.VMEM((tl, xbc), F32),
            pltpu.VMEM((tl, w_ssd), F32),
            pltpu.VMEM((tl, d), BF16),
            pltpu.VMEM((SSD_GROUPS, SSD_STATE, w_ssd // SSD_GROUPS), F32),
        ],
        compiler_params=pltpu.CompilerParams(
            dimension_semantics=("arbitrary",), vmem_limit_bytes=VMEM_LIMIT_BYTES),
        name="mixer",
    )(x2, row(norm_w), w_main, w_dt_t, taps(conv_a_w), taps(conv_ssd_w), taps(conv_ssd_b[None])[0],
      col(dt_bias), col(a_log), d_skip_x, row(ssd_norm_w),
      taps(conv_conf_w), taps(conv_conf_b[None])[0], row(conf_ln_g), row(conf_ln_b), w_out.astype(BF16))
    return out.reshape(bsz, seq, d)


def _dense_ffn_kernel(final_norm, n_cast, x_ref, nw_ref, wg_ref, wu_ref, wd_ref, fw_ref, *rest):
    cast_in, out_ref, cast_out, hn = rest[:n_cast], rest[n_cast], rest[n_cast + 1:2 * n_cast + 1], rest[-1]
    hn[...] = _rmsnorm(x_ref[...], nw_ref[...]).astype(BF16)
    a = (_silu(_dot(hn[...], wg_ref[...])) * _dot(hn[...], wu_ref[...])).astype(BF16)
    y = x_ref[...] + _dot(a, wd_ref[...])
    out_ref[...] = _rmsnorm(y, fw_ref[...]) if final_norm else y
    for src, dst in zip(cast_in, cast_out):
        dst[...] = src[...].astype(BF16)


def _dense_ffn(x2, norm_w, w_gate, w_up, w_down, final_w, tm, later=()):
    t, d = x2.shape
    ff = w_gate.shape[1]
    assert t % tm == 0
    n_steps = t // tm
    final_norm = final_w is not None
    fw = (final_w if final_norm else jnp.ones((d,), F32)).reshape(1, d).astype(F32)
    resident = lambda shape: pl.BlockSpec(shape, lambda i: (0, 0), pipeline_mode=pl.Buffered(1))
    bf16_rows = 2 * SUBLANES
    slabs = []
    for w in later:
        rows = w.size // w.shape[-1]
        assert rows % (n_steps * bf16_rows) == 0
        slabs.append((rows // n_steps, w.shape[-1]))
    slab_specs = [pl.BlockSpec(s, lambda i: (i, 0)) for s in slabs]
    res = pl.pallas_call(
        functools.partial(_dense_ffn_kernel, final_norm, len(later)),
        out_shape=[jax.ShapeDtypeStruct((t, d), F32)]
        + [jax.ShapeDtypeStruct((n_steps * r, c), BF16) for r, c in slabs],
        grid=(n_steps,),
        in_specs=[
            pl.BlockSpec((tm, d), lambda i: (i, 0)),
            resident((1, d)), resident((d, ff)), resident((d, ff)), resident((ff, d)), resident((1, d)),
        ] + slab_specs,
        out_specs=[pl.BlockSpec((tm, d), lambda i: (i, 0))] + slab_specs,
        scratch_shapes=[pltpu.VMEM((tm, d), BF16)],
        compiler_params=pltpu.CompilerParams(
            dimension_semantics=("arbitrary",), vmem_limit_bytes=VMEM_LIMIT_BYTES),
        name="dense_ffn",
    )(x2, norm_w.reshape(1, d).astype(F32), w_gate.astype(BF16), w_up.astype(BF16), w_down.astype(BF16), fw,
      *[w.reshape(n_steps * r, c) for w, (r, c) in zip(later, slabs)])
    return res[0], [b.reshape(w.shape) for b, w in zip(res[1:], later)]


def _rows_to_tiles(tile_ref, val):
    n, d = val.shape
    n_sub = d // LANES
    for c in range(n_sub):
        tile_ref[pl.ds(c, n, stride=n_sub), :] = val[:, c * LANES:(c + 1) * LANES]


def _tiles_to_rows(tile_ref, n_sub):
    n = tile_ref.shape[0] // n_sub
    return jnp.concatenate([tile_ref[pl.ds(c, n, stride=n_sub), :] for c in range(n_sub)], axis=1)


def _expert_ffn_kernel(blk_e_ref, nact_ref, xb_ref, wg_ref, wu_ref, wd_ref, out_ref, hb, acc):
    i = pl.program_id(0)
    j = pl.program_id(1)

    n_j = pl.num_programs(1)
    active = i < nact_ref[0]

    def partial_out():
        h = hb[...]
        a = (_silu(_dot(h, wg_ref[...])) * _dot(h, wu_ref[...])).astype(BF16)
        return _dot(a, wd_ref[...])

    @pl.when(active & (j == 0))
    def _():
        hb[...] = _tiles_to_rows(xb_ref, hb.shape[1] // LANES).astype(BF16)
        acc[...] = partial_out()

    @pl.when(active & (j > 0) & (j < n_j - 1))
    def _():
        acc[...] += partial_out()

    @pl.when(active & (j == n_j - 1))
    def _():
        _rows_to_tiles(out_ref, acc[...] + partial_out())

    @pl.when((i >= nact_ref[0]) & (j == 0))
    def _():
        out_ref[...] = jnp.zeros(out_ref.shape, F32)


def _expert_ffn(buf, blk_e, nact, w_gate, w_up, w_down, bm, tf):
    d, ff = w_gate.shape[1], w_gate.shape[2]
    n_sub = d // LANES
    cap = buf.shape[0] // n_sub
    n_blocks = cap // bm
    n_j = ff // tf
    assert cap % bm == 0 and ff % tf == 0 and n_j >= 2 and w_gate.dtype == BF16
    row_blk = lambda i, j, be, na: (jnp.minimum(i, na[0] - 1), 0)
    snake = lambda i, j: jnp.where(i % 2 == 0, j, n_j - 1 - j)
    col_j = lambda i, j, na: jnp.where(i < na[0], snake(i, j), snake(na[0] - 1, n_j - 1))
    return pl.pallas_call(
        _expert_ffn_kernel,
        out_shape=jax.ShapeDtypeStruct((cap * n_sub, LANES), F32),
        grid_spec=pltpu.PrefetchScalarGridSpec(
            num_scalar_prefetch=2,
            grid=(n_blocks, n_j),
            in_specs=[
                pl.BlockSpec((bm * n_sub, LANES), row_blk),
                pl.BlockSpec((None, d, tf), lambda i, j, be, na: (be[i], 0, col_j(i, j, na))),
                pl.BlockSpec((None, d, tf), lambda i, j, be, na: (be[i], 0, col_j(i, j, na))),
                pl.BlockSpec((None, tf, d), lambda i, j, be, na: (be[i], col_j(i, j, na), 0)),
            ],
            out_specs=pl.BlockSpec((bm * n_sub, LANES), lambda i, j, be, na: (i, 0)),
            scratch_shapes=[pltpu.VMEM((bm, d), BF16), pltpu.VMEM((bm, d), F32)],
        ),
        compiler_params=pltpu.CompilerParams(
            dimension_semantics=("arbitrary", "arbitrary"), vmem_limit_bytes=VMEM_LIMIT_BYTES),
        name="expert_ffn",
    )(blk_e, nact, buf, w_gate, w_up, w_down)


RANK_BLOCK = 256


def _router_kernel(x_ref, nw_ref, wr_ref, hn_ref, route_ref, route_t_ref, count_ref):
    h = _rmsnorm(x_ref[...], nw_ref[...])
    _rows_to_tiles(hn_ref, h)
    h_hi, h_lo, _ = _split3(h)
    w_hi, w_lo, _ = _split3(wr_ref[...])
    logits = _dot(h_hi, w_hi) + _dot(h_hi, w_lo) + _dot(h_lo, w_hi)
    lane = lax.broadcasted_iota(jnp.int32, logits.shape, 1)
    neg = jnp.float32(-jnp.inf)
    logits = jnp.where(lane < N_EXPERTS, logits, neg)
    m1 = jnp.max(logits, axis=-1, keepdims=True)
    i1 = jnp.min(jnp.where(logits == m1, lane, LANES), axis=-1, keepdims=True)
    rest = jnp.where(lane == i1, neg, logits)
    m2 = jnp.max(rest, axis=-1, keepdims=True)
    i2 = jnp.min(jnp.where(rest == m2, lane, LANES), axis=-1, keepdims=True)
    e2 = jnp.exp(m2 - m1)
    g1 = 1.0 / (1.0 + e2)
    g2 = e2 * g1
    pick1, pick2 = lane == i1, lane == i2
    onehot = jnp.where(pick1 | pick2, 1.0, 0.0)
    rb = RANK_BLOCK
    ri = lax.broadcasted_iota(jnp.int32, (rb, rb), 0)
    ci = lax.broadcasted_iota(jnp.int32, (rb, rb), 1)
    before = jnp.where(ri > ci, 1.0, 0.0).astype(BF16)
    total = jnp.zeros((1, LANES), F32)
    earlier = []
    for b in range(h.shape[0] // rb):
        oh_b = onehot[b * rb:(b + 1) * rb, :]
        earlier.append(_dot(before, oh_b.astype(BF16)) + total)
        total = total + jnp.sum(oh_b, axis=0, keepdims=True)
    earlier = jnp.concatenate(earlier, axis=0)
    r1 = jnp.sum(jnp.where(pick1, earlier, 0.0), axis=-1, keepdims=True)
    r2 = jnp.sum(jnp.where(pick2, earlier, 0.0), axis=-1, keepdims=True)
    cols = (i1.astype(F32), i2.astype(F32), g1, g2, r1, r2)
    route = jnp.zeros(logits.shape, F32)
    for c, v in enumerate(cols):
        route = jnp.where(lane == c, v, route)
    route_ref[...] = route
    route_t_ref[...] = route.T[0:SUBLANES, :]
    count_ref[...] = jnp.broadcast_to(total, count_ref.shape)


def _router(x2, norm_w, w_router, tm):
    t, d = x2.shape
    n_e = w_router.shape[1]
    assert n_e == N_EXPERTS and t % tm == 0 and d % LANES == 0 and tm % RANK_BLOCK == 0
    n_sub = d // LANES
    wr = jnp.pad(w_router.astype(F32), ((0, 0), (0, LANES - n_e)))
    return pl.pallas_call(
        _router_kernel,
        out_shape=(jax.ShapeDtypeStruct((t * n_sub, LANES), F32), jax.ShapeDtypeStruct((t, LANES), F32),
                   jax.ShapeDtypeStruct((SUBLANES, t), F32),
                   jax.ShapeDtypeStruct((t // tm * SUBLANES, LANES), F32)),
        grid=(t // tm,),
        in_specs=[
            pl.BlockSpec((tm, d), lambda i: (i, 0)),
            pl.BlockSpec((1, d), lambda i: (0, 0)),
            pl.BlockSpec((d, LANES), lambda i: (0, 0)),
        ],
        out_specs=(pl.BlockSpec((tm * n_sub, LANES), lambda i: (i, 0)),
                   pl.BlockSpec((tm, LANES), lambda i: (i, 0)),
                   pl.BlockSpec((SUBLANES, tm), lambda i: (0, i)),
                   pl.BlockSpec((SUBLANES, LANES), lambda i: (i, 0))),
        compiler_params=pltpu.CompilerParams(
            dimension_semantics=("arbitrary",), vmem_limit_bytes=VMEM_LIMIT_BYTES),
        name="router",
    )(x2, norm_w.reshape(1, d).astype(F32), wr)


def _row_copy(n_sub, src_ref, src_row, dst_ref, dst_row, sem):
    return pltpu.make_async_copy(src_ref.at[pl.ds(pl.multiple_of(src_row, n_sub), n_sub)],
                                 dst_ref.at[pl.ds(pl.multiple_of(dst_row, n_sub), n_sub)], sem)


def _tile_indices(dest, tile):
    k, t = dest.shape
    return dest.reshape(k, t // tile, tile).transpose(1, 0, 2).reshape(t // tile, 1, k * tile)


def _scatter_kernel(ts, n_sub, dest_ref, pad_ref, hn_ref, buf_ref, zero_tile, sem, pad_sem):
    @pl.when(pl.program_id(0) == 0)
    def _():
        zero_tile[...] = jnp.zeros(zero_tile.shape, F32)
        n_ranges = pad_ref.shape[0] // 2
        for e in range(n_ranges):
            first, count = pad_ref[e], pad_ref[n_ranges + e]

            def fill(p, carry):
                _row_copy(n_sub, zero_tile, 0, buf_ref, (first + p) * n_sub, pad_sem).start()
                return carry

            def fill_done(p, carry):
                _row_copy(n_sub, zero_tile, 0, buf_ref, 0, pad_sem).wait()
                return carry

            lax.fori_loop(0, count, fill, 0)
            lax.fori_loop(0, count, fill_done, 0)

    def issue(rb, carry):
        for u in range(DMA_UNROLL):
            r = rb * DMA_UNROLL + u
            for k in range(TOP_K):
                _row_copy(n_sub, hn_ref, r * n_sub, buf_ref, dest_ref[0, 0, k * ts + r], sem).start(priority=k % 2)
        return carry

    lax.fori_loop(0, ts // DMA_UNROLL, issue, 0)
    for k in range(TOP_K):
        pltpu.make_async_copy(hn_ref, buf_ref.at[pl.ds(0, ts * n_sub)], sem).wait()


def _scatter_rows(hn, dest, pad_slots, cap, ts, n_sub):
    t = hn.shape[0] // n_sub
    assert t % ts == 0 and ts % DMA_UNROLL == 0
    return pl.pallas_call(
        functools.partial(_scatter_kernel, ts, n_sub),
        out_shape=jax.ShapeDtypeStruct((cap * n_sub, LANES), F32),
        grid=(t // ts,),
        in_specs=[
            pl.BlockSpec((1, 1, TOP_K * ts), lambda i: (i, 0, 0), memory_space=pltpu.SMEM),
            pl.BlockSpec(memory_space=pltpu.SMEM),
            pl.BlockSpec((ts * n_sub, LANES), lambda i: (i, 0)),
        ],
        out_specs=pl.BlockSpec(memory_space=pl.ANY),
        scratch_shapes=[pltpu.VMEM((n_sub, LANES), F32), pltpu.SemaphoreType.DMA(()), pltpu.SemaphoreType.DMA(())],
        compiler_params=pltpu.CompilerParams(
            dimension_semantics=("arbitrary",), vmem_limit_bytes=VMEM_LIMIT_BYTES),
        name="scatter_rows",
    )(_tile_indices(dest, ts), pad_slots, hn)


def _combine_kernel(tc, final_norm, dest_ref, dest_next_ref, x_ref, route_ref, fw_ref, ybuf_ref, out_ref, rows, sems):
    n_sub = x_ref.shape[1] // LANES
    i = pl.program_id(0)
    slot = i % 2

    def gather(idx_ref, s):
        def issue(rb, carry):
            for u in range(DMA_UNROLL):
                r = rb * DMA_UNROLL + u
                for k in range(TOP_K):
                    _row_copy(n_sub, ybuf_ref, idx_ref[0, 0, k * tc + r], rows.at[s, k], r * n_sub,
                              sems.at[s]).start(priority=k % 2)
            return carry

        lax.fori_loop(0, tc // DMA_UNROLL, issue, 0)

    @pl.when(i == 0)
    def _():
        gather(dest_ref, 0)

    @pl.when(i + 1 < pl.num_programs(0))
    def _():
        gather(dest_next_ref, 1 - slot)

    for k in range(TOP_K):
        pltpu.make_async_copy(ybuf_ref.at[pl.ds(0, tc * n_sub)], rows.at[slot, k], sems.at[slot]).wait()

    route = route_ref[...]
    y = x_ref[...]
    for k in range(TOP_K):
        y = y + route[:, TOP_K + k:TOP_K + k + 1] * _tiles_to_rows(rows.at[slot, k], n_sub)
    out_ref[...] = _rmsnorm(y, fw_ref[...]) if final_norm else y


def _combine(x2, route, dest, ybuf, final_w, tc):
    t, d = x2.shape
    assert t % tc == 0 and tc % DMA_UNROLL == 0
    n_tiles = t // tc
    final_norm = final_w is not None
    fw = (final_w if final_norm else jnp.ones((d,), F32)).reshape(1, d).astype(F32)
    idx = _tile_indices(dest, tc)
    return pl.pallas_call(
        functools.partial(_combine_kernel, tc, final_norm),
        out_shape=jax.ShapeDtypeStruct((t, d), F32),
        grid=(n_tiles,),
        in_specs=[
            pl.BlockSpec((1, 1, TOP_K * tc), lambda i: (i, 0, 0), memory_space=pltpu.SMEM),
            pl.BlockSpec((1, 1, TOP_K * tc), lambda i: (jnp.minimum(i + 1, n_tiles - 1), 0, 0), memory_space=pltpu.SMEM),
            pl.BlockSpec((tc, d), lambda i: (i, 0)),
            pl.BlockSpec((tc, LANES), lambda i: (i, 0)),
            pl.BlockSpec((1, d), lambda i: (0, 0)),
            pl.BlockSpec(memory_space=pl.ANY),
        ],
        out_specs=pl.BlockSpec((tc, d), lambda i: (i, 0)),
        scratch_shapes=[pltpu.VMEM((2, TOP_K, tc * (d // LANES), LANES), F32), pltpu.SemaphoreType.DMA((2,))],
        compiler_params=pltpu.CompilerParams(
            dimension_semantics=("arbitrary",), vmem_limit_bytes=VMEM_LIMIT_BYTES),
        name="combine",
    )(idx, idx, x2, route, fw, ybuf)


def _moe_layer(x2, norm_w, w_router, w_gate, w_up, w_down, final_w, tiles):
    t, d = x2.shape
    bm = tiles["moe_bm"]
    tm = tiles["router_tm"]
    hn, route, route_t, counts = _router(x2, norm_w, w_router, tm)
    counts = counts[::SUBLANES, :N_EXPERTS].astype(jnp.int32)
    sizes = jnp.sum(counts, axis=0)
    padded = ((sizes + bm - 1) // bm) * bm
    ends = jnp.cumsum(padded)
    starts = ends - padded
    tile_base = starts[None, :] + jnp.cumsum(counts, axis=0) - counts
    e_idx = route_t[0:TOP_K].astype(jnp.int32).reshape(TOP_K, t // tm, tm)
    rank = route_t[2 * TOP_K:3 * TOP_K].astype(jnp.int32).reshape(TOP_K, t // tm, tm)
    base = jnp.zeros_like(rank)
    for e in range(N_EXPERTS):
        base = jnp.where(e_idx == e, tile_base[None, :, e:e + 1], base)
    dest = (base + rank).reshape(TOP_K, t)
    cap = TOP_K * t + N_EXPERTS * bm
    n_blocks = cap // bm
    first_slot = jnp.arange(n_blocks, dtype=jnp.int32) * bm
    blk_e = jnp.minimum(jnp.sum(first_slot[:, None] >= ends[None, :], axis=1), N_EXPERTS - 1).astype(jnp.int32)
    nact = (ends[-1:] // bm).astype(jnp.int32)

    n_sub = d // LANES
    dest_rows = dest * n_sub
    pad_slots = jnp.concatenate([starts + sizes, ends[-1:], padded - sizes, cap - ends[-1:]]).astype(jnp.int32)
    buf = _scatter_rows(hn, dest_rows, pad_slots, cap, tiles["scatter_ts"], n_sub)
    ybuf = _expert_ffn(buf, blk_e, nact, w_gate, w_up, w_down, bm, tiles["moe_tf"])
    return _combine(x2, route, dest_rows, ybuf, final_w, tiles["combine_tc"])


TILES = {
    "mixer_tl": 512,
    "ffn_tm": 512,
    "router_tm": 1024,
    "moe_bm": 512, "moe_tf": 1792,
    "scatter_ts": 1024,
    "combine_tc": 512,
}


def kernel(x, norm_mix, w_in, conv_a_w, conv_ssd_w, conv_ssd_b, dt_bias, a_log, d_skip, ssd_norm_w, conv_conf_w, conv_conf_b, conf_ln_g, conf_ln_b, w_out, norm_ffn, ffn_w_gate, ffn_w_up, ffn_w_down, moe_router, moe_w_gate, moe_w_up, moe_w_down, norm_final):
    bsz, seq, d = x.shape
    depth = norm_mix.shape[0]
    tiles = dict(TILES)
    tiles["mixer_tl"] = min(tiles["mixer_tl"], seq)
    expert_w = None
    for i in range(depth):
        x = _mixer_layer(x, norm_mix[i], w_in[i], conv_a_w[i], conv_ssd_w[i], conv_ssd_b[i], dt_bias[i], a_log[i],
                         d_skip[i], ssd_norm_w[i], conv_conf_w[i], conv_conf_b[i], conf_ln_g[i], conf_ln_b[i],
                         w_out[i], tiles["mixer_tl"])
        x2 = x.reshape(bsz * seq, d)
        final_w = norm_final if i == depth - 1 else None
        j = i // 2
        if i % 2 == 0:
            later = (moe_w_gate[j], moe_w_up[j], moe_w_down[j]) if i + 1 < depth else ()
            x2, expert_w = _dense_ffn(x2, norm_ffn[i], ffn_w_gate[j], ffn_w_up[j], ffn_w_down[j], final_w,
                                      tiles["ffn_tm"], later)
        else:
            if not expert_w:
                expert_w = [w.astype(BF16) for w in (moe_w_gate[j], moe_w_up[j], moe_w_down[j])]
            x2 = _moe_layer(x2, norm_ffn[i], moe_router[j], *expert_w, final_w, tiles)
            expert_w = None
        x = x2.reshape(bsz, seq, d)
    return x
```

```python
import functools

import jax
import jax.numpy as jnp
from jax import lax
from jax.experimental import pallas as pl
from jax.experimental.pallas import tpu as pltpu

F32 = jnp.float32
BF16 = jnp.bfloat16
EPS = 1e-5

LANES = 128
SUBLANES = 8
VMEM_LIMIT_BYTES = 56 * 1024 * 1024

A_CONV = 3
SSD_HEAD_DIM = 64
SSD_GROUPS = 2
SSD_STATE = 128
SSD_CONV = 4
CONF_KERNEL = 31
N_EXPERTS = 8
TOP_K = 2

SSD_CHUNK = 128
CONF_HALO = 32
SMALL_HALO = SUBLANES
CONV_ROWS = 32
ROW_BLOCK = 64
LANE_BLOCK = 256
DMA_UNROLL = 16


def _split3(v):
    hi = v.astype(BF16)
    r1 = v - hi.astype(F32)
    mid = r1.astype(BF16)
    lo = (r1 - mid.astype(F32)).astype(BF16)
    return hi, mid, lo


def _dot(a, b):
    return jnp.dot(a, b, preferred_element_type=F32)


def _dot_nt(a, b):
    return lax.dot_general(a, b, (((1,), (1,)), ((), ())), preferred_element_type=F32)


def _dot_sel_right(v, sel):
    hi, mid, lo = _split3(v)
    return _dot(hi, sel) + _dot(mid, sel) + _dot(lo, sel)


def _softplus(v):
    return jnp.maximum(v, 0.0) + jnp.log1p(jnp.exp(-jnp.abs(v)))


def _silu(v):
    return v * jax.nn.sigmoid(v)


def _tile_rows(w8, rows):
    return jnp.tile(w8, (rows // SUBLANES, 1))


def _rmsnorm(x, w):
    return x * lax.rsqrt(jnp.mean(x * x, axis=-1, keepdims=True) + EPS) * w


def _mixer_kernel(dims, x_ref, nw_ref, win_ref, wdtT_ref, caw_ref, csw_ref, csb_ref, dtbr_ref,
                  alr_ref, dsk_ref, snw_ref, ccw_ref, ccb_ref, lng_ref, lnb_ref, wout_ref,
                  out_ref,
                  hn_s, p_a, p_c, p_s, p_z, dtr, buf_a, buf_c, shc, xact, ysd, ycat, hstate):
    tl, w_a, w_ssd, w_conf, n_heads, n_l = dims
    q = SSD_CHUNK
    n_bc = SSD_GROUPS * SSD_STATE
    xbc = w_ssd + 2 * n_bc
    hpg = n_heads // SSD_GROUPS
    gw = hpg * SSD_HEAD_DIM
    hb = SMALL_HALO
    o_z = 3 * w_a
    o_xbc = o_z + w_ssd
    o_c = o_xbc + xbc
    lt = lax.rem(pl.program_id(0), n_l)

    @pl.when(lt == 0)
    def _():
        buf_a[0:hb, :] = jnp.zeros((hb, w_a), F32)
        p_s[0:hb, :] = jnp.zeros((hb, xbc), F32)
        buf_c[0:CONF_HALO, :] = jnp.zeros((CONF_HALO, w_conf), F32)
        hstate[...] = jnp.zeros(hstate.shape, F32)

    @pl.when(lt > 0)
    def _():
        buf_a[0:hb, :] = buf_a[tl:tl + hb, :]
        p_s[0:hb, :] = p_s[tl:tl + hb, :]
        buf_c[0:CONF_HALO, :] = buf_c[tl:tl + CONF_HALO, :]

    hn_s[...] = _rmsnorm(x_ref[...], nw_ref[...]).astype(BF16)

    p_s[hb:hb + tl, :] = _dot(hn_s[...], win_ref[:, o_xbc:o_xbc + xbc])
    for rb in range(tl // ROW_BLOCK):
        for lb in range(xbc // LANE_BLOCK):
            cols = slice(lb * LANE_BLOCK, (lb + 1) * LANE_BLOCK)
            xc = _tile_rows(csb_ref[:, cols], ROW_BLOCK)
            for j in range(SSD_CONV):
                s = hb + rb * ROW_BLOCK - (SSD_CONV - 1) + j
                xc = xc + _tile_rows(csw_ref[j, :, cols], ROW_BLOCK) * p_s[s:s + ROW_BLOCK, cols]
            xact[rb * ROW_BLOCK:(rb + 1) * ROW_BLOCK, cols] = _silu(xc)
    p_z[...] = _dot(hn_s[...], win_ref[:, o_z:o_z + w_ssd])
    dtr[...] = _dot_nt(wdtT_ref[...], hn_s[...])

    p_c[...] = _dot(hn_s[...], win_ref[:, o_c:o_c + 2 * w_conf])
    for rb in range(tl // ROW_BLOCK):
        r0 = rb * ROW_BLOCK
        buf_c[CONF_HALO + r0:CONF_HALO + r0 + ROW_BLOCK, :] = (
            p_c[r0:r0 + ROW_BLOCK, 0:w_conf] * jax.nn.sigmoid(p_c[r0:r0 + ROW_BLOCK, w_conf:2 * w_conf]))
    n_sh = shc.shape[1]
    for r in range(1, SUBLANES):
        shc[r - 1] = buf_c[r:r + n_sh, :]
    for rb in range(tl // CONV_ROWS):
        acc = _tile_rows(ccb_ref[...], CONV_ROWS)
        for j in range(CONF_KERNEL):
            a, r = divmod(CONF_HALO - (CONF_KERNEL - 1) + j, SUBLANES)
            base = a * SUBLANES + rb * CONV_ROWS
            src = buf_c[base:base + CONV_ROWS, :] if r == 0 else shc[r - 1, base:base + CONV_ROWS, :]
            acc = acc + _tile_rows(ccw_ref[j], CONV_ROWS) * src
        mu = jnp.mean(acc, axis=-1, keepdims=True)
        cen = acc - mu
        var = jnp.mean(cen * cen, axis=-1, keepdims=True)
        yc = cen * lax.rsqrt(var + EPS) * lng_ref[...] + lnb_ref[...]
        ycat[rb * CONV_ROWS:(rb + 1) * CONV_ROWS, w_a + w_ssd:w_a + w_ssd + w_conf] = _silu(yc).astype(BF16)

    p_a[...] = _dot(hn_s[...], win_ref[:, 0:3 * w_a])
    for rb in range(tl // ROW_BLOCK):
        r0 = rb * ROW_BLOCK
        b0 = hb + r0
        buf_a[b0:b0 + ROW_BLOCK, :] = p_a[r0:r0 + ROW_BLOCK, w_a:2 * w_a] * p_a[r0:r0 + ROW_BLOCK, 2 * w_a:3 * w_a]
        ya = jnp.zeros((ROW_BLOCK, w_a), F32)
        for j in range(A_CONV):
            s = b0 - (A_CONV - 1) + j
            ya = ya + _tile_rows(caw_ref[j], ROW_BLOCK) * buf_a[s:s + ROW_BLOCK, :]
        ycat[r0:r0 + ROW_BLOCK, 0:w_a] = (p_a[r0:r0 + ROW_BLOCK, 0:w_a] * ya).astype(BF16)

    a_row = -jnp.exp(alr_ref[...])
    ri = lax.broadcasted_iota(jnp.int32, (q, q), 0)
    ci = lax.broadcasted_iota(jnp.int32, (q, q), 1)
    causal = ri >= ci
    triu = jnp.where(ri <= ci, 1.0, 0.0).astype(BF16)
    hd = SSD_HEAD_DIM

    for c in range(tl // q):
        rows = slice(c * q, (c + 1) * q)
        dt_r = _softplus(dtr[:, rows] + dtbr_ref[...])
        acum_r = _dot_sel_right(dt_r * a_row, triu)
        to_end_r = dt_r * jnp.exp(acum_r[:, q - 1:q] - acum_r)
        acum_c = jnp.concatenate([acum_r, jnp.zeros((LANES - n_heads, q), F32)], axis=0).T
        for g in range(SSD_GROUPS):
            bg = xact[rows, w_ssd + g * SSD_STATE:w_ssd + (g + 1) * SSD_STATE]
            cg = xact[rows, w_ssd + n_bc + g * SSD_STATE:w_ssd + n_bc + (g + 1) * SSD_STATE].astype(BF16)
            cb = _dot_nt(cg, bg.astype(BF16))
            bgt = bg.T
            hg = hstate[g]
            yoff = _dot(cg, hg.astype(BF16))
            for r in range(hpg):
                h = g * hpg + r
                xh = xact[rows, h * hd:(h + 1) * hd].astype(BF16)
                acol = jnp.broadcast_to(acum_c[:, h:h + 1], (q, q))
                dec = jnp.where(causal, jnp.exp(acol - acum_r[h:h + 1, :]), 0.0)
                yd = _dot((cb * dec * dt_r[h:h + 1, :]).astype(BF16), xh)
                ysd[rows, h * hd:(h + 1) * hd] = yd + yoff[:, r * hd:(r + 1) * hd] * jnp.exp(acol[:, 0:hd])
                st = _dot((bgt * to_end_r[h:h + 1, :]).astype(BF16), xh)
                hstate[g, :, r * hd:(r + 1) * hd] = jnp.exp(acol[q - 1:q, 0:hd]) * hg[:, r * hd:(r + 1) * hd] + st
        xs = xact[rows, 0:w_ssd]
        yz = (ysd[rows, :] + dsk_ref[...] * xs) * _silu(p_z[rows, :])
        for g in range(SSD_GROUPS):
            yg = yz[:, g * gw:(g + 1) * gw]
            yn = yg * lax.rsqrt(jnp.mean(yg * yg, axis=-1, keepdims=True) + EPS) * snw_ref[:, g * gw:(g + 1) * gw]
            ycat[rows, w_a + g * gw:w_a + (g + 1) * gw] = yn.astype(BF16)

    k_c = w_a + w_ssd
    out_ref[...] = (x_ref[...] + _dot(ycat[:, 0:k_c], wout_ref[0:k_c, :])) + _dot(ycat[:, k_c:], wout_ref[k_c:, :])


def _mixer_layer(x, norm_w, w_in, conv_a_w, conv_ssd_w, conv_ssd_b, dt_bias, a_log, d_skip, ssd_norm_w,
                 conv_conf_w, conv_conf_b, conf_ln_g, conf_ln_b, w_out, tl):
    bsz, seq, d = x.shape
    w_a = conv_a_w.shape[1]
    w_ssd = ssd_norm_w.shape[0]
    w_conf = conv_conf_w.shape[1]
    n_heads = dt_bias.shape[0]
    n_bc = SSD_GROUPS * SSD_STATE
    xbc = w_ssd + 2 * n_bc
    assert n_heads == SUBLANES and n_heads * SSD_HEAD_DIM == w_ssd and seq % tl == 0 and tl % SSD_CHUNK == 0
    assert tl % ROW_BLOCK == 0 and xbc % LANE_BLOCK == 0 and conv_a_w.shape[0] == A_CONV and conv_ssd_w.shape[0] == SSD_CONV
    assert conv_conf_w.shape[0] == CONF_KERNEL
    o_dt = 3 * w_a + w_ssd + xbc
    w_dt = w_in[:, o_dt:o_dt + n_heads]
    w_main = jnp.concatenate([w_in[:, :o_dt], w_in[:, o_dt + n_heads:]], axis=1).astype(BF16)
    n_cols = w_main.shape[1]
    w_dt_t = w_dt.T.astype(BF16)
    row = lambda v: v.reshape(1, -1).astype(F32)
    col = lambda v: v.reshape(-1, 1).astype(F32)
    taps = lambda w: jnp.broadcast_to(w.astype(F32)[:, None, :], (w.shape[0], SUBLANES, w.shape[1]))
    d_skip_x = jnp.repeat(d_skip.astype(F32), SSD_HEAD_DIM).reshape(1, w_ssd)

    const = lambda shape: pl.BlockSpec(shape, lambda s: (0,) * len(shape), pipeline_mode=pl.Buffered(1))
    n_l = seq // tl
    n_tiles = bsz * n_l
    dims = (tl, w_a, w_ssd, w_conf, n_heads, n_l)
    x2 = x.reshape(bsz * seq, d)
    out = pl.pallas_call(
        functools.partial(_mixer_kernel, dims),
        out_shape=jax.ShapeDtypeStruct((bsz * seq, d), F32),
        grid=(n_tiles,),
        in_specs=[
            pl.BlockSpec((tl, d), lambda s: (s, 0)),
            const((1, d)), const((d, n_cols)), const((n_heads, d)),
            const((A_CONV, SUBLANES, w_a)), const((SSD_CONV, SUBLANES, xbc)), const((SUBLANES, xbc)),
            const((n_heads, 1)), const((n_heads, 1)),
            const((1, w_ssd)), const((1, w_ssd)),
            const((CONF_KERNEL, SUBLANES, w_conf)), const((SUBLANES, w_conf)), const((1, w_conf)), const((1, w_conf)),
            const((d, d)),
        ],
        out_specs=pl.BlockSpec((tl, d), lambda s: (s, 0)),
        scratch_shapes=[
            pltpu.VMEM((tl, d), BF16),
            pltpu.VMEM((tl, 3 * w_a), F32),
            pltpu.VMEM((tl, 2 * w_conf), F32),
            pltpu.VMEM((SMALL_HALO + tl, xbc), F32),
            pltpu.VMEM((tl, w_ssd), F32),
            pltpu.VMEM((n_heads, tl), F32),
            pltpu.VMEM((SMALL_HALO + tl, w_a), F32),
            pltpu.VMEM((CONF_HALO + tl, w_conf), F32),
            pltpu.VMEM((SUBLANES - 1, tl + CONF_HALO - SUBLANES, w_conf), F32),
            pltpu.VMEM((tl, xbc), F32),
            pltpu.VMEM((tl, w_ssd), F32),
            pltpu.VMEM((tl, d), BF16),
            pltpu.VMEM((SSD_GROUPS, SSD_STATE, w_ssd // SSD_GROUPS), F32),
        ],
        compiler_params=pltpu.CompilerParams(
            dimension_semantics=("arbitrary",), vmem_limit_bytes=VMEM_LIMIT_BYTES),
        name="mixer",
    )(x2, row(norm_w), w_main, w_dt_t, taps(conv_a_w), taps(conv_ssd_w), taps(conv_ssd_b[None])[0],
      col(dt_bias), col(a_log), d_skip_x, row(ssd_norm_w),
      taps(conv_conf_w), taps(conv_conf_b[None])[0], row(conf_ln_g), row(conf_ln_b), w_out.astype(BF16))
    return out.reshape(bsz, seq, d)


def _dense_ffn_kernel(final_norm, n_cast, x_ref, nw_ref, wg_ref, wu_ref, wd_ref, fw_ref, *rest):
    cast_in, out_ref, cast_out, hn = rest[:n_cast], rest[n_cast], rest[n_cast + 1:2 * n_cast + 1], rest[-1]
    hn[...] = _rmsnorm(x_ref[...], nw_ref[...]).astype(BF16)
    a = (_silu(_dot(hn[...], wg_ref[...])) * _dot(hn[...], wu_ref[...])).astype(BF16)
    y = x_ref[...] + _dot(a, wd_ref[...])
    out_ref[...] = _rmsnorm(y, fw_ref[...]) if final_norm else y
    for src, dst in zip(cast_in, cast_out):
        dst[...] = src[...].astype(BF16)


def _dense_ffn(x2, norm_w, w_gate, w_up, w_down, final_w, tm, later=()):
    t, d = x2.shape
    ff = w_gate.shape[1]
    assert t % tm == 0
    n_steps = t // tm
    final_norm = final_w is not None
    fw = (final_w if final_norm else jnp.ones((d,), F32)).reshape(1, d).astype(F32)
    resident = lambda shape: pl.BlockSpec(shape, lambda i: (0, 0), pipeline_mode=pl.Buffered(1))
    bf16_rows = 2 * SUBLANES
    slabs = []
    for w in later:
        rows = w.size // w.shape[-1]
        assert rows % (n_steps * bf16_rows) == 0
        slabs.append((rows // n_steps, w.shape[-1]))
    slab_specs = [pl.BlockSpec(s, lambda i: (i, 0)) for s in slabs]
    res = pl.pallas_call(
        functools.partial(_dense_ffn_kernel, final_norm, len(later)),
        out_shape=[jax.ShapeDtypeStruct((t, d), F32)]
        + [jax.ShapeDtypeStruct((n_steps * r, c), BF16) for r, c in slabs],
        grid=(n_steps,),
        in_specs=[
            pl.BlockSpec((tm, d), lambda i: (i, 0)),
            resident((1, d)), resident((d, ff)), resident((d, ff)), resident((ff, d)), resident((1, d)),
        ] + slab_specs,
        out_specs=[pl.BlockSpec((tm, d), lambda i: (i, 0))] + slab_specs,
        scratch_shapes=[pltpu.VMEM((tm, d), BF16)],
        compiler_params=pltpu.CompilerParams(
            dimension_semantics=("arbitrary",), vmem_limit_bytes=VMEM_LIMIT_BYTES),
        name="dense_ffn",
    )(x2, norm_w.reshape(1, d).astype(F32), w_gate.astype(BF16), w_up.astype(BF16), w_down.astype(BF16), fw,
      *[w.reshape(n_steps * r, c) for w, (r, c) in zip(later, slabs)])
    return res[0], [b.reshape(w.shape) for b, w in zip(res[1:], later)]


def _rows_to_tiles(tile_ref, val):
    n, d = val.shape
    n_sub = d // LANES
    for c in range(n_sub):
        tile_ref[pl.ds(c, n, stride=n_sub), :] = val[:, c * LANES:(c + 1) * LANES]


def _tiles_to_rows(tile_ref, n_sub):
    n = tile_ref.shape[0] // n_sub
    return jnp.concatenate([tile_ref[pl.ds(c, n, stride=n_sub), :] for c in range(n_sub)], axis=1)


def _expert_ffn_kernel(blk_e_ref, nact_ref, xb_ref, wg_ref, wu_ref, wd_ref, out_ref, hb, acc):
    i = pl.program_id(0)
    j = pl.program_id(1)

    n_j = pl.num_programs(1)
    active = i < nact_ref[0]

    def partial_out():
        h = hb[...]
        a = (_silu(_dot(h, wg_ref[...])) * _dot(h, wu_ref[...])).astype(BF16)
        return _dot(a, wd_ref[...])

    @pl.when(active & (j == 0))
    def _():
        hb[...] = _tiles_to_rows(xb_ref, hb.shape[1] // LANES).astype(BF16)
        acc[...] = partial_out()

    @pl.when(active & (j > 0) & (j < n_j - 1))
    def _():
        acc[...] += partial_out()

    @pl.when(active & (j == n_j - 1))
    def _():
        _rows_to_tiles(out_ref, acc[...] + partial_out())

    @pl.when((i >= nact_ref[0]) & (j == 0))
    def _():
        out_ref[...] = jnp.zeros(out_ref.shape, F32)


def _expert_ffn(buf, blk_e, nact, w_gate, w_up, w_down, bm, tf):
    d, ff = w_gate.shape[1], w_gate.shape[2]
    n_sub = d // LANES
    cap = buf.shape[0] // n_sub
    n_blocks = cap // bm
    n_j = ff // tf
    assert cap % bm == 0 and ff % tf == 0 and n_j >= 2 and w_gate.dtype == BF16
    row_blk = lambda i, j, be, na: (jnp.minimum(i, na[0] - 1), 0)
    snake = lambda i, j: jnp.where(i % 2 == 0, j, n_j - 1 - j)
    col_j = lambda i, j, na: jnp.where(i < na[0], snake(i, j), snake(na[0] - 1, n_j - 1))
    return pl.pallas_call(
        _expert_ffn_kernel,
        out_shape=jax.ShapeDtypeStruct((cap * n_sub, LANES), F32),
        grid_spec=pltpu.PrefetchScalarGridSpec(
            num_scalar_prefetch=2,
            grid=(n_blocks, n_j),
            in_specs=[
                pl.BlockSpec((bm * n_sub, LANES), row_blk),
                pl.BlockSpec((None, d, tf), lambda i, j, be, na: (be[i], 0, col_j(i, j, na))),
                pl.BlockSpec((None, d, tf), lambda i, j, be, na: (be[i], 0, col_j(i, j, na))),
                pl.BlockSpec((None, tf, d), lambda i, j, be, na: (be[i], col_j(i, j, na), 0)),
            ],
            out_specs=pl.BlockSpec((bm * n_sub, LANES), lambda i, j, be, na: (i, 0)),
            scratch_shapes=[pltpu.VMEM((bm, d), BF16), pltpu.VMEM((bm, d), F32)],
        ),
        compiler_params=pltpu.CompilerParams(
            dimension_semantics=("arbitrary", "arbitrary"), vmem_limit_bytes=VMEM_LIMIT_BYTES),
        name="expert_ffn",
    )(blk_e, nact, buf, w_gate, w_up, w_down)


RANK_BLOCK = 256


def _router_kernel(x_ref, nw_ref, wr_ref, hn_ref, route_ref, route_t_ref, count_ref):
    h = _rmsnorm(x_ref[...], nw_ref[...])
    _rows_to_tiles(hn_ref, h)
    h_hi, h_lo, _ = _split3(h)
    w_hi, w_lo, _ = _split3(wr_ref[...])
    logits = _dot(h_hi, w_hi) + _dot(h_hi, w_lo) + _dot(h_lo, w_hi)
    lane = lax.broadcasted_iota(jnp.int32, logits.shape, 1)
    neg = jnp.float32(-jnp.inf)
    logits = jnp.where(lane < N_EXPERTS, logits, neg)
    m1 = jnp.max(logits, axis=-1, keepdims=True)
    i1 = jnp.min(jnp.where(logits == m1, lane, LANES), axis=-1, keepdims=True)
    rest = jnp.where(lane == i1, neg, logits)
    m2 = jnp.max(rest, axis=-1, keepdims=True)
    i2 = jnp.min(jnp.where(rest == m2, lane, LANES), axis=-1, keepdims=True)
    e2 = jnp.exp(m2 - m1)
    g1 = 1.0 / (1.0 + e2)
    g2 = e2 * g1
    pick1, pick2 = lane == i1, lane == i2
    onehot = jnp.where(pick1 | pick2, 1.0, 0.0)
    rb = RANK_BLOCK
    ri = lax.broadcasted_iota(jnp.int32, (rb, rb), 0)
    ci = lax.broadcasted_iota(jnp.int32, (rb, rb), 1)
    before = jnp.where(ri > ci, 1.0, 0.0).astype(BF16)
    total = jnp.zeros((1, LANES), F32)
    earlier = []
    for b in range(h.shape[0] // rb):
        oh_b = onehot[b * rb:(b + 1) * rb, :]
        earlier.append(_dot(before, oh_b.astype(BF16)) + total)
        total = total + jnp.sum(oh_b, axis=0, keepdims=True)
    earlier = jnp.concatenate(earlier, axis=0)
    r1 = jnp.sum(jnp.where(pick1, earlier, 0.0), axis=-1, keepdims=True)
    r2 = jnp.sum(jnp.where(pick2, earlier, 0.0), axis=-1, keepdims=True)
    cols = (i1.astype(F32), i2.astype(F32), g1, g2, r1, r2)
    route = jnp.zeros(logits.shape, F32)
    for c, v in enumerate(cols):
        route = jnp.where(lane == c, v, route)
    route_ref[...] = route
    route_t_ref[...] = route.T[0:SUBLANES, :]
    count_ref[...] = jnp.broadcast_to(total, count_ref.shape)


def _router(x2, norm_w, w_router, tm):
    t, d = x2.shape
    n_e = w_router.shape[1]
    assert n_e == N_EXPERTS and t % tm == 0 and d % LANES == 0 and tm % RANK_BLOCK == 0
    n_sub = d // LANES
    wr = jnp.pad(w_router.astype(F32), ((0, 0), (0, LANES - n_e)))
    return pl.pallas_call(
        _router_kernel,
        out_shape=(jax.ShapeDtypeStruct((t * n_sub, LANES), F32), jax.ShapeDtypeStruct((t, LANES), F32),
                   jax.ShapeDtypeStruct((SUBLANES, t), F32),
                   jax.ShapeDtypeStruct((t // tm * SUBLANES, LANES), F32)),
        grid=(t // tm,),
        in_specs=[
            pl.BlockSpec((tm, d), lambda i: (i, 0)),
            pl.BlockSpec((1, d), lambda i: (0, 0)),
            pl.BlockSpec((d, LANES), lambda i: (0, 0)),
        ],
        out_specs=(pl.BlockSpec((tm * n_sub, LANES), lambda i: (i, 0)),
                   pl.BlockSpec((tm, LANES), lambda i: (i, 0)),
                   pl.BlockSpec((SUBLANES, tm), lambda i: (0, i)),
                   pl.BlockSpec((SUBLANES, LANES), lambda i: (i, 0))),
        compiler_params=pltpu.CompilerParams(
            dimension_semantics=("arbitrary",), vmem_limit_bytes=VMEM_LIMIT_BYTES),
        name="router",
    )(x2, norm_w.reshape(1, d).astype(F32), wr)


def _row_copy(n_sub, src_ref, src_row, dst_ref, dst_row, sem):
    return pltpu.make_async_copy(src_ref.at[pl.ds(pl.multiple_of(src_row, n_sub), n_sub)],
                                 dst_ref.at[pl.ds(pl.multiple_of(dst_row, n_sub), n_sub)], sem)


def _tile_indices(dest, tile):
    k, t = dest.shape
    return dest.reshape(k, t // tile, tile).transpose(1, 0, 2).reshape(t // tile, 1, k * tile)


def _scatter_kernel(ts, n_sub, dest_ref, pad_ref, hn_ref, buf_ref, zero_tile, sem, pad_sem):
    @pl.when(pl.program_id(0) == 0)
    def _():
        zero_tile[...] = jnp.zeros(zero_tile.shape, F32)
        n_ranges = pad_ref.shape[0] // 2
        for e in range(n_ranges):
            first, count = pad_ref[e], pad_ref[n_ranges + e]

            def fill(p, carry):
                _row_copy(n_sub, zero_tile, 0, buf_ref, (first + p) * n_sub, pad_sem).start()
                return carry

            def fill_done(p, carry):
                _row_copy(n_sub, zero_tile, 0, buf_ref, 0, pad_sem).wait()
                return carry

            lax.fori_loop(0, count, fill, 0)
            lax.fori_loop(0, count, fill_done, 0)

    def issue(rb, carry):
        for u in range(DMA_UNROLL):
            r = rb * DMA_UNROLL + u
            for k in range(TOP_K):
                _row_copy(n_sub, hn_ref, r * n_sub, buf_ref, dest_ref[0, 0, k * ts + r], sem).start(priority=k % 2)
        return carry

    lax.fori_loop(0, ts // DMA_UNROLL, issue, 0)
    for k in range(TOP_K):
        pltpu.make_async_copy(hn_ref, buf_ref.at[pl.ds(0, ts * n_sub)], sem).wait()


def _scatter_rows(hn, dest, pad_slots, cap, ts, n_sub):
    t = hn.shape[0] // n_sub
    assert t % ts == 0 and ts % DMA_UNROLL == 0
    return pl.pallas_call(
        functools.partial(_scatter_kernel, ts, n_sub),
        out_shape=jax.ShapeDtypeStruct((cap * n_sub, LANES), F32),
        grid=(t // ts,),
        in_specs=[
            pl.BlockSpec((1, 1, TOP_K * ts), lambda i: (i, 0, 0), memory_space=pltpu.SMEM),
            pl.BlockSpec(memory_space=pltpu.SMEM),
            pl.BlockSpec((ts * n_sub, LANES), lambda i: (i, 0)),
        ],
        out_specs=pl.BlockSpec(memory_space=pl.ANY),
        scratch_shapes=[pltpu.VMEM((n_sub, LANES), F32), pltpu.SemaphoreType.DMA(()), pltpu.SemaphoreType.DMA(())],
        compiler_params=pltpu.CompilerParams(
            dimension_semantics=("arbitrary",), vmem_limit_bytes=VMEM_LIMIT_BYTES),
        name="scatter_rows",
    )(_tile_indices(dest, ts), pad_slots, hn)


def _combine_kernel(tc, final_norm, dest_ref, dest_next_ref, x_ref, route_ref, fw_ref, ybuf_ref, out_ref, rows, sems):
    n_sub = x_ref.shape[1] // LANES
    i = pl.program_id(0)
    slot = i % 2

    def gather(idx_ref, s):
        def issue(rb, carry):
            for u in range(DMA_UNROLL):
                r = rb * DMA_UNROLL + u
                for k in range(TOP_K):
                    _row_copy(n_sub, ybuf_ref, idx_ref[0, 0, k * tc + r], rows.at[s, k], r * n_sub,
                              sems.at[s]).start(priority=k % 2)
            return carry

        lax.fori_loop(0, tc // DMA_UNROLL, issue, 0)

    @pl.when(i == 0)
    def _():
        gather(dest_ref, 0)

    @pl.when(i + 1 < pl.num_programs(0))
    def _():
        gather(dest_next_ref, 1 - slot)

    for k in range(TOP_K):
        pltpu.make_async_copy(ybuf_ref.at[pl.ds(0, tc * n_sub)], rows.at[slot, k], sems.at[slot]).wait()

    route = route_ref[...]
    y = x_ref[...]
    for k in range(TOP_K):
        y = y + route[:, TOP_K + k:TOP_K + k + 1] * _tiles_to_rows(rows.at[slot, k], n_sub)
    out_ref[...] = _rmsnorm(y, fw_ref[...]) if final_norm else y


def _combine(x2, route, dest, ybuf, final_w, tc):
    t, d = x2.shape
    assert t % tc == 0 and tc % DMA_UNROLL == 0
    n_tiles = t // tc
    final_norm = final_w is not None
    fw = (final_w if final_norm else jnp.ones((d,), F32)).reshape(1, d).astype(F32)
    idx = _tile_indices(dest, tc)
    return pl.pallas_call(
        functools.partial(_combine_kernel, tc, final_norm),
        out_shape=jax.ShapeDtypeStruct((t, d), F32),
        grid=(n_tiles,),
        in_specs=[
            pl.BlockSpec((1, 1, TOP_K * tc), lambda i: (i, 0, 0), memory_space=pltpu.SMEM),
            pl.BlockSpec((1, 1, TOP_K * tc), lambda i: (jnp.minimum(i + 1, n_tiles - 1), 0, 0), memory_space=pltpu.SMEM),
            pl.BlockSpec((tc, d), lambda i: (i, 0)),
            pl.BlockSpec((tc, LANES), lambda i: (i, 0)),
            pl.BlockSpec((1, d), lambda i: (0, 0)),
            pl.BlockSpec(memory_space=pl.ANY),
        ],
        out_specs=pl.BlockSpec((tc, d), lambda i: (i, 0)),
        scratch_shapes=[pltpu.VMEM((2, TOP_K, tc * (d // LANES), LANES), F32), pltpu.SemaphoreType.DMA((2,))],
        compiler_params=pltpu.CompilerParams(
            dimension_semantics=("arbitrary",), vmem_limit_bytes=VMEM_LIMIT_BYTES),
        name="combine",
    )(idx, idx, x2, route, fw, ybuf)


def _moe_layer(x2, norm_w, w_router, w_gate, w_up, w_down, final_w, tiles):
    t, d = x2.shape
    bm = tiles["moe_bm"]
    tm = tiles["router_tm"]
    hn, route, route_t, counts = _router(x2, norm_w, w_router, tm)
    counts = counts[::SUBLANES, :N_EXPERTS].astype(jnp.int32)
    sizes = jnp.sum(counts, axis=0)
    padded = ((sizes + bm - 1) // bm) * bm
    ends = jnp.cumsum(padded)
    starts = ends - padded
    tile_base = starts[None, :] + jnp.cumsum(counts, axis=0) - counts
    e_idx = route_t[0:TOP_K].astype(jnp.int32).reshape(TOP_K, t // tm, tm)
    rank = route_t[2 * TOP_K:3 * TOP_K].astype(jnp.int32).reshape(TOP_K, t // tm, tm)
    base = jnp.zeros_like(rank)
    for e in range(N_EXPERTS):
        base = jnp.where(e_idx == e, tile_base[None, :, e:e + 1], base)
    dest = (base + rank).reshape(TOP_K, t)
    cap = TOP_K * t + N_EXPERTS * bm
    n_blocks = cap // bm
    first_slot = jnp.arange(n_blocks, dtype=jnp.int32) * bm
    blk_e = jnp.minimum(jnp.sum(first_slot[:, None] >= ends[None, :], axis=1), N_EXPERTS - 1).astype(jnp.int32)
    nact = (ends[-1:] // bm).astype(jnp.int32)

    n_sub = d // LANES
    dest_rows = dest * n_sub
    pad_slots = jnp.concatenate([starts + sizes, ends[-1:], padded - sizes, cap - ends[-1:]]).astype(jnp.int32)
    buf = _scatter_rows(hn, dest_rows, pad_slots, cap, tiles["scatter_ts"], n_sub)
    ybuf = _expert_ffn(buf, blk_e, nact, w_gate, w_up, w_down, bm, tiles["moe_tf"])
    return _combine(x2, route, dest_rows, ybuf, final_w, tiles["combine_tc"])


TILES = {
    "mixer_tl": 1024,
    "ffn_tm": 512,
    "router_tm": 1024,
    "moe_bm": 512, "moe_tf": 1792,
    "scatter_ts": 1024,
    "combine_tc": 512,
}


def kernel(x, norm_mix, w_in, conv_a_w, conv_ssd_w, conv_ssd_b, dt_bias, a_log, d_skip, ssd_norm_w, conv_conf_w, conv_conf_b, conf_ln_g, conf_ln_b, w_out, norm_ffn, ffn_w_gate, ffn_w_up, ffn_w_down, moe_router, moe_w_gate, moe_w_up, moe_w_down, norm_final):
    bsz, seq, d = x.shape
    depth = norm_mix.shape[0]
    tiles = dict(TILES)
    tiles["mixer_tl"] = min(tiles["mixer_tl"], seq)
    expert_w = None
    for i in range(depth):
        x = _mixer_layer(x, norm_mix[i], w_in[i], conv_a_w[i], conv_ssd_w[i], conv_ssd_b[i], dt_bias[i], a_log[i],
                         d_skip[i], ssd_norm_w[i], conv_conf_w[i], conv_conf_b[i], conf_ln_g[i], conf_ln_b[i],
                         w_out[i], tiles["mixer_tl"])
        x2 = x.reshape(bsz * seq, d)
        final_w = norm_final if i == depth - 1 else None
        j = i // 2
        if i % 2 == 0:
            later = (moe_w_gate[j], moe_w_up[j], moe_w_down[j]) if i + 1 < depth else ()
            x2, expert_w = _dense_ffn(x2, norm_ffn[i], ffn_w_gate[j], ffn_w_up[j], ffn_w_down[j], final_w,
                                      tiles["ffn_tm"], later)
        else:
            if not expert_w:
                expert_w = [w.astype(BF16) for w in (moe_w_gate[j], moe_w_up[j], moe_w_down[j])]
            x2 = _moe_layer(x2, norm_ffn[i], moe_router[j], *expert_w, final_w, tiles)
            expert_w = None
        x = x2.reshape(bsz, seq, d)
    return x
```
